```python
import jax, jax.numpy as jnp
from jax import lax
import numpy as np

D_MODEL = 1024
BATCH = 4
SEQ = 4096
DEPTH = 1
DEC_BATCH = 16
DEC_SEQ = 32
PAST_LEN = 4096

CHUNK = 64
N_HEADS = 16
Q_LORA = 256
KV_LORA = 128
NOPE_DIM = 64
ROPE_DIM = 32
V_DIM = 64
QK_DIM = NOPE_DIM + ROPE_DIM
ROPE_THETA = 10000.0
Q_BLOCK = 128
LRU_WIDTH = D_MODEL
LRU_BLOCKS = 8
LRU_BLOCK_W = LRU_WIDTH // LRU_BLOCKS
LRU_CONV_W = 4
RG_C = 8.0
D_FF = 2816
FFN_CONV_W = 3
EPS = 1e-6
N_IN = Q_LORA + KV_LORA + ROPE_DIM + LRU_WIDTH + 2 * D_MODEL
IN_SPLITS = (Q_LORA,
             Q_LORA + KV_LORA,
             Q_LORA + KV_LORA + ROPE_DIM,
             Q_LORA + KV_LORA + ROPE_DIM + LRU_WIDTH,
             Q_LORA + KV_LORA + ROPE_DIM + LRU_WIDTH + D_MODEL)

kernel_name = 'hybrid_mla_rglru_convffn_stream_step'


def rms_norm(x, g):
    xf = x.astype(jnp.float32)
    y = xf * lax.rsqrt(jnp.mean(xf * xf, axis=-1, keepdims=True) + EPS)
    return (y * g.astype(jnp.float32)).astype(x.dtype)


def rope_tables(pos):
    inv = ROPE_THETA ** (-jnp.arange(0, ROPE_DIM, 2, dtype=jnp.float32) / ROPE_DIM)
    ang = pos.astype(jnp.float32)[:, None] * inv[None, :]
    return jnp.cos(ang), jnp.sin(ang)


def apply_rope(x, cos, sin):
    half = x.shape[-1] // 2
    x1, x2 = x[..., :half], x[..., half:]
    c = cos.astype(x.dtype)
    s = sin.astype(x.dtype)
    return jnp.concatenate([x1 * c - x2 * s, x1 * s + x2 * c], axis=-1)


def causal_dwconv(x, buf, w, b):
    width = w.shape[0]
    t = x.shape[1]
    xc = jnp.concatenate([buf.astype(x.dtype), x], axis=1)
    y = xc[:, 0:t] * w[0]
    for j in range(1, width):
        y = y + xc[:, j:j + t] * w[j]
    return y + b, xc[:, xc.shape[1] - (width - 1):]


def chunk_causal_mla(q_nope, q_rope, k_nope, k_rope, v, q_pos, k_pos):
    b, t, h, _ = q_nope.shape
    qb = t if t <= Q_BLOCK else Q_BLOCK
    nb = t // qb
    scale = QK_DIM ** -0.5
    k_chunk = k_pos // CHUNK

    def to_blocks(a):
        return jnp.moveaxis(a.reshape((b, nb, qb) + a.shape[2:]), 1, 0)

    def one_block(args):
        qn, qr, qp = args
        s = (jnp.einsum('bqhd,bkhd->bhqk', qn, k_nope, preferred_element_type=jnp.float32)
             + jnp.einsum('bqhr,bkr->bhqk', qr, k_rope, preferred_element_type=jnp.float32))
        visible = k_chunk[None, :] <= (qp // CHUNK)[:, None]
        s = jnp.where(visible[None, None], s * scale, -jnp.inf)
        p = jax.nn.softmax(s, axis=-1).astype(v.dtype)
        return jnp.einsum('bhqk,bkhd->bqhd', p, v)

    o = lax.map(one_block, (to_blocks(q_nope), to_blocks(q_rope), q_pos.reshape(nb, qb)))
    return jnp.moveaxis(o, 0, 1).reshape(b, t, h * v.shape[-1])


def rg_lru(u, h0, w_rg, b_rg, w_ig, b_ig, lam):
    b, t, c = u.shape
    ub = u.reshape(b, t, LRU_BLOCKS, LRU_BLOCK_W)
    r = jax.nn.sigmoid(jnp.einsum('btnc,ncd->btnd', ub, w_rg).reshape(b, t, c) + b_rg)
    i = jax.nn.sigmoid(jnp.einsum('btnc,ncd->btnd', ub, w_ig).reshape(b, t, c) + b_ig)
    log_a = (-RG_C * r.astype(jnp.float32)) * jax.nn.softplus(-lam.astype(jnp.float32))
    a = jnp.exp(log_a)
    mult = jnp.sqrt(-jnp.expm1(2.0 * log_a))
    bt = mult * (i * u).astype(jnp.float32)
    bt = bt.at[:, 0].add(a[:, 0] * h0.astype(jnp.float32))

    def combine(left, right):
        a_l, b_l = left
        a_r, b_r = right
        return a_l * a_r, a_r * b_l + b_r

    _, h = lax.associative_scan(combine, (a, bt), axis=1)
    return h.astype(u.dtype), h[:, -1].astype(h0.dtype)


def hybrid_layer(x, q_pos, past_lat, past_kr, h0, lru_buf, ffn_buf,
                 g_mix_norm, w_in, g_q_a, w_q_b, g_kv_a, w_kv_b, g_qn, g_qr, g_kn, g_kr,
                 lru_conv_w, lru_conv_b, w_rg, b_rg, w_ig, b_ig, lru_lambda,
                 w_out, g_ffn_norm, w_up, ffn_conv_w, ffn_conv_b, w_down):
    b, t, _ = x.shape
    xn = rms_norm(x, g_mix_norm)
    z = xn @ w_in
    c_q, c_kv, k_r, u, gate_a, gate_b = jnp.split(z, IN_SPLITS, axis=-1)
    cos, sin = rope_tables(q_pos)
    q = (rms_norm(c_q, g_q_a) @ w_q_b).reshape(b, t, N_HEADS, QK_DIM)
    q_nope = rms_norm(q[..., :NOPE_DIM], g_qn)
    q_rope = apply_rope(rms_norm(q[..., NOPE_DIM:], g_qr), cos[:, None, :], sin[:, None, :])
    c_kv = rms_norm(c_kv, g_kv_a)
    k_r = apply_rope(rms_norm(k_r, g_kr), cos, sin)
    lat_all = jnp.concatenate([past_lat.astype(x.dtype), c_kv], axis=1)
    kr_all = jnp.concatenate([past_kr.astype(x.dtype), k_r], axis=1)
    n_k = lat_all.shape[1]
    kv = (lat_all @ w_kv_b).reshape(b, n_k, N_HEADS, NOPE_DIM + V_DIM)
    k_nope = rms_norm(kv[..., :NOPE_DIM], g_kn)
    v = kv[..., NOPE_DIM:]
    attn = chunk_causal_mla(q_nope, q_rope, k_nope, kr_all, v, q_pos,
                            jnp.arange(n_k, dtype=jnp.int32))
    u_c, lru_buf_new = causal_dwconv(u, lru_buf, lru_conv_w, lru_conv_b)
    rec, h_last = rg_lru(u_c, h0, w_rg, b_rg, w_ig, b_ig, lru_lambda)
    mixed = jax.nn.sigmoid(gate_a) * attn + jax.nn.sigmoid(gate_b) * rec
    x = x + mixed @ w_out
    up = rms_norm(x, g_ffn_norm) @ w_up
    up_c, ffn_buf_new = causal_dwconv(up, ffn_buf, ffn_conv_w, ffn_conv_b)
    gate, val = jnp.split(up_c, 2, axis=-1)
    x = x + (jax.nn.silu(gate) * val) @ w_down
    return x, (c_kv, k_r, h_last, lru_buf_new, ffn_buf_new)


def setup_inputs(seed: int = 0) -> dict:
    key = jax.random.key(seed)
    ks = jax.random.split(key, 40)
    f32 = jnp.float32

    def nrm(k, shape, scale):
        return jax.random.normal(k, shape, f32) * scale

    def gain(k, shape):
        return 1.0 + 0.01 * jax.random.normal(k, shape, f32)

    a0 = jax.random.uniform(ks[29], (DEPTH, LRU_WIDTH), f32, minval=0.9, maxval=0.999)
    return {
        'x_prompt': nrm(ks[0], (BATCH, SEQ, D_MODEL), 1.0),
        'x_sample': nrm(ks[1], (DEC_BATCH, DEC_SEQ, D_MODEL), 1.0),
        'cache_kv_latent': nrm(ks[2], (DEPTH, DEC_BATCH, PAST_LEN, KV_LORA), 1.0),
        'cache_k_rope': nrm(ks[3], (DEPTH, DEC_BATCH, PAST_LEN, ROPE_DIM), 1.0),
        'state_lru_h': nrm(ks[4], (DEPTH, DEC_BATCH, LRU_WIDTH), 0.5),
        'state_lru_conv': nrm(ks[5], (DEPTH, DEC_BATCH, LRU_CONV_W - 1, LRU_WIDTH), 1.0),
        'state_ffn_conv': nrm(ks[6], (DEPTH, DEC_BATCH, FFN_CONV_W - 1, 2 * D_FF), 1.0),
        'g_mix_norm': gain(ks[7], (DEPTH, D_MODEL)),
        'w_in': nrm(ks[8], (DEPTH, D_MODEL, N_IN), D_MODEL ** -0.5),
        'g_q_a': gain(ks[9], (DEPTH, Q_LORA)),
        'w_q_b': nrm(ks[10], (DEPTH, Q_LORA, N_HEADS * QK_DIM), Q_LORA ** -0.5),
        'g_kv_a': gain(ks[11], (DEPTH, KV_LORA)),
        'w_kv_b': nrm(ks[12], (DEPTH, KV_LORA, N_HEADS * (NOPE_DIM + V_DIM)), KV_LORA ** -0.5),
        'g_qn': gain(ks[13], (DEPTH, NOPE_DIM)),
        'g_qr': gain(ks[14], (DEPTH, ROPE_DIM)),
        'g_kn': gain(ks[15], (DEPTH, NOPE_DIM)),
        'g_kr': gain(ks[16], (DEPTH, ROPE_DIM)),
        'lru_conv_w': nrm(ks[17], (DEPTH, LRU_CONV_W, LRU_WIDTH), LRU_CONV_W ** -0.5),
        'lru_conv_b': nrm(ks[18], (DEPTH, LRU_WIDTH), 0.01),
        'w_rg': nrm(ks[19], (DEPTH, LRU_BLOCKS, LRU_BLOCK_W, LRU_BLOCK_W), LRU_BLOCK_W ** -0.5),
        'b_rg': nrm(ks[20], (DEPTH, LRU_WIDTH), 0.01),
        'w_ig': nrm(ks[21], (DEPTH, LRU_BLOCKS, LRU_BLOCK_W, LRU_BLOCK_W), LRU_BLOCK_W ** -0.5),
        'b_ig': nrm(ks[22], (DEPTH, LRU_WIDTH), 0.01),
        'lru_lambda': jnp.log(a0) - jnp.log1p(-a0),
        'w_out': nrm(ks[23], (DEPTH, D_MODEL, D_MODEL), D_MODEL ** -0.5),
        'g_ffn_norm': gain(ks[24], (DEPTH, D_MODEL)),
        'w_up': nrm(ks[25], (DEPTH, D_MODEL, 2 * D_FF), D_MODEL ** -0.5),
        'ffn_conv_w': nrm(ks[26], (DEPTH, FFN_CONV_W, 2 * D_FF), FFN_CONV_W ** -0.5),
        'ffn_conv_b': nrm(ks[27], (DEPTH, 2 * D_FF), 0.01),
        'w_down': nrm(ks[28], (DEPTH, D_FF, D_MODEL), D_FF ** -0.5),
    }


def reference(x_prompt, x_sample, cache_kv_latent, cache_k_rope, state_lru_h, state_lru_conv,
              state_ffn_conv, g_mix_norm, w_in, g_q_a, w_q_b, g_kv_a, w_kv_b, g_qn, g_qr, g_kn, g_kr,
              lru_conv_w, lru_conv_b, w_rg, b_rg, w_ig, b_ig, lru_lambda, w_out, g_ffn_norm,
              w_up, ffn_conv_w, ffn_conv_b, w_down):
    bp, tp = x_prompt.shape[0], x_prompt.shape[1]
    ts = x_sample.shape[1]
    past = cache_kv_latent.shape[2]
    dt = x_prompt.dtype
    pos_p = jnp.arange(tp, dtype=jnp.int32)
    pos_s = past + jnp.arange(ts, dtype=jnp.int32)
    yp, ys = x_prompt, x_sample
    p_states = [[], [], [], [], []]
    s_states = [[], [], [], [], []]
    for l in range(DEPTH):
        lp = (g_mix_norm[l], w_in[l], g_q_a[l], w_q_b[l], g_kv_a[l], w_kv_b[l], g_qn[l], g_qr[l],
              g_kn[l], g_kr[l], lru_conv_w[l], lru_conv_b[l], w_rg[l], b_rg[l], w_ig[l], b_ig[l],
              lru_lambda[l], w_out[l], g_ffn_norm[l], w_up[l], ffn_conv_w[l], ffn_conv_b[l], w_down[l])
        yp, st_p = hybrid_layer(yp, pos_p,
                                jnp.zeros((bp, 0, KV_LORA), dt), jnp.zeros((bp, 0, ROPE_DIM), dt),
                                jnp.zeros((bp, LRU_WIDTH), dt),
                                jnp.zeros((bp, LRU_CONV_W - 1, LRU_WIDTH), dt),
                                jnp.zeros((bp, FFN_CONV_W - 1, 2 * D_FF), dt), *lp)
        ys, st_s = hybrid_layer(ys, pos_s, cache_kv_latent[l], cache_k_rope[l], state_lru_h[l],
                                state_lru_conv[l], state_ffn_conv[l], *lp)
        for j in range(5):
            p_states[j].append(st_p[j])
            s_states[j].append(st_s[j])
    p_lat, p_kr, p_h, p_conv, p_ffn = [jnp.stack(s, axis=0) for s in p_states]
    s_lat, s_kr, s_h, s_conv, s_ffn = [jnp.stack(s, axis=0) for s in s_states]
    return (yp, ys, p_lat, p_kr, p_h, p_conv, p_ffn, s_lat, s_kr, s_h, s_conv, s_ffn)
```

```python
import functools

import jax
import jax.numpy as jnp
from jax import lax
from jax.experimental import pallas as pl
from jax.experimental.pallas import tpu as pltpu

CHUNK = 64
CHUNK_SHIFT = CHUNK.bit_length() - 1
assert CHUNK == 1 << CHUNK_SHIFT
N_HEADS = 16
NOPE_DIM = 64
ROPE_DIM = 32
V_DIM = 64
QK_DIM = NOPE_DIM + ROPE_DIM
ROPE_THETA = 10000.0
RG_C = 8.0
EPS = 1e-6

LANES = 128
SUBLANES = 8
MXU_DIM = 256
VMEM_BYTES_V7X = 64 * 1024 * 1024

HEAD_PAD = LANES
HALF_ROPE = ROPE_DIM // 2
F32 = jnp.float32
BF16 = jnp.bfloat16


def _round_up(n, m):
    return (n + m - 1) // m * m


def _vmem_limit(nbytes):
    return int(min(2 * nbytes, VMEM_BYTES_V7X - 8 * 1024 * 1024))


def _const_spec(shape):
    nd = len(shape)
    return pl.BlockSpec(shape, lambda *_: (0,) * nd, pipeline_mode=pl.Buffered(1))


def _dot(a, b):
    return jnp.dot(a, b, preferred_element_type=F32)


def _rms_rows(x, n):
    return lax.rsqrt(jnp.sum(x * x, axis=-1, keepdims=True) * (1.0 / n) + EPS)


def _rope_head(x, c, s_lo, s_hi):
    return (x * c + pltpu.roll(x, HALF_ROPE, 1) * s_hi
            + pltpu.roll(x, HEAD_PAD - HALF_ROPE, 1) * s_lo)


def _shift_rows(x, n, fill):
    rows, cols = x.shape
    if n % SUBLANES == 0:
        return jnp.concatenate([jnp.full((n, cols), fill, x.dtype), x[:rows - n]], axis=0)
    rolled = pltpu.roll(x, n, 0)
    row = lax.broadcasted_iota(jnp.int32, x.shape, 0)
    return jnp.where(row >= n, rolled, fill)


def _linear_scan(a, b, nb):
    rows = a.shape[0]
    s = nb
    while s < rows:
        b = a * _shift_rows(b, s, 0.0) + b
        if 2 * s < rows:
            a = a * _shift_rows(a, s, 1.0)
        s *= 2
    return b


def _mixer_in_kernel(x_ref, hist_ref, h0_ref, cos_ref, slo_ref, shi_ref,
                     g_mix_ref, w_in_ref, g_qa_ref, wq_ref, gq_ref, g_kva_ref, g_kr_ref,
                     wk_ref, gk_ref, wv_ref, cw_ref, cb_ref, wlru_ref, brg_ref, big_ref, lam_ref,
                     q_ref, k_ref, v_ref, lat_ref, kr_ref, ga_ref, rg_ref, hl_ref, cst_ref,
                     xc_ref, hprev_ref, *, rows, nb, hp, d_model, q_lora, kv_lora, conv_w):
    t = pl.program_id(1)

    @pl.when(t == 0)
    def _():
        xc_ref[0:hp, :] = hist_ref[...]
        hprev_ref[...] = h0_ref[...]

    x = x_ref[...]
    xn = (x * _rms_rows(x, d_model) * g_mix_ref[...]).astype(BF16)
    z = _dot(xn, w_in_ref[...])
    o_kv = q_lora
    o_kr = o_kv + kv_lora
    o_u = o_kr + HEAD_PAD
    o_ga = o_u + d_model
    o_gb = o_ga + d_model
    cq = z[:, 0:o_kv]
    ckv = z[:, o_kv:o_kr]
    krb = z[:, o_kr:o_u]
    u = z[:, o_u:o_ga]
    gate_a = z[:, o_ga:o_gb]
    gate_b = z[:, o_gb:o_gb + d_model]

    cos = cos_ref[...]
    s_lo = slo_ref[...]
    s_hi = shi_ref[...]
    lane = lax.broadcasted_iota(jnp.int32, (rows, HEAD_PAD), 1)
    is_nope = lane < NOPE_DIM

    lat = ckv * _rms_rows(ckv, kv_lora) * g_kva_ref[...]
    lat_ref[...] = lat
    kr = _rope_head(krb * _rms_rows(krb, ROPE_DIM) * g_kr_ref[...], cos, s_lo, s_hi)
    kr_ref[...] = kr[:, NOPE_DIM:NOPE_DIM + ROPE_DIM]

    cqn = (cq * _rms_rows(cq, q_lora) * g_qa_ref[...]).astype(BF16)
    qp = _dot(cqn, wq_ref[...])
    for h in range(N_HEADS):
        sl = slice(h * HEAD_PAD, (h + 1) * HEAD_PAD)
        qh = qp[:, sl]
        sq = qh * qh
        r_n = lax.rsqrt(jnp.sum(jnp.where(is_nope, sq, 0.0), axis=-1, keepdims=True) * (1.0 / NOPE_DIM) + EPS)
        r_r = lax.rsqrt(jnp.sum(jnp.where(is_nope, 0.0, sq), axis=-1, keepdims=True) * (1.0 / ROPE_DIM) + EPS)
        qh = qh * jnp.where(is_nope, r_n, r_r) * gq_ref[:, sl]
        q_ref[:, sl] = _rope_head(qh, cos, s_lo, s_hi).astype(BF16)

    latb = lat.astype(BF16)
    kp = _dot(latb, wk_ref[...])
    for h in range(N_HEADS):
        sl = slice(h * HEAD_PAD, (h + 1) * HEAD_PAD)
        kh = kp[:, sl]
        k_ref[:, sl] = (kh * _rms_rows(kh, NOPE_DIM) * gk_ref[:, sl] + kr).astype(BF16)
    v_ref[...] = _dot(latb, wv_ref[...]).astype(BF16)

    xc_ref[hp:hp + rows, :] = u
    u_c = cb_ref[...] + cw_ref[conv_w - 1:conv_w, :] * u
    for j in range(1, conv_w):
        u_c = u_c + cw_ref[conv_w - 1 - j:conv_w - j, :] * xc_ref[hp - j * nb:hp - j * nb + rows, :]
    tail = xc_ref[rows:rows + hp, :]
    cst_ref[...] = tail
    xc_ref[0:hp, :] = tail

    lam = lam_ref[...]
    softplus_neg = jnp.maximum(-lam, 0.0) + jnp.log1p(jnp.exp(-jnp.abs(lam)))
    n_blocks, blk_w, _ = wlru_ref.shape
    a_parts, b_parts = [], []
    for n in range(n_blocks):
        sl = slice(n * blk_w, (n + 1) * blk_w)
        ucn = u_c[:, sl]
        g = _dot(ucn.astype(BF16), wlru_ref[n])
        r = jax.nn.sigmoid(g[:, 0:blk_w] + brg_ref[:, sl])
        i = jax.nn.sigmoid(g[:, blk_w:2 * blk_w] + big_ref[:, sl])
        log_a = (-RG_C * r) * softplus_neg[:, sl]
        a_n = jnp.exp(log_a)
        a_parts.append(a_n)
        b_parts.append(jnp.sqrt(-jnp.tanh(log_a) * (1.0 + a_n * a_n)) * (i * ucn))
    a = jnp.concatenate(a_parts, axis=1)
    b = jnp.concatenate(b_parts, axis=1)
    if nb == 1:
        first = lax.broadcasted_iota(jnp.int32, a.shape, 0) < nb
        b = b + jnp.where(first, a * hprev_ref[...], 0.0)
    else:
        b = jnp.concatenate([b[0:nb] + a[0:nb] * hprev_ref[...], b[nb:]], axis=0)
    hseq = _linear_scan(a, b, nb)
    h_last = hseq[rows - nb:rows]
    hprev_ref[...] = h_last
    hl_ref[...] = h_last

    ga_ref[...] = jax.nn.sigmoid(gate_a).astype(BF16)
    rg_ref[...] = (jax.nn.sigmoid(gate_b) * hseq).astype(BF16)


def _mixer_in(x, hist, h0, rope, wts, *, rows, nb):
    groups, tg, d_model = x.shape
    hp = hist.shape[1]
    q_lora = wts["wq"].shape[0]
    kv_lora = wts["wk"].shape[0]
    conv_w = wts["cw"].shape[0]
    n_t = tg // rows
    qk_w = N_HEADS * HEAD_PAD
    v_w = N_HEADS * V_DIM

    def row_spec(width):
        return pl.BlockSpec((None, rows, width), lambda g, t: (g, t, 0))

    def group_spec(r, width):
        return pl.BlockSpec((None, r, width), lambda g, t: (g, 0, 0))

    tab_spec = pl.BlockSpec((rows, HEAD_PAD), lambda g, t: (t, 0))
    w_names = ["g_mix", "w_in", "g_qa", "wq", "gq", "g_kva", "g_kr", "wk", "gk", "wv",
               "cw", "cb", "wlru", "brg", "big", "lam"]
    w_list = [wts[n] for n in w_names]
    in_specs = ([row_spec(d_model), group_spec(hp, d_model), group_spec(nb, d_model),
                 tab_spec, tab_spec, tab_spec] + [_const_spec(w.shape) for w in w_list])
    out_shape = (
        jax.ShapeDtypeStruct((groups, tg, qk_w), BF16),
        jax.ShapeDtypeStruct((groups, tg, qk_w), BF16),
        jax.ShapeDtypeStruct((groups, tg, v_w), BF16),
        jax.ShapeDtypeStruct((groups, tg, kv_lora), F32),
        jax.ShapeDtypeStruct((groups, tg, ROPE_DIM), F32),
        jax.ShapeDtypeStruct((groups, tg, d_model), BF16),
        jax.ShapeDtypeStruct((groups, tg, d_model), BF16),
        jax.ShapeDtypeStruct((groups, nb, d_model), F32),
        jax.ShapeDtypeStruct((groups, hp, d_model), F32),
    )
    out_specs = (row_spec(qk_w), row_spec(qk_w), row_spec(v_w), row_spec(kv_lora), row_spec(ROPE_DIM),
                 row_spec(d_model), row_spec(d_model), group_spec(nb, d_model), group_spec(hp, d_model))
    w_bytes = sum(w.size * w.dtype.itemsize for w in w_list)
    io_bytes = 2 * rows * (4 * d_model + 2 * (2 * qk_w + v_w + 2 * d_model) + 4 * (kv_lora + LANES) + 12 * LANES)
    tmp_bytes = 4 * rows * (wts["w_in"].shape[1] + 2 * qk_w + 6 * d_model) + 4 * (hp + rows) * d_model
    kern = functools.partial(_mixer_in_kernel, rows=rows, nb=nb, hp=hp, d_model=d_model,
                             q_lora=q_lora, kv_lora=kv_lora, conv_w=conv_w)
    return pl.pallas_call(
        kern,
        grid=(groups, n_t),
        in_specs=in_specs,
        out_specs=out_specs,
        out_shape=out_shape,
        scratch_shapes=[pltpu.VMEM((hp + rows, d_model), F32), pltpu.VMEM((nb, d_model), F32)],
        compiler_params=pltpu.CompilerParams(
            dimension_semantics=("parallel", "arbitrary"),
            vmem_limit_bytes=_vmem_limit(w_bytes + io_bytes + tmp_bytes)),
        name="mixer_in",
    )(x, hist, h0, *rope, *w_list)


def _last_key_tile(qi, tq, tk, q_off):
    last_q = q_off + (qi + 1) * tq - 1
    return ((last_q // CHUNK) * CHUNK + CHUNK - 1) // tk


def _attn_kernel(q_ref, k_ref, v_ref, ga_ref, rg_ref, o_ref, m_ref, l_ref, acc_ref,
                 *, tq, tk, q_off, n_k, n_kt):
    qi = pl.program_id(2)
    ki = pl.program_id(3)

    @pl.when(ki == 0)
    def _():
        m_ref[...] = jnp.full(m_ref.shape, -jnp.inf, F32)
        l_ref[...] = jnp.zeros(l_ref.shape, F32)
        acc_ref[...] = jnp.zeros(acc_ref.shape, F32)

    first_head = lax.broadcasted_iota(jnp.int32, (tq, LANES), 1) < V_DIM

    @pl.when(ki <= jnp.minimum(_last_key_tile(qi, tq, tk, q_off), n_kt - 1))
    def _():
        q_pos = q_off + qi * tq + lax.broadcasted_iota(jnp.int32, (tq, tk), 0)
        k_pos = ki * tk + lax.broadcasted_iota(jnp.int32, (tq, tk), 1)
        visible = jnp.logical_and((k_pos >> CHUNK_SHIFT) <= (q_pos >> CHUNK_SHIFT), k_pos < n_k)
        v = v_ref[...]
        alphas, pvs = [], []
        for hh in range(2):
            q = q_ref[:, hh * HEAD_PAD:(hh + 1) * HEAD_PAD]
            k = k_ref[:, hh * HEAD_PAD:(hh + 1) * HEAD_PAD]
            s = lax.dot_general(q, k, (((1,), (1,)), ((), ())), preferred_element_type=F32)
            s = jnp.where(visible, s, -jnp.inf)
            m_prev = m_ref[hh]
            m_new = jnp.maximum(m_prev, jnp.max(s, axis=-1, keepdims=True))
            alpha = jnp.exp(m_prev - m_new)
            p = jnp.exp(s - m_new)
            l_ref[hh] = alpha * l_ref[hh] + jnp.sum(p, axis=-1, keepdims=True)
            m_ref[hh] = m_new
            alphas.append(alpha)
            pvs.append(_dot(p.astype(BF16), v))
        acc_ref[...] = (jnp.where(first_head, alphas[0], alphas[1]) * acc_ref[...]
                        + jnp.where(first_head, pvs[0], pvs[1]))

    @pl.when(ki == n_kt - 1)
    def _():
        attn = acc_ref[...] / jnp.where(first_head, l_ref[0], l_ref[1])
        o_ref[...] = (ga_ref[...].astype(F32) * attn + rg_ref[...].astype(F32)).astype(BF16)


def _attention(q, k, v, ga, rg, *, tq, tk, q_off, n_k):
    bsz, t_q, _ = q.shape
    t_k = k.shape[1]
    assert LANES == 2 * V_DIM and N_HEADS % 2 == 0
    n_hp = N_HEADS // 2
    n_qt = t_q // tq
    n_kt = t_k // tk
    qk_blk = 2 * HEAD_PAD

    def k_index(b, p, qi, ki):
        return (b, jnp.minimum(ki, _last_key_tile(qi, tq, tk, q_off)), p)

    def q_index(b, p, qi, ki):
        return (b, qi, p)

    kern = functools.partial(_attn_kernel, tq=tq, tk=tk, q_off=q_off, n_k=n_k, n_kt=n_kt)
    blk_bytes = 2 * 2 * (tq * qk_blk + tk * qk_blk + tk * LANES + 3 * tq * LANES)
    tmp_bytes = 4 * 4 * tq * tk + 4 * 5 * tq * LANES
    return pl.pallas_call(
        kern,
        grid=(bsz, n_hp, n_qt, n_kt),
        in_specs=[pl.BlockSpec((None, tq, qk_blk), q_index),
                  pl.BlockSpec((None, tk, qk_blk), k_index),
                  pl.BlockSpec((None, tk, LANES), k_index),
                  pl.BlockSpec((None, tq, LANES), q_index),
                  pl.BlockSpec((None, tq, LANES), q_index)],
        out_specs=pl.BlockSpec((None, tq, LANES), q_index),
        out_shape=jax.ShapeDtypeStruct((bsz, t_q, N_HEADS * V_DIM), BF16),
        scratch_shapes=[pltpu.VMEM((2, tq, 1), F32),
                        pltpu.VMEM((2, tq, 1), F32),
                        pltpu.VMEM((tq, LANES), F32)],
        compiler_params=pltpu.CompilerParams(
            dimension_semantics=("parallel", "parallel", "parallel", "arbitrary"),
            vmem_limit_bytes=_vmem_limit(blk_bytes + tmp_bytes)),
        name="attention",
    )(q, k, v, ga, rg)


def _kv_up_kernel(lat_ref, kr_ref, wk_ref, gk_ref, wv_ref, k_ref, v_ref):
    latb = lat_ref[...].astype(BF16)
    kr = kr_ref[...]
    kp = _dot(latb, wk_ref[...])
    for h in range(N_HEADS):
        sl = slice(h * HEAD_PAD, (h + 1) * HEAD_PAD)
        kh = kp[:, sl]
        k_ref[:, sl] = (kh * _rms_rows(kh, NOPE_DIM) * gk_ref[:, sl] + kr).astype(BF16)
    v_ref[...] = _dot(latb, wv_ref[...]).astype(BF16)


def _kv_up(lat, kr_pad, wts, *, rows):
    n, kv_lora = lat.shape
    qk_w = N_HEADS * HEAD_PAD
    v_w = N_HEADS * V_DIM
    w_list = [wts["wk"], wts["gk"], wts["wv"]]
    w_bytes = sum(w.size * w.dtype.itemsize for w in w_list)
    blk_bytes = 2 * rows * (4 * kv_lora + 4 * HEAD_PAD + 2 * qk_w + 2 * v_w)
    return pl.pallas_call(
        _kv_up_kernel,
        grid=(n // rows,),
        in_specs=[pl.BlockSpec((rows, kv_lora), lambda i: (i, 0)),
                  pl.BlockSpec((rows, HEAD_PAD), lambda i: (i, 0))] + [_const_spec(w.shape) for w in w_list],
        out_specs=(pl.BlockSpec((rows, qk_w), lambda i: (i, 0)), pl.BlockSpec((rows, v_w), lambda i: (i, 0))),
        out_shape=(jax.ShapeDtypeStruct((n, qk_w), BF16), jax.ShapeDtypeStruct((n, v_w), BF16)),
        compiler_params=pltpu.CompilerParams(
            dimension_semantics=("parallel",),
            vmem_limit_bytes=_vmem_limit(w_bytes + blk_bytes + 4 * rows * (qk_w + v_w))),
        name="kv_up",
    )(lat, kr_pad, *w_list)


def _mixer_out_kernel(x_ref, mix_ref, hist_ref, w_out_ref, g_ffn_ref, w_up_ref, fw_ref, fb_ref, w_down_ref,
                      y_ref, fst_ref, upc_ref, *, rows, nb, hp, d_model, d_ff, conv_w, col_blk):
    t = pl.program_id(1)

    @pl.when(t == 0)
    def _():
        upc_ref[0:hp, :] = hist_ref[...]

    x1 = x_ref[...] + _dot(mix_ref[...], w_out_ref[...])
    xn = (x1 * _rms_rows(x1, d_model) * g_ffn_ref[...]).astype(BF16)

    def conv_cols(c0):
        sl = slice(c0, c0 + col_blk)
        up = _dot(xn, w_up_ref[:, sl])
        upc_ref[hp:hp + rows, sl] = up
        out = fb_ref[:, sl] + fw_ref[conv_w - 1:conv_w, sl] * up
        for j in range(1, conv_w):
            out = out + fw_ref[conv_w - 1 - j:conv_w - j, sl] * upc_ref[hp - j * nb:hp - j * nb + rows, sl]
        return out

    y = x1
    for c in range(d_ff // col_blk):
        gate = conv_cols(c * col_blk)
        val = conv_cols(d_ff + c * col_blk)
        hmid = (gate * jax.nn.sigmoid(gate) * val).astype(BF16)
        y = y + _dot(hmid, w_down_ref[c * col_blk:(c + 1) * col_blk, :])
    y_ref[...] = y

    tail = upc_ref[rows:rows + hp, :]
    fst_ref[...] = tail
    upc_ref[0:hp, :] = tail


def _mixer_out(x, mixed, hist, wts, *, rows, nb):
    groups, tg, d_model = x.shape
    hp = hist.shape[1]
    d_ff = wts["w_down"].shape[0]
    conv_w = wts["fw"].shape[0]
    n_t = tg // rows
    col_blk = MXU_DIM

    def row_spec(width):
        return pl.BlockSpec((None, rows, width), lambda g, t: (g, t, 0))

    def group_spec(r, width):
        return pl.BlockSpec((None, r, width), lambda g, t: (g, 0, 0))

    w_names = ["w_out", "g_ffn", "w_up", "fw", "fb", "w_down"]
    w_list = [wts[n] for n in w_names]
    w_bytes = sum(w.size * w.dtype.itemsize for w in w_list)
    blk_bytes = 2 * rows * d_model * (4 + 2 + 4) + 4 * 4 * hp * 2 * d_ff
    tmp_bytes = 4 * (hp + rows) * 2 * d_ff + 4 * rows * (3 * d_model + 8 * col_blk)
    kern = functools.partial(_mixer_out_kernel, rows=rows, nb=nb, hp=hp, d_model=d_model, d_ff=d_ff,
                             conv_w=conv_w, col_blk=col_blk)
    return pl.pallas_call(
        kern,
        grid=(groups, n_t),
        in_specs=[row_spec(d_model), row_spec(d_model), group_spec(hp, 2 * d_ff)]
                 + [_const_spec(w.shape) for w in w_list],
        out_specs=(row_spec(d_model), group_spec(hp, 2 * d_ff)),
        out_shape=(jax.ShapeDtypeStruct((groups, tg, d_model), F32),
                   jax.ShapeDtypeStruct((groups, hp, 2 * d_ff), F32)),
        scratch_shapes=[pltpu.VMEM((hp + rows, 2 * d_ff), F32)],
        compiler_params=pltpu.CompilerParams(
            dimension_semantics=("parallel", "arbitrary"),
            vmem_limit_bytes=_vmem_limit(w_bytes + blk_bytes + tmp_bytes)),
        name="mixer_out",
    )(x, mixed, hist, *w_list)


def _head_pad(w, widths):
    lead = w.shape[:-1]
    per_head = sum(widths)
    w = w.reshape(lead + (N_HEADS, per_head))
    w = jnp.pad(w, [(0, 0)] * len(lead) + [(0, 0), (0, HEAD_PAD - per_head)])
    return w.reshape(lead + (N_HEADS * HEAD_PAD,))


def _prep_weights(l, g_mix_norm, w_in, g_q_a, w_q_b, g_kv_a, w_kv_b, g_qn, g_qr, g_kn, g_kr,
                  lru_conv_w, lru_conv_b, w_rg, b_rg, w_ig, b_ig, lru_lambda, w_out, g_ffn_norm,
                  w_up, ffn_conv_w, ffn_conv_b, w_down):
    d_model = w_in.shape[1]
    q_lora = w_q_b.shape[1]
    kv_lora = w_kv_b.shape[1]
    row = lambda a: a.reshape(1, -1)
    o_kr = q_lora + kv_lora
    wi = w_in[l]
    w_kr = jnp.pad(wi[:, o_kr:o_kr + ROPE_DIM], ((0, 0), (NOPE_DIM, HEAD_PAD - QK_DIM)))
    w_in_p = jnp.concatenate([wi[:, :o_kr], w_kr, wi[:, o_kr + ROPE_DIM:]], axis=1).astype(BF16)
    scale = QK_DIM ** -0.5
    gq = jnp.tile(jnp.pad(jnp.concatenate([g_qn[l], g_qr[l]]) * scale, (0, HEAD_PAD - QK_DIM)), N_HEADS)
    gk = jnp.tile(jnp.pad(g_kn[l], (0, HEAD_PAD - NOPE_DIM)), N_HEADS)
    kv = w_kv_b[l].reshape(kv_lora, N_HEADS, NOPE_DIM + V_DIM)
    wk = _head_pad(kv[:, :, :NOPE_DIM].reshape(kv_lora, N_HEADS * NOPE_DIM), (NOPE_DIM,))
    wv = kv[:, :, NOPE_DIM:].reshape(kv_lora, N_HEADS * V_DIM)
    return {
        "g_mix": row(g_mix_norm[l]), "w_in": w_in_p, "g_qa": row(g_q_a[l]),
        "wq": _head_pad(w_q_b[l], (NOPE_DIM, ROPE_DIM)).astype(BF16), "gq": row(gq),
        "g_kva": row(g_kv_a[l]), "g_kr": row(jnp.pad(g_kr[l], (NOPE_DIM, HEAD_PAD - QK_DIM))),
        "wk": wk.astype(BF16), "gk": row(gk), "wv": wv.astype(BF16),
        "cw": lru_conv_w[l], "cb": row(lru_conv_b[l]),
        "wlru": jnp.concatenate([w_rg[l], w_ig[l]], axis=-1).astype(BF16),
        "brg": row(b_rg[l]), "big": row(b_ig[l]), "lam": row(lru_lambda[l]),
        "w_out": w_out[l].astype(BF16), "g_ffn": row(g_ffn_norm[l]), "w_up": w_up[l].astype(BF16),
        "fw": ffn_conv_w[l], "fb": row(ffn_conv_b[l]), "w_down": w_down[l].astype(BF16),
    }


def _rope_tables(pos):
    inv = ROPE_THETA ** (-jnp.arange(0, ROPE_DIM, 2, dtype=F32) / ROPE_DIM)
    ang = pos.astype(F32)[:, None] * inv[None, :]
    cos, sin = jnp.cos(ang), jnp.sin(ang)
    n = pos.shape[0]
    ones_lo = jnp.ones((n, NOPE_DIM), F32)
    zeros_lo = jnp.zeros((n, NOPE_DIM), F32)
    zeros_half = jnp.zeros((n, HALF_ROPE), F32)
    tail = jnp.zeros((n, HEAD_PAD - QK_DIM), F32)
    c = jnp.concatenate([ones_lo, cos, cos, tail], axis=1)
    s_lo = jnp.concatenate([zeros_lo, -sin, zeros_half, tail], axis=1)
    s_hi = jnp.concatenate([zeros_lo, zeros_half, sin, tail], axis=1)
    return c, s_lo, s_hi


def _front_pad_rows(a, hp):
    return jnp.pad(a, ((0, 0), (hp - a.shape[1], 0), (0, 0)))


def _layer_prompt(x, wts, *, rows, tq, tk):
    bsz, t, d_model = x.shape
    lru_w = wts["cw"].shape[0]
    ffn_w = wts["fw"].shape[0]
    d_ff2 = wts["w_up"].shape[1]
    hp1 = _round_up(lru_w - 1, SUBLANES)
    hp2 = _round_up(ffn_w - 1, SUBLANES)
    rope = _rope_tables(jnp.arange(t, dtype=jnp.int32))
    q, k, v, lat, kr, ga, rg, h_last, cst = _mixer_in(
        x, jnp.zeros((bsz, hp1, d_model), F32), jnp.zeros((bsz, 1, d_model), F32), rope, wts, rows=rows, nb=1)
    mixed = _attention(q, k, v, ga, rg, tq=tq, tk=tk, q_off=0, n_k=t)
    y, fst = _mixer_out(x, mixed, jnp.zeros((bsz, hp2, d_ff2), F32), wts, rows=rows, nb=1)
    return y, (lat, kr, h_last[:, 0], cst[:, hp1 - (lru_w - 1):], fst[:, hp2 - (ffn_w - 1):])


def _layer_sample(x, past_lat, past_kr, h0, lru_buf, ffn_buf, wts, *, tk):
    bsz, t, d_model = x.shape
    past = past_lat.shape[1]
    lru_w = wts["cw"].shape[0]
    ffn_w = wts["fw"].shape[0]
    hp1 = _round_up((lru_w - 1) * bsz, SUBLANES)
    hp2 = _round_up((ffn_w - 1) * bsz, SUBLANES)
    rows = t * bsz

    def to_tm(a):
        return jnp.swapaxes(a, 0, 1).reshape(1, a.shape[1] * bsz, a.shape[2])

    def from_tm(a):
        return jnp.swapaxes(a.reshape(a.shape[1] // bsz, bsz, a.shape[2]), 0, 1)

    pos = past + jnp.arange(t, dtype=jnp.int32)
    rope = _rope_tables(jnp.repeat(pos, bsz))
    q, k, v, lat, kr, ga, rg, h_last, cst = _mixer_in(
        to_tm(x), _front_pad_rows(to_tm(lru_buf), hp1), h0[None], rope, wts, rows=rows, nb=bsz)

    kr_pad = jnp.pad(past_kr.reshape(bsz * past, ROPE_DIM), ((0, 0), (NOPE_DIM, HEAD_PAD - QK_DIM)))
    k_past, v_past = _kv_up(past_lat.reshape(bsz * past, -1), kr_pad, wts, rows=tk)
    n_k = past + t
    n_k_pad = _round_up(n_k, tk)
    k_all = jnp.concatenate([k_past.reshape(bsz, past, -1), from_tm(k)], axis=1)
    v_all = jnp.concatenate([v_past.reshape(bsz, past, -1), from_tm(v)], axis=1)
    k_all = jnp.pad(k_all, ((0, 0), (0, n_k_pad - n_k), (0, 0)))
    v_all = jnp.pad(v_all, ((0, 0), (0, n_k_pad - n_k), (0, 0)))
    mixed = _attention(from_tm(q), k_all, v_all, from_tm(ga), from_tm(rg), tq=t, tk=tk, q_off=past, n_k=n_k)

    y, fst = _mixer_out(to_tm(x), to_tm(mixed), _front_pad_rows(to_tm(ffn_buf), hp2), wts, rows=rows, nb=bsz)
    states = (from_tm(lat), from_tm(kr), h_last[0],
              from_tm(cst[:, hp1 - (lru_w - 1) * bsz:]), from_tm(fst[:, hp2 - (ffn_w - 1) * bsz:]))
    return from_tm(y), states


def kernel(x_prompt, x_sample, cache_kv_latent, cache_k_rope, state_lru_h, state_lru_conv, state_ffn_conv,
           g_mix_norm, w_in, g_q_a, w_q_b, g_kv_a, w_kv_b, g_qn, g_qr, g_kn, g_kr, lru_conv_w, lru_conv_b,
           w_rg, b_rg, w_ig, b_ig, lru_lambda, w_out, g_ffn_norm, w_up, ffn_conv_w, ffn_conv_b, w_down):
    depth = w_in.shape[0]
    yp, ys = x_prompt, x_sample
    p_states, s_states = [], []
    for l in range(depth):
        wts = _prep_weights(l, g_mix_norm, w_in, g_q_a, w_q_b, g_kv_a, w_kv_b, g_qn, g_qr, g_kn, g_kr,
                            lru_conv_w, lru_conv_b, w_rg, b_rg, w_ig, b_ig, lru_lambda, w_out, g_ffn_norm,
                            w_up, ffn_conv_w, ffn_conv_b, w_down)
        yp, st_p = _layer_prompt(yp, wts, rows=256, tq=512, tk=512)
        ys, st_s = _layer_sample(ys, cache_kv_latent[l], cache_k_rope[l], state_lru_h[l],
                                 state_lru_conv[l], state_ffn_conv[l], wts, tk=512)
        p_states.append(st_p)
        s_states.append(st_s)
    p_out = [jnp.stack([st[j] for st in p_states], axis=0) for j in range(5)]
    s_out = [jnp.stack([st[j] for st in s_states], axis=0) for j in range(5)]
    return (yp, ys, *p_out, *s_out)
```

```python
import functools

import jax
import jax.numpy as jnp
from jax import lax
from jax.experimental import pallas as pl
from jax.experimental.pallas import tpu as pltpu

CHUNK = 64
CHUNK_SHIFT = CHUNK.bit_length() - 1
assert CHUNK == 1 << CHUNK_SHIFT
N_HEADS = 16
NOPE_DIM = 64
ROPE_DIM = 32
V_DIM = 64
QK_DIM = NOPE_DIM + ROPE_DIM
ROPE_THETA = 10000.0
RG_C = 8.0
EPS = 1e-6
LOG2_E = 1.4426950408889634

LANES = 128
SUBLANES = 8
MXU_DIM = 256
VMEM_BYTES_V7X = 64 * 1024 * 1024

HEAD_PAD = LANES
HALF_ROPE = ROPE_DIM // 2
F32 = jnp.float32
BF16 = jnp.bfloat16


def _round_up(n, m):
    return (n + m - 1) // m * m


def _vmem_limit(nbytes):
    return int(min(2 * nbytes, VMEM_BYTES_V7X - 8 * 1024 * 1024))


def _const_spec(shape):
    nd = len(shape)
    return pl.BlockSpec(shape, lambda *_: (0,) * nd, pipeline_mode=pl.Buffered(1))


def _dot(a, b):
    return jnp.dot(a, b, preferred_element_type=F32)


def _dot_nt(a, b):
    return lax.dot_general(a, b, (((1,), (1,)), ((), ())), preferred_element_type=F32)


def _rms_rows(x, n):
    return lax.rsqrt(jnp.sum(x * x, axis=-1, keepdims=True) * (1.0 / n) + EPS)


def _rope_head(x, c, s_lo, s_hi):
    return (x * c + pltpu.roll(x, HALF_ROPE, 1) * s_hi
            + pltpu.roll(x, HEAD_PAD - HALF_ROPE, 1) * s_lo)


def _shift_rows(x, n, fill):
    rows, cols = x.shape
    if n % SUBLANES == 0:
        return jnp.concatenate([jnp.full((n, cols), fill, x.dtype), x[:rows - n]], axis=0)
    rolled = pltpu.roll(x, n, 0)
    row = lax.broadcasted_iota(jnp.int32, x.shape, 0)
    return jnp.where(row >= n, rolled, fill)


def _linear_scan(a, b, nb):
    rows = a.shape[0]
    s = nb
    while s < rows:
        b = a * _shift_rows(b, s, 0.0) + b
        if 2 * s < rows:
            a = a * _shift_rows(a, s, 1.0)
        s *= 2
    return b


def _mixer_in_kernel(x_ref, hist_ref, h0_ref, cos_ref, slo_ref, shi_ref,
                     g_mix_ref, w_in_ref, g_qa_ref, wq_ref, gq_ref, g_kva_ref, g_kr_ref,
                     wk_ref, gk_ref, wvt_ref, cw_ref, cb_ref, wlru_ref, brg_ref, big_ref, lam_ref,
                     q_ref, k_ref, vt_ref, lat_ref, kr_ref, ga_ref, rg_ref, hl_ref, cst_ref,
                     xc_ref, hprev_ref, *, rows, nb, hp, d_model, q_lora, kv_lora, conv_w):
    t = pl.program_id(1)

    @pl.when(t == 0)
    def _():
        xc_ref[0:hp, :] = hist_ref[...]
        hprev_ref[...] = h0_ref[...]

    x = x_ref[...]
    xn = (x * _rms_rows(x, d_model) * g_mix_ref[...]).astype(BF16)
    z = _dot(xn, w_in_ref[...])
    o_kv = q_lora
    o_kr = o_kv + kv_lora
    o_u = o_kr + HEAD_PAD
    o_ga = o_u + d_model
    o_gb = o_ga + d_model
    cq = z[:, 0:o_kv]
    ckv = z[:, o_kv:o_kr]
    krb = z[:, o_kr:o_u]
    u = z[:, o_u:o_ga]
    gate_a = z[:, o_ga:o_gb]
    gate_b = z[:, o_gb:o_gb + d_model]

    cos = cos_ref[...]
    s_lo = slo_ref[...]
    s_hi = shi_ref[...]
    lane = lax.broadcasted_iota(jnp.int32, (rows, HEAD_PAD), 1)
    is_nope = lane < NOPE_DIM

    lat = ckv * _rms_rows(ckv, kv_lora) * g_kva_ref[...]
    lat_ref[...] = lat
    kr = _rope_head(krb * _rms_rows(krb, ROPE_DIM) * g_kr_ref[...], cos, s_lo, s_hi)
    kr_ref[...] = kr[:, NOPE_DIM:NOPE_DIM + ROPE_DIM]

    cqn = (cq * _rms_rows(cq, q_lora) * g_qa_ref[...]).astype(BF16)
    qp = _dot(cqn, wq_ref[...])
    for h in range(N_HEADS):
        sl = slice(h * HEAD_PAD, (h + 1) * HEAD_PAD)
        qh = qp[:, sl]
        sq = qh * qh
        r_n = lax.rsqrt(jnp.sum(jnp.where(is_nope, sq, 0.0), axis=-1, keepdims=True) * (1.0 / NOPE_DIM) + EPS)
        r_r = lax.rsqrt(jnp.sum(jnp.where(is_nope, 0.0, sq), axis=-1, keepdims=True) * (1.0 / ROPE_DIM) + EPS)
        qh = qh * jnp.where(is_nope, r_n, r_r) * gq_ref[:, sl]
        q_ref[:, sl] = _rope_head(qh, cos, s_lo, s_hi).astype(BF16)

    latb = lat.astype(BF16)
    kp = _dot(latb, wk_ref[...])
    for h in range(N_HEADS):
        sl = slice(h * HEAD_PAD, (h + 1) * HEAD_PAD)
        kh = kp[:, sl]
        k_ref[:, sl] = (kh * _rms_rows(kh, NOPE_DIM) * gk_ref[:, sl] + kr).astype(BF16)
    vt_ref[...] = _dot_nt(wvt_ref[...], latb).astype(BF16)

    xc_ref[hp:hp + rows, :] = u
    u_c = cb_ref[...] + cw_ref[conv_w - 1:conv_w, :] * u
    for j in range(1, conv_w):
        u_c = u_c + cw_ref[conv_w - 1 - j:conv_w - j, :] * xc_ref[hp - j * nb:hp - j * nb + rows, :]
    tail = xc_ref[rows:rows + hp, :]
    cst_ref[...] = tail
    xc_ref[0:hp, :] = tail

    lam = lam_ref[...]
    softplus_neg = jnp.maximum(-lam, 0.0) + jnp.log1p(jnp.exp(-jnp.abs(lam)))
    n_blocks, blk_w, _ = wlru_ref.shape
    a_parts, b_parts = [], []
    for n in range(n_blocks):
        sl = slice(n * blk_w, (n + 1) * blk_w)
        ucn = u_c[:, sl]
        g = _dot(ucn.astype(BF16), wlru_ref[n])
        r = jax.nn.sigmoid(g[:, 0:blk_w] + brg_ref[:, sl])
        i = jax.nn.sigmoid(g[:, blk_w:2 * blk_w] + big_ref[:, sl])
        log_a = (-RG_C * r) * softplus_neg[:, sl]
        a_n = jnp.exp(log_a)
        a_parts.append(a_n)
        b_parts.append(jnp.sqrt(-jnp.tanh(log_a) * (1.0 + a_n * a_n)) * (i * ucn))
    a = jnp.concatenate(a_parts, axis=1)
    b = jnp.concatenate(b_parts, axis=1)
    if nb == 1:
        first = lax.broadcasted_iota(jnp.int32, a.shape, 0) < nb
        b = b + jnp.where(first, a * hprev_ref[...], 0.0)
    else:
        b = jnp.concatenate([b[0:nb] + a[0:nb] * hprev_ref[...], b[nb:]], axis=0)
    hseq = _linear_scan(a, b, nb)
    h_last = hseq[rows - nb:rows]
    hprev_ref[...] = h_last
    hl_ref[...] = h_last

    ga_ref[...] = jax.nn.sigmoid(gate_a).astype(BF16)
    rg_ref[...] = (jax.nn.sigmoid(gate_b) * hseq).astype(BF16)


def _mixer_in(x, hist, h0, rope, wts, *, rows, nb):
    groups, tg, d_model = x.shape
    hp = hist.shape[1]
    q_lora = wts["wq"].shape[0]
    kv_lora = wts["wk"].shape[0]
    conv_w = wts["cw"].shape[0]
    n_t = tg // rows
    qk_w = N_HEADS * HEAD_PAD
    v_w = N_HEADS * V_DIM

    def row_spec(width):
        return pl.BlockSpec((None, rows, width), lambda g, t: (g, t, 0))

    def group_spec(r, width):
        return pl.BlockSpec((None, r, width), lambda g, t: (g, 0, 0))

    tab_spec = pl.BlockSpec((rows, HEAD_PAD), lambda g, t: (t, 0))
    w_names = ["g_mix", "w_in", "g_qa", "wq", "gq", "g_kva", "g_kr", "wk", "gk", "wvt",
               "cw", "cb", "wlru", "brg", "big", "lam"]
    w_list = [wts[n] for n in w_names]
    in_specs = ([row_spec(d_model), group_spec(hp, d_model), group_spec(nb, d_model),
                 tab_spec, tab_spec, tab_spec] + [_const_spec(w.shape) for w in w_list])
    out_shape = (
        jax.ShapeDtypeStruct((groups, tg, qk_w), BF16),
        jax.ShapeDtypeStruct((groups, tg, qk_w), BF16),
        jax.ShapeDtypeStruct((groups, n_t, v_w, rows), BF16),
        jax.ShapeDtypeStruct((groups, tg, kv_lora), F32),
        jax.ShapeDtypeStruct((groups, tg, ROPE_DIM), F32),
        jax.ShapeDtypeStruct((groups, tg, d_model), BF16),
        jax.ShapeDtypeStruct((groups, tg, d_model), BF16),
        jax.ShapeDtypeStruct((groups, nb, d_model), F32),
        jax.ShapeDtypeStruct((groups, hp, d_model), F32),
    )
    vt_spec = pl.BlockSpec((None, None, v_w, rows), lambda g, t: (g, t, 0, 0))
    out_specs = (row_spec(qk_w), row_spec(qk_w), vt_spec, row_spec(kv_lora), row_spec(ROPE_DIM),
                 row_spec(d_model), row_spec(d_model), group_spec(nb, d_model), group_spec(hp, d_model))
    w_bytes = sum(w.size * w.dtype.itemsize for w in w_list)
    io_bytes = 2 * rows * (4 * d_model + 2 * (2 * qk_w + v_w + 2 * d_model) + 4 * (kv_lora + LANES) + 12 * LANES)
    tmp_bytes = 4 * rows * (wts["w_in"].shape[1] + 2 * qk_w + 6 * d_model) + 4 * (hp + rows) * d_model
    kern = functools.partial(_mixer_in_kernel, rows=rows, nb=nb, hp=hp, d_model=d_model,
                             q_lora=q_lora, kv_lora=kv_lora, conv_w=conv_w)
    return pl.pallas_call(
        kern,
        grid=(groups, n_t),
        in_specs=in_specs,
        out_specs=out_specs,
        out_shape=out_shape,
        scratch_shapes=[pltpu.VMEM((hp + rows, d_model), F32), pltpu.VMEM((nb, d_model), F32)],
        compiler_params=pltpu.CompilerParams(
            dimension_semantics=("parallel", "arbitrary"),
            vmem_limit_bytes=_vmem_limit(w_bytes + io_bytes + tmp_bytes)),
        name="mixer_in",
    )(x, hist, h0, *rope, *w_list)


def _attn_kernel(q_ref, k_ref, vt_ref, ga_ref, rg_ref, o_ref, m_ref, l_ref, acc_ref,
                 *, tq, tk, vblk, q_off, n_k):
    qi = pl.program_id(2)
    first_q = q_off + qi * tq
    full_end = jnp.minimum(((first_q >> CHUNK_SHIFT) + 1) << CHUNK_SHIFT, n_k)
    any_end = jnp.minimum((((first_q + (tq - 1)) >> CHUNK_SHIFT) + 1) << CHUNK_SHIFT, n_k)
    n_full = lax.div(full_end, tk)
    n_any = lax.div(any_end + (tk - 1), tk)

    m_ref[...] = jnp.full(m_ref.shape, -jnp.inf, F32)
    l_ref[...] = jnp.zeros(l_ref.shape, F32)
    acc_ref[...] = jnp.zeros(acc_ref.shape, F32)

    def key_tile(kj, masked):
        k0 = pl.multiple_of(kj * tk, tk)
        if masked:
            q_pos = first_q + lax.broadcasted_iota(jnp.int32, (tk, tq), 1)
            k_pos = k0 + lax.broadcasted_iota(jnp.int32, (tk, tq), 0)
            visible = jnp.logical_and((k_pos >> CHUNK_SHIFT) <= (q_pos >> CHUNK_SHIFT), k_pos < n_k)
        for hh in range(2):
            lanes = slice(hh * HEAD_PAD, (hh + 1) * HEAD_PAD)
            s = _dot_nt(k_ref[pl.ds(k0, tk), lanes], q_ref[:, lanes])
            if masked:
                s = jnp.where(visible, s, -jnp.inf)
            m_prev = m_ref[hh]
            m_new = jnp.maximum(m_prev, jnp.max(s, axis=0, keepdims=True))
            alpha = jnp.exp2(m_prev - m_new)
            p = jnp.exp2(s - m_new)
            l_ref[hh] = alpha * l_ref[hh] + jnp.sum(p, axis=0, keepdims=True)
            m_ref[hh] = m_new
            pb = p.astype(BF16)
            rows = slice(hh * V_DIM, (hh + 1) * V_DIM)
            pv = None
            for j in range(tk // vblk):
                part = _dot(vt_ref[kj * (tk // vblk) + j, rows, :], pb[j * vblk:(j + 1) * vblk])
                pv = part if pv is None else pv + part
            acc_ref[rows, :] = alpha * acc_ref[rows, :] + pv

    def full_body(kj, carry):
        key_tile(kj, False)
        return carry

    def masked_body(kj, carry):
        key_tile(kj, True)
        return carry

    lax.fori_loop(0, n_full, full_body, 0)
    lax.fori_loop(n_full, n_any, masked_body, 0)

    attn_t = jnp.concatenate([acc_ref[0:V_DIM, :] / l_ref[0], acc_ref[V_DIM:2 * V_DIM, :] / l_ref[1]], axis=0)
    o_ref[...] = (ga_ref[...].astype(F32) * attn_t.T + rg_ref[...].astype(F32)).astype(BF16)


def _attention(q, k, vt, ga, rg, *, tq, tk, q_off, n_k):
    bsz, t_q, _ = q.shape
    t_k = k.shape[1]
    n_vb, _, vblk = vt.shape[1:]
    assert LANES == 2 * V_DIM and N_HEADS % 2 == 0
    assert t_k == n_vb * vblk and tk % vblk == 0 and t_k % tk == 0 and t_q % tq == 0
    n_hp = N_HEADS // 2
    qk_blk = 2 * HEAD_PAD

    def q_index(b, p, qi):
        return (b, qi, p)

    kern = functools.partial(_attn_kernel, tq=tq, tk=tk, vblk=vblk, q_off=q_off, n_k=n_k)
    blk_bytes = 2 * 2 * (tq * qk_blk + t_k * qk_blk + t_k * LANES + 3 * tq * LANES)
    tmp_bytes = 4 * 6 * tq * tk + 4 * 4 * tq * LANES
    return pl.pallas_call(
        kern,
        grid=(bsz, n_hp, t_q // tq),
        in_specs=[pl.BlockSpec((None, tq, qk_blk), q_index),
                  pl.BlockSpec((None, t_k, qk_blk), lambda b, p, qi: (b, 0, p)),
                  pl.BlockSpec((None, n_vb, LANES, vblk), lambda b, p, qi: (b, 0, p, 0)),
                  pl.BlockSpec((None, tq, LANES), q_index),
                  pl.BlockSpec((None, tq, LANES), q_index)],
        out_specs=pl.BlockSpec((None, tq, LANES), q_index),
        out_shape=jax.ShapeDtypeStruct((bsz, t_q, N_HEADS * V_DIM), BF16),
        scratch_shapes=[pltpu.VMEM((2, 1, tq), F32),
                        pltpu.VMEM((2, 1, tq), F32),
                        pltpu.VMEM((LANES, tq), F32)],
        compiler_params=pltpu.CompilerParams(
            dimension_semantics=("parallel", "parallel", "arbitrary"),
            vmem_limit_bytes=_vmem_limit(blk_bytes + tmp_bytes)),
        name="attention",
    )(q, k, vt, ga, rg)


def _kv_up_kernel(lat_ref, kr_ref, wk_ref, gk_ref, wvt_ref, k_ref, vt_ref):
    latb = lat_ref[...].astype(BF16)
    kr = kr_ref[...]
    kp = _dot(latb, wk_ref[...])
    for h in range(N_HEADS):
        sl = slice(h * HEAD_PAD, (h + 1) * HEAD_PAD)
        kh = kp[:, sl]
        k_ref[:, sl] = (kh * _rms_rows(kh, NOPE_DIM) * gk_ref[:, sl] + kr).astype(BF16)
    vt_ref[...] = _dot_nt(wvt_ref[...], latb).astype(BF16)


def _kv_up(lat, kr_pad, wts, *, rows):
    bsz, n, kv_lora = lat.shape
    qk_w = N_HEADS * HEAD_PAD
    v_w = N_HEADS * V_DIM
    w_list = [wts["wk"], wts["gk"], wts["wvt"]]
    w_bytes = sum(w.size * w.dtype.itemsize for w in w_list)
    blk_bytes = 2 * rows * (4 * kv_lora + 4 * HEAD_PAD + 2 * qk_w + 2 * v_w)

    def row_spec(width):
        return pl.BlockSpec((None, rows, width), lambda b, i: (b, i, 0))

    return pl.pallas_call(
        _kv_up_kernel,
        grid=(bsz, n // rows),
        in_specs=[row_spec(kv_lora), row_spec(HEAD_PAD)] + [_const_spec(w.shape) for w in w_list],
        out_specs=(row_spec(qk_w), pl.BlockSpec((None, v_w, rows), lambda b, i: (b, 0, i))),
        out_shape=(jax.ShapeDtypeStruct((bsz, n, qk_w), BF16), jax.ShapeDtypeStruct((bsz, v_w, n), BF16)),
        compiler_params=pltpu.CompilerParams(
            dimension_semantics=("parallel", "parallel"),
            vmem_limit_bytes=_vmem_limit(w_bytes + blk_bytes + 4 * rows * (qk_w + v_w))),
        name="kv_up",
    )(lat, kr_pad, *w_list)


def _mixer_out_kernel(x_ref, mix_ref, hist_ref, w_out_ref, g_ffn_ref, w_up_ref, fw_ref, fb_ref, w_down_ref,
                      y_ref, fst_ref, upc_ref, *, rows, nb, hp, d_model, d_ff, conv_w, col_blk):
    t = pl.program_id(1)

    @pl.when(t == 0)
    def _():
        upc_ref[0:hp, :] = hist_ref[...]

    x1 = x_ref[...] + _dot(mix_ref[...], w_out_ref[...])
    xn = (x1 * _rms_rows(x1, d_model) * g_ffn_ref[...]).astype(BF16)

    def conv_cols(c0):
        sl = slice(c0, c0 + col_blk)
        up = _dot(xn, w_up_ref[:, sl])
        upc_ref[hp:hp + rows, sl] = up
        out = fb_ref[:, sl] + fw_ref[conv_w - 1:conv_w, sl] * up
        for j in range(1, conv_w):
            out = out + fw_ref[conv_w - 1 - j:conv_w - j, sl] * upc_ref[hp - j * nb:hp - j * nb + rows, sl]
        return out

    y = x1
    for c in range(d_ff // col_blk):
        gate = conv_cols(c * col_blk)
        val = conv_cols(d_ff + c * col_blk)
        hmid = (gate * jax.nn.sigmoid(gate) * val).astype(BF16)
        y = y + _dot(hmid, w_down_ref[c * col_blk:(c + 1) * col_blk, :])
    y_ref[...] = y

    tail = upc_ref[rows:rows + hp, :]
    fst_ref[...] = tail
    upc_ref[0:hp, :] = tail


def _mixer_out(x, mixed, hist, wts, *, rows, nb):
    groups, tg, d_model = x.shape
    hp = hist.shape[1]
    d_ff = wts["w_down"].shape[0]
    conv_w = wts["fw"].shape[0]
    n_t = tg // rows
    col_blk = MXU_DIM

    def row_spec(width):
        return pl.BlockSpec((None, rows, width), lambda g, t: (g, t, 0))

    def group_spec(r, width):
        return pl.BlockSpec((None, r, width), lambda g, t: (g, 0, 0))

    w_names = ["w_out", "g_ffn", "w_up", "fw", "fb", "w_down"]
    w_list = [wts[n] for n in w_names]
    w_bytes = sum(w.size * w.dtype.itemsize for w in w_list)
    blk_bytes = 2 * rows * d_model * (4 + 2 + 4) + 4 * 4 * hp * 2 * d_ff
    tmp_bytes = 4 * (hp + rows) * 2 * d_ff + 4 * rows * (3 * d_model + 8 * col_blk)
    kern = functools.partial(_mixer_out_kernel, rows=rows, nb=nb, hp=hp, d_model=d_model, d_ff=d_ff,
                             conv_w=conv_w, col_blk=col_blk)
    return pl.pallas_call(
        kern,
        grid=(groups, n_t),
        in_specs=[row_spec(d_model), row_spec(d_model), group_spec(hp, 2 * d_ff)]
                 + [_const_spec(w.shape) for w in w_list],
        out_specs=(row_spec(d_model), group_spec(hp, 2 * d_ff)),
        out_shape=(jax.ShapeDtypeStruct((groups, tg, d_model), F32),
                   jax.ShapeDtypeStruct((groups, hp, 2 * d_ff), F32)),
        scratch_shapes=[pltpu.VMEM((hp + rows, 2 * d_ff), F32)],
        compiler_params=pltpu.CompilerParams(
            dimension_semantics=("parallel", "arbitrary"),
            vmem_limit_bytes=_vmem_limit(w_bytes + blk_bytes + tmp_bytes)),
        name="mixer_out",
    )(x, mixed, hist, *w_list)


def _head_pad(w, widths):
    lead = w.shape[:-1]
    per_head = sum(widths)
    w = w.reshape(lead + (N_HEADS, per_head))
    w = jnp.pad(w, [(0, 0)] * len(lead) + [(0, 0), (0, HEAD_PAD - per_head)])
    return w.reshape(lead + (N_HEADS * HEAD_PAD,))


def _prep_weights(l, g_mix_norm, w_in, g_q_a, w_q_b, g_kv_a, w_kv_b, g_qn, g_qr, g_kn, g_kr,
                  lru_conv_w, lru_conv_b, w_rg, b_rg, w_ig, b_ig, lru_lambda, w_out, g_ffn_norm,
                  w_up, ffn_conv_w, ffn_conv_b, w_down):
    d_model = w_in.shape[1]
    q_lora = w_q_b.shape[1]
    kv_lora = w_kv_b.shape[1]
    row = lambda a: a.reshape(1, -1)
    o_kr = q_lora + kv_lora
    wi = w_in[l]
    w_kr = jnp.pad(wi[:, o_kr:o_kr + ROPE_DIM], ((0, 0), (NOPE_DIM, HEAD_PAD - QK_DIM)))
    w_in_p = jnp.concatenate([wi[:, :o_kr], w_kr, wi[:, o_kr + ROPE_DIM:]], axis=1).astype(BF16)
    scale = QK_DIM ** -0.5 * LOG2_E
    gq = jnp.tile(jnp.pad(jnp.concatenate([g_qn[l], g_qr[l]]) * scale, (0, HEAD_PAD - QK_DIM)), N_HEADS)
    gk = jnp.tile(jnp.pad(g_kn[l], (0, HEAD_PAD - NOPE_DIM)), N_HEADS)
    kv = w_kv_b[l].reshape(kv_lora, N_HEADS, NOPE_DIM + V_DIM)
    wk = _head_pad(kv[:, :, :NOPE_DIM].reshape(kv_lora, N_HEADS * NOPE_DIM), (NOPE_DIM,))
    wv = kv[:, :, NOPE_DIM:].reshape(kv_lora, N_HEADS * V_DIM)
    return {
        "g_mix": row(g_mix_norm[l]), "w_in": w_in_p, "g_qa": row(g_q_a[l]),
        "wq": _head_pad(w_q_b[l], (NOPE_DIM, ROPE_DIM)).astype(BF16), "gq": row(gq),
        "g_kva": row(g_kv_a[l]), "g_kr": row(jnp.pad(g_kr[l], (NOPE_DIM, HEAD_PAD - QK_DIM))),
        "wk": wk.astype(BF16), "gk": row(gk), "wvt": wv.T.astype(BF16),
        "cw": lru_conv_w[l], "cb": row(lru_conv_b[l]),
        "wlru": jnp.concatenate([w_rg[l], w_ig[l]], axis=-1).astype(BF16),
        "brg": row(b_rg[l]), "big": row(b_ig[l]), "lam": row(lru_lambda[l]),
        "w_out": w_out[l].astype(BF16), "g_ffn": row(g_ffn_norm[l]), "w_up": w_up[l].astype(BF16),
        "fw": ffn_conv_w[l], "fb": row(ffn_conv_b[l]), "w_down": w_down[l].astype(BF16),
    }


def _rope_tables(pos):
    inv = ROPE_THETA ** (-jnp.arange(0, ROPE_DIM, 2, dtype=F32) / ROPE_DIM)
    ang = pos.astype(F32)[:, None] * inv[None, :]
    cos, sin = jnp.cos(ang), jnp.sin(ang)
    n = pos.shape[0]
    ones_lo = jnp.ones((n, NOPE_DIM), F32)
    zeros_lo = jnp.zeros((n, NOPE_DIM), F32)
    zeros_half = jnp.zeros((n, HALF_ROPE), F32)
    tail = jnp.zeros((n, HEAD_PAD - QK_DIM), F32)
    c = jnp.concatenate([ones_lo, cos, cos, tail], axis=1)
    s_lo = jnp.concatenate([zeros_lo, -sin, zeros_half, tail], axis=1)
    s_hi = jnp.concatenate([zeros_lo, zeros_half, sin, tail], axis=1)
    return c, s_lo, s_hi


def _front_pad_rows(a, hp):
    return jnp.pad(a, ((0, 0), (hp - a.shape[1], 0), (0, 0)))


def _layer_prompt(x, wts, *, rows, tq, tk):
    bsz, t, d_model = x.shape
    lru_w = wts["cw"].shape[0]
    ffn_w = wts["fw"].shape[0]
    d_ff2 = wts["w_up"].shape[1]
    hp1 = _round_up(lru_w - 1, SUBLANES)
    hp2 = _round_up(ffn_w - 1, SUBLANES)
    rope = _rope_tables(jnp.arange(t, dtype=jnp.int32))
    q, k, vt, lat, kr, ga, rg, h_last, cst = _mixer_in(
        x, jnp.zeros((bsz, hp1, d_model), F32), jnp.zeros((bsz, 1, d_model), F32), rope, wts, rows=rows, nb=1)
    mixed = _attention(q, k, vt, ga, rg, tq=tq, tk=tk, q_off=0, n_k=t)
    y, fst = _mixer_out(x, mixed, jnp.zeros((bsz, hp2, d_ff2), F32), wts, rows=rows, nb=1)
    return y, (lat, kr, h_last[:, 0], cst[:, hp1 - (lru_w - 1):], fst[:, hp2 - (ffn_w - 1):])


def _layer_sample(x, past_lat, past_kr, h0, lru_buf, ffn_buf, wts, *, tk, vblk):
    bsz, t, d_model = x.shape
    past = past_lat.shape[1]
    lru_w = wts["cw"].shape[0]
    ffn_w = wts["fw"].shape[0]
    hp1 = _round_up((lru_w - 1) * bsz, SUBLANES)
    hp2 = _round_up((ffn_w - 1) * bsz, SUBLANES)
    rows = t * bsz

    def to_tm(a):
        return jnp.swapaxes(a, 0, 1).reshape(1, a.shape[1] * bsz, a.shape[2])

    def from_tm(a):
        return jnp.swapaxes(a.reshape(a.shape[1] // bsz, bsz, a.shape[2]), 0, 1)

    pos = past + jnp.arange(t, dtype=jnp.int32)
    rope = _rope_tables(jnp.repeat(pos, bsz))
    q, k, vt, lat, kr, ga, rg, h_last, cst = _mixer_in(
        to_tm(x), _front_pad_rows(to_tm(lru_buf), hp1), h0[None], rope, wts, rows=rows, nb=bsz)

    kr_pad = jnp.pad(past_kr, ((0, 0), (0, 0), (NOPE_DIM, HEAD_PAD - QK_DIM)))
    k_past, vt_past = _kv_up(past_lat, kr_pad, wts, rows=vblk)
    n_k = past + t
    n_k_pad = _round_up(n_k, tk)
    k_all = jnp.concatenate([k_past, from_tm(k)], axis=1)
    k_all = jnp.pad(k_all, ((0, 0), (0, n_k_pad - n_k), (0, 0)))
    vt_new = jnp.transpose(vt[0, 0].reshape(-1, t, bsz), (2, 0, 1))
    vt_all = jnp.pad(jnp.concatenate([vt_past, vt_new], axis=2), ((0, 0), (0, 0), (0, n_k_pad - n_k)))
    vt_all = jnp.transpose(vt_all.reshape(bsz, -1, n_k_pad // vblk, vblk), (0, 2, 1, 3))
    tq = _round_up(t, LANES)
    pad_q = lambda a: jnp.pad(from_tm(a), ((0, 0), (0, tq - t), (0, 0)))
    mixed = _attention(pad_q(q), k_all, vt_all, pad_q(ga), pad_q(rg),
                       tq=tq, tk=tk, q_off=past, n_k=n_k)[:, :t]

    y, fst = _mixer_out(to_tm(x), to_tm(mixed), _front_pad_rows(to_tm(ffn_buf), hp2), wts, rows=rows, nb=bsz)
    states = (from_tm(lat), from_tm(kr), h_last[0],
              from_tm(cst[:, hp1 - (lru_w - 1) * bsz:]), from_tm(fst[:, hp2 - (ffn_w - 1) * bsz:]))
    return from_tm(y), states


def kernel(x_prompt, x_sample, cache_kv_latent, cache_k_rope, state_lru_h, state_lru_conv, state_ffn_conv,
           g_mix_norm, w_in, g_q_a, w_q_b, g_kv_a, w_kv_b, g_qn, g_qr, g_kn, g_kr, lru_conv_w, lru_conv_b,
           w_rg, b_rg, w_ig, b_ig, lru_lambda, w_out, g_ffn_norm, w_up, ffn_conv_w, ffn_conv_b, w_down):
    depth = w_in.shape[0]
    yp, ys = x_prompt, x_sample
    p_states, s_states = [], []
    for l in range(depth):
        wts = _prep_weights(l, g_mix_norm, w_in, g_q_a, w_q_b, g_kv_a, w_kv_b, g_qn, g_qr, g_kn, g_kr,
                            lru_conv_w, lru_conv_b, w_rg, b_rg, w_ig, b_ig, lru_lambda, w_out, g_ffn_norm,
                            w_up, ffn_conv_w, ffn_conv_b, w_down)
        yp, st_p = _layer_prompt(yp, wts, rows=256, tq=512, tk=512)
        ys, st_s = _layer_sample(ys, cache_kv_latent[l], cache_k_rope[l], state_lru_h[l],
                                 state_lru_conv[l], state_ffn_conv[l], wts, tk=512, vblk=256)
        p_states.append(st_p)
        s_states.append(st_s)
    p_out = [jnp.stack([st[j] for st in p_states], axis=0) for j in range(5)]
    s_out = [jnp.stack([st[j] for st in s_states], axis=0) for j in range(5)]
    return (yp, ys, *p_out, *s_out)
```

```python
import functools

import jax
import jax.numpy as jnp
from jax import lax
from jax.experimental import pallas as pl
from jax.experimental.pallas import tpu as pltpu

CHUNK = 64
CHUNK_SHIFT = CHUNK.bit_length() - 1
assert CHUNK == 1 << CHUNK_SHIFT
N_HEADS = 16
NOPE_DIM = 64
ROPE_DIM = 32
V_DIM = 64
QK_DIM = NOPE_DIM + ROPE_DIM
ROPE_THETA = 10000.0
RG_C = 8.0
EPS = 1e-6
LOG2_E = 1.4426950408889634

LANES = 128
SUBLANES = 8
MXU_DIM = 256
VMEM_BYTES_V7X = 64 * 1024 * 1024

HEAD_PAD = LANES
HALF_ROPE = ROPE_DIM // 2
F32 = jnp.float32
BF16 = jnp.bfloat16


def _round_up(n, m):
    return (n + m - 1) // m * m


def _vmem_limit(nbytes):
    return int(min(2 * nbytes, VMEM_BYTES_V7X - 8 * 1024 * 1024))


def _const_spec(shape):
    nd = len(shape)
    return pl.BlockSpec(shape, lambda *_: (0,) * nd, pipeline_mode=pl.Buffered(1))


def _dot(a, b):
    return jnp.dot(a, b, preferred_element_type=F32)


def _dot_nt(a, b):
    return lax.dot_general(a, b, (((1,), (1,)), ((), ())), preferred_element_type=F32)


def _rms_rows(x, n):
    return lax.rsqrt(jnp.sum(x * x, axis=-1, keepdims=True) * (1.0 / n) + EPS)


def _rope_head(x, c, s_lo, s_hi):
    return (x * c + pltpu.roll(x, HALF_ROPE, 1) * s_hi
            + pltpu.roll(x, HEAD_PAD - HALF_ROPE, 1) * s_lo)


def _shift_rows(x, n, fill):
    rows, cols = x.shape
    if n % SUBLANES == 0:
        return jnp.concatenate([jnp.full((n, cols), fill, x.dtype), x[:rows - n]], axis=0)
    rolled = pltpu.roll(x, n, 0)
    row = lax.broadcasted_iota(jnp.int32, x.shape, 0)
    return jnp.where(row >= n, rolled, fill)


def _linear_scan(a, b, nb):
    rows = a.shape[0]
    s = nb
    while s < rows:
        b = a * _shift_rows(b, s, 0.0) + b
        if 2 * s < rows:
            a = a * _shift_rows(a, s, 1.0)
        s *= 2
    return b


def _mixer_in_kernel(x_ref, hist_ref, h0_ref, cos_ref, slo_ref, shi_ref,
                     g_mix_ref, w_in_ref, g_qa_ref, wq_ref, gq_ref, g_kva_ref, g_kr_ref,
                     wk_ref, gk_ref, wvt_ref, cw_ref, cb_ref, wlru_ref, brg_ref, big_ref, lam_ref,
                     q_ref, k_ref, vt_ref, lat_ref, kr_ref, ga_ref, rg_ref, hl_ref, cst_ref,
                     xc_ref, hprev_ref, *, rows, nb, hp, d_model, q_lora, kv_lora, conv_w):
    t = pl.program_id(1)

    @pl.when(t == 0)
    def _():
        xc_ref[0:hp, :] = hist_ref[...]
        hprev_ref[...] = h0_ref[...]

    x = x_ref[...]
    xn = (x * _rms_rows(x, d_model) * g_mix_ref[...]).astype(BF16)
    z = _dot(xn, w_in_ref[...])
    o_kv = q_lora
    o_kr = o_kv + kv_lora
    o_u = o_kr + HEAD_PAD
    o_ga = o_u + d_model
    o_gb = o_ga + d_model
    cq = z[:, 0:o_kv]
    ckv = z[:, o_kv:o_kr]
    krb = z[:, o_kr:o_u]
    u = z[:, o_u:o_ga]
    gate_a = z[:, o_ga:o_gb]
    gate_b = z[:, o_gb:o_gb + d_model]

    cos = cos_ref[...]
    s_lo = slo_ref[...]
    s_hi = shi_ref[...]
    lane = lax.broadcasted_iota(jnp.int32, (rows, HEAD_PAD), 1)
    is_nope = lane < NOPE_DIM

    lat = ckv * _rms_rows(ckv, kv_lora) * g_kva_ref[...]
    lat_ref[...] = lat
    kr = _rope_head(krb * _rms_rows(krb, ROPE_DIM) * g_kr_ref[...], cos, s_lo, s_hi)
    kr_ref[...] = kr[:, NOPE_DIM:NOPE_DIM + ROPE_DIM]

    cqn = (cq * _rms_rows(cq, q_lora) * g_qa_ref[...]).astype(BF16)
    qp = _dot(cqn, wq_ref[...])
    for h in range(N_HEADS):
        sl = slice(h * HEAD_PAD, (h + 1) * HEAD_PAD)
        qh = qp[:, sl]
        sq = qh * qh
        r_n = lax.rsqrt(jnp.sum(jnp.where(is_nope, sq, 0.0), axis=-1, keepdims=True) * (1.0 / NOPE_DIM) + EPS)
        r_r = lax.rsqrt(jnp.sum(jnp.where(is_nope, 0.0, sq), axis=-1, keepdims=True) * (1.0 / ROPE_DIM) + EPS)
        qh = qh * jnp.where(is_nope, r_n, r_r) * gq_ref[:, sl]
        q_ref[:, sl] = _rope_head(qh, cos, s_lo, s_hi).astype(BF16)

    latb = lat.astype(BF16)
    kp = _dot(latb, wk_ref[...])
    for h in range(N_HEADS):
        sl = slice(h * HEAD_PAD, (h + 1) * HEAD_PAD)
        kh = kp[:, sl]
        k_ref[:, sl] = (kh * _rms_rows(kh, NOPE_DIM) * gk_ref[:, sl] + kr).astype(BF16)
    vt_ref[...] = _dot_nt(wvt_ref[...], latb).astype(BF16)

    xc_ref[hp:hp + rows, :] = u
    u_c = cb_ref[...] + cw_ref[conv_w - 1:conv_w, :] * u
    for j in range(1, conv_w):
        u_c = u_c + cw_ref[conv_w - 1 - j:conv_w - j, :] * xc_ref[hp - j * nb:hp - j * nb + rows, :]
    tail = xc_ref[rows:rows + hp, :]
    cst_ref[...] = tail
    xc_ref[0:hp, :] = tail

    lam = lam_ref[...]
    softplus_neg = jnp.maximum(-lam, 0.0) + jnp.log1p(jnp.exp(-jnp.abs(lam)))
    n_blocks, blk_w, _ = wlru_ref.shape
    a_parts, b_parts = [], []
    for n in range(n_blocks):
        sl = slice(n * blk_w, (n + 1) * blk_w)
        ucn = u_c[:, sl]
        g = _dot(ucn.astype(BF16), wlru_ref[n])
        r = jax.nn.sigmoid(g[:, 0:blk_w] + brg_ref[:, sl])
        i = jax.nn.sigmoid(g[:, blk_w:2 * blk_w] + big_ref[:, sl])
        log_a = (-RG_C * r) * softplus_neg[:, sl]
        a_n = jnp.exp(log_a)
        a_parts.append(a_n)
        b_parts.append(jnp.sqrt(-jnp.tanh(log_a) * (1.0 + a_n * a_n)) * (i * ucn))
    a = jnp.concatenate(a_parts, axis=1)
    b = jnp.concatenate(b_parts, axis=1)
    if nb == 1:
        first = lax.broadcasted_iota(jnp.int32, a.shape, 0) < nb
        b = b + jnp.where(first, a * hprev_ref[...], 0.0)
    else:
        b = jnp.concatenate([b[0:nb] + a[0:nb] * hprev_ref[...], b[nb:]], axis=0)
    hseq = _linear_scan(a, b, nb)
    h_last = hseq[rows - nb:rows]
    hprev_ref[...] = h_last
    hl_ref[...] = h_last

    ga_ref[...] = jax.nn.sigmoid(gate_a).astype(BF16)
    rg_ref[...] = (jax.nn.sigmoid(gate_b) * hseq).astype(BF16)


def _mixer_in(x, hist, h0, rope, wts, *, rows, nb):
    groups, tg, d_model = x.shape
    hp = hist.shape[1]
    q_lora = wts["wq"].shape[0]
    kv_lora = wts["wk"].shape[0]
    conv_w = wts["cw"].shape[0]
    n_t = tg // rows
    qk_w = N_HEADS * HEAD_PAD
    v_w = N_HEADS * V_DIM

    def row_spec(width):
        return pl.BlockSpec((None, rows, width), lambda g, t: (g, t, 0))

    def group_spec(r, width):
        return pl.BlockSpec((None, r, width), lambda g, t: (g, 0, 0))

    tab_spec = pl.BlockSpec((rows, HEAD_PAD), lambda g, t: (t, 0))
    w_names = ["g_mix", "w_in", "g_qa", "wq", "gq", "g_kva", "g_kr", "wk", "gk", "wvt",
               "cw", "cb", "wlru", "brg", "big", "lam"]
    w_list = [wts[n] for n in w_names]
    in_specs = ([row_spec(d_model), group_spec(hp, d_model), group_spec(nb, d_model),
                 tab_spec, tab_spec, tab_spec] + [_const_spec(w.shape) for w in w_list])
    out_shape = (
        jax.ShapeDtypeStruct((groups, tg, qk_w), BF16),
        jax.ShapeDtypeStruct((groups, tg, qk_w), BF16),
        jax.ShapeDtypeStruct((groups, n_t, v_w, rows), BF16),
        jax.ShapeDtypeStruct((groups, tg, kv_lora), F32),
        jax.ShapeDtypeStruct((groups, tg, ROPE_DIM), F32),
        jax.ShapeDtypeStruct((groups, tg, d_model), BF16),
        jax.ShapeDtypeStruct((groups, tg, d_model), BF16),
        jax.ShapeDtypeStruct((groups, nb, d_model), F32),
        jax.ShapeDtypeStruct((groups, hp, d_model), F32),
    )
    vt_spec = pl.BlockSpec((None, None, v_w, rows), lambda g, t: (g, t, 0, 0))
    out_specs = (row_spec(qk_w), row_spec(qk_w), vt_spec, row_spec(kv_lora), row_spec(ROPE_DIM),
                 row_spec(d_model), row_spec(d_model), group_spec(nb, d_model), group_spec(hp, d_model))
    w_bytes = sum(w.size * w.dtype.itemsize for w in w_list)
    io_bytes = 2 * rows * (4 * d_model + 2 * (2 * qk_w + v_w + 2 * d_model) + 4 * (kv_lora + LANES) + 12 * LANES)
    tmp_bytes = 4 * rows * (wts["w_in"].shape[1] + 2 * qk_w + 6 * d_model) + 4 * (hp + rows) * d_model
    kern = functools.partial(_mixer_in_kernel, rows=rows, nb=nb, hp=hp, d_model=d_model,
                             q_lora=q_lora, kv_lora=kv_lora, conv_w=conv_w)
    return pl.pallas_call(
        kern,
        grid=(groups, n_t),
        in_specs=in_specs,
        out_specs=out_specs,
        out_shape=out_shape,
        scratch_shapes=[pltpu.VMEM((hp + rows, d_model), F32), pltpu.VMEM((nb, d_model), F32)],
        compiler_params=pltpu.CompilerParams(
            dimension_semantics=("parallel", "arbitrary"),
            vmem_limit_bytes=_vmem_limit(w_bytes + io_bytes + tmp_bytes)),
        name="mixer_in",
    )(x, hist, h0, *rope, *w_list)


def _attn_kernel(q_ref, k_ref, vt_ref, ga_ref, rg_ref, o_ref, m_ref, l_ref, acc_ref,
                 *, tq, tk, vblk, q_off, n_k):
    qi = pl.program_id(2)
    first_q = q_off + qi * tq
    full_end = jnp.minimum(((first_q >> CHUNK_SHIFT) + 1) << CHUNK_SHIFT, n_k)
    any_end = jnp.minimum((((first_q + (tq - 1)) >> CHUNK_SHIFT) + 1) << CHUNK_SHIFT, n_k)
    n_full = lax.div(full_end, tk)
    n_any = lax.div(any_end + (tk - 1), tk)

    m_ref[...] = jnp.full(m_ref.shape, -jnp.inf, F32)
    l_ref[...] = jnp.zeros(l_ref.shape, F32)
    acc_ref[...] = jnp.zeros(acc_ref.shape, F32)

    def key_tile(kj, masked):
        k0 = pl.multiple_of(kj * tk, tk)
        if masked:
            q_pos = first_q + lax.broadcasted_iota(jnp.int32, (tk, tq), 1)
            k_pos = k0 + lax.broadcasted_iota(jnp.int32, (tk, tq), 0)
            visible = jnp.logical_and((k_pos >> CHUNK_SHIFT) <= (q_pos >> CHUNK_SHIFT), k_pos < n_k)
        for hh in range(2):
            lanes = slice(hh * HEAD_PAD, (hh + 1) * HEAD_PAD)
            s = _dot_nt(k_ref[pl.ds(k0, tk), lanes], q_ref[:, lanes])
            if masked:
                s = jnp.where(visible, s, -jnp.inf)
            m_prev = m_ref[hh]
            m_new = jnp.maximum(m_prev, jnp.max(s, axis=0, keepdims=True))
            alpha = jnp.exp2(m_prev - m_new)
            p = jnp.exp2(s - m_new)
            l_ref[hh] = alpha * l_ref[hh] + jnp.sum(p, axis=0, keepdims=True)
            m_ref[hh] = m_new
            pb = p.astype(BF16)
            rows = slice(hh * V_DIM, (hh + 1) * V_DIM)
            pv = None
            for j in range(tk // vblk):
                part = _dot(vt_ref[kj * (tk // vblk) + j, rows, :], pb[j * vblk:(j + 1) * vblk])
                pv = part if pv is None else pv + part
            acc_ref[rows, :] = alpha * acc_ref[rows, :] + pv

    def full_body(kj, carry):
        key_tile(kj, False)
        return carry

    def masked_body(kj, carry):
        key_tile(kj, True)
        return carry

    lax.fori_loop(0, n_full, full_body, 0)
    lax.fori_loop(n_full, n_any, masked_body, 0)

    attn_t = jnp.concatenate([acc_ref[0:V_DIM, :] / l_ref[0], acc_ref[V_DIM:2 * V_DIM, :] / l_ref[1]], axis=0)
    o_ref[...] = (ga_ref[...].astype(F32) * attn_t.T + rg_ref[...].astype(F32)).astype(BF16)


def _attention(q, k, vt, ga, rg, *, tq, tk, q_off, n_k):
    bsz, t_q, _ = q.shape
    t_k = k.shape[1]
    n_vb, _, vblk = vt.shape[1:]
    assert LANES == 2 * V_DIM and N_HEADS % 2 == 0
    assert t_k == n_vb * vblk and tk % vblk == 0 and t_k % tk == 0 and t_q % tq == 0
    n_hp = N_HEADS // 2
    qk_blk = 2 * HEAD_PAD

    def q_index(b, p, qi):
        return (b, qi, p)

    kern = functools.partial(_attn_kernel, tq=tq, tk=tk, vblk=vblk, q_off=q_off, n_k=n_k)
    blk_bytes = 2 * 2 * (tq * qk_blk + t_k * qk_blk + t_k * LANES + 3 * tq * LANES)
    tmp_bytes = 4 * 6 * tq * tk + 4 * 4 * tq * LANES
    return pl.pallas_call(
        kern,
        grid=(bsz, n_hp, t_q // tq),
        in_specs=[pl.BlockSpec((None, tq, qk_blk), q_index),
                  pl.BlockSpec((None, t_k, qk_blk), lambda b, p, qi: (b, 0, p)),
                  pl.BlockSpec((None, n_vb, LANES, vblk), lambda b, p, qi: (b, 0, p, 0)),
                  pl.BlockSpec((None, tq, LANES), q_index),
                  pl.BlockSpec((None, tq, LANES), q_index)],
        out_specs=pl.BlockSpec((None, tq, LANES), q_index),
        out_shape=jax.ShapeDtypeStruct((bsz, t_q, N_HEADS * V_DIM), BF16),
        scratch_shapes=[pltpu.VMEM((2, 1, tq), F32),
                        pltpu.VMEM((2, 1, tq), F32),
                        pltpu.VMEM((LANES, tq), F32)],
        compiler_params=pltpu.CompilerParams(
            dimension_semantics=("parallel", "parallel", "arbitrary"),
            vmem_limit_bytes=_vmem_limit(blk_bytes + tmp_bytes)),
        name="attention",
    )(q, k, vt, ga, rg)


def _cache_attn_kernel(q_ref, latc_ref, krc_ref, latn_ref, krn_ref, ga_ref, rg_ref,
                       wkg_ref, wkc_ref, seg_ref, expand_ref, wv_ref, o_ref,
                       qabs_ref, qr_ref, m_ref, l_ref, acc_ref, *, t, tk, n_new):
    past = latc_ref.shape[0]
    hq = N_HEADS * t
    for h in range(N_HEADS):
        qh = q_ref[:, h * HEAD_PAD:(h + 1) * HEAD_PAD]
        qabs_ref[h * t:(h + 1) * t, :] = _dot(qh, wkg_ref[h]).astype(BF16)
        qr_ref[h * t:(h + 1) * t, :] = qh[:, NOPE_DIM:QK_DIM]
    m_ref[...] = jnp.full(m_ref.shape, -jnp.inf, F32)
    l_ref[...] = jnp.zeros(l_ref.shape, F32)
    acc_ref[...] = jnp.zeros(acc_ref.shape, F32)

    def key_rows(lat, kr, n_valid):
        n = lat.shape[0]
        latb = lat.astype(BF16)
        kvk = _dot(latb, wkc_ref[...])
        ssum = _dot((kvk * kvk).astype(BF16), seg_ref[...])
        r = lax.rsqrt(ssum * (1.0 / NOPE_DIM) + EPS)
        r_hi = r.astype(BF16)
        r_lo = (r - r_hi.astype(F32)).astype(BF16)
        r_cols = _dot(jnp.concatenate([r_hi, r_lo], axis=1), expand_ref[...])
        s = _dot_nt(latb, qabs_ref[...]) * r_cols + _dot_nt(kr.astype(BF16), qr_ref[...])
        if n_valid < n:
            s = jnp.where(lax.broadcasted_iota(jnp.int32, (n, hq), 0) < n_valid, s, -jnp.inf)
        m_prev = m_ref[...]
        m_new = jnp.maximum(m_prev, jnp.max(s, axis=0, keepdims=True))
        alpha = jnp.exp2(m_prev - m_new)
        p = jnp.exp2(s - m_new)
        l_ref[...] = alpha * l_ref[...] + jnp.sum(p, axis=0, keepdims=True)
        m_ref[...] = m_new
        acc_ref[...] = alpha * acc_ref[...] + _dot(lat.T.astype(BF16), p.astype(BF16))

    def body(kj, carry):
        k0 = pl.multiple_of(kj * tk, tk)
        key_rows(latc_ref[pl.ds(k0, tk), :], krc_ref[pl.ds(k0, tk), :], tk)
        return carry

    lax.fori_loop(0, past // tk, body, 0)
    key_rows(latn_ref[...], krn_ref[...], n_new)

    ctx = (acc_ref[...] / l_ref[...]).T.astype(BF16)
    first_head = lax.broadcasted_iota(jnp.int32, (t, LANES), 1) < V_DIM
    for pair in range(N_HEADS // 2):
        cols = slice(pair * LANES, (pair + 1) * LANES)
        wv = wv_ref[:, cols]
        a0 = _dot(ctx[(2 * pair) * t:(2 * pair + 1) * t], wv)
        a1 = _dot(ctx[(2 * pair + 1) * t:(2 * pair + 2) * t], wv)
        attn = jnp.where(first_head, a0, a1)
        o_ref[:, cols] = (ga_ref[:, cols].astype(F32) * attn + rg_ref[:, cols].astype(F32)).astype(BF16)


def _cache_attention(q, lat_cache, kr_cache, lat_new, kr_new, ga, rg, wts, *, tk, n_new):
    bsz, t, qk_w = q.shape
    past, kv_lora = lat_cache.shape[1:]
    n_pad = lat_new.shape[1]
    d_model = ga.shape[2]
    hq = N_HEADS * t
    assert past % tk == 0 and LANES == 2 * V_DIM
    w_list = [wts["wkg"], wts["wkc"], wts["seg"], wts["expand"], wts["wv"]]
    w_bytes = sum(w.size * w.dtype.itemsize for w in w_list)
    blk_bytes = 2 * (4 * (past + n_pad) * (kv_lora + LANES) + 2 * t * (qk_w + 3 * d_model))
    tmp_bytes = 4 * tk * (2 * N_HEADS * NOPE_DIM + 4 * hq)

    def batch_spec(r, width):
        return pl.BlockSpec((None, r, width), lambda b: (b, 0, 0))

    kern = functools.partial(_cache_attn_kernel, t=t, tk=tk, n_new=n_new)
    return pl.pallas_call(
        kern,
        grid=(bsz,),
        in_specs=[batch_spec(t, qk_w), batch_spec(past, kv_lora), batch_spec(past, ROPE_DIM),
                  batch_spec(n_pad, kv_lora), batch_spec(n_pad, ROPE_DIM),
                  batch_spec(t, d_model), batch_spec(t, d_model)] + [_const_spec(w.shape) for w in w_list],
        out_specs=batch_spec(t, d_model),
        out_shape=jax.ShapeDtypeStruct((bsz, t, d_model), BF16),
        scratch_shapes=[pltpu.VMEM((hq, kv_lora), BF16), pltpu.VMEM((hq, ROPE_DIM), BF16),
                        pltpu.VMEM((1, hq), F32), pltpu.VMEM((1, hq), F32), pltpu.VMEM((kv_lora, hq), F32)],
        compiler_params=pltpu.CompilerParams(
            dimension_semantics=("parallel",),
            vmem_limit_bytes=_vmem_limit(w_bytes + blk_bytes + tmp_bytes)),
        name="cache_attention",
    )(q, lat_cache, kr_cache, lat_new, kr_new, ga, rg, *w_list)


def _mixer_out_kernel(x_ref, mix_ref, hist_ref, w_out_ref, g_ffn_ref, w_up_ref, fw_ref, fb_ref, w_down_ref,
                      y_ref, fst_ref, upc_ref, *, rows, nb, hp, d_model, d_ff, conv_w, col_blk):
    t = pl.program_id(1)

    @pl.when(t == 0)
    def _():
        upc_ref[0:hp, :] = hist_ref[...]

    x1 = x_ref[...] + _dot(mix_ref[...], w_out_ref[...])
    xn = (x1 * _rms_rows(x1, d_model) * g_ffn_ref[...]).astype(BF16)

    def conv_cols(c0):
        sl = slice(c0, c0 + col_blk)
        up = _dot(xn, w_up_ref[:, sl])
        upc_ref[hp:hp + rows, sl] = up
        out = fb_ref[:, sl] + fw_ref[conv_w - 1:conv_w, sl] * up
        for j in range(1, conv_w):
            out = out + fw_ref[conv_w - 1 - j:conv_w - j, sl] * upc_ref[hp - j * nb:hp - j * nb + rows, sl]
        return out

    y = x1
    for c in range(d_ff // col_blk):
        gate = conv_cols(c * col_blk)
        val = conv_cols(d_ff + c * col_blk)
        hmid = (gate * jax.nn.sigmoid(gate) * val).astype(BF16)
        y = y + _dot(hmid, w_down_ref[c * col_blk:(c + 1) * col_blk, :])
    y_ref[...] = y

    tail = upc_ref[rows:rows + hp, :]
    fst_ref[...] = tail
    upc_ref[0:hp, :] = tail


def _mixer_out(x, mixed, hist, wts, *, rows, nb):
    groups, tg, d_model = x.shape
    hp = hist.shape[1]
    d_ff = wts["w_down"].shape[0]
    conv_w = wts["fw"].shape[0]
    n_t = tg // rows
    col_blk = MXU_DIM

    def row_spec(width):
        return pl.BlockSpec((None, rows, width), lambda g, t: (g, t, 0))

    def group_spec(r, width):
        return pl.BlockSpec((None, r, width), lambda g, t: (g, 0, 0))

    w_names = ["w_out", "g_ffn", "w_up", "fw", "fb", "w_down"]
    w_list = [wts[n] for n in w_names]
    w_bytes = sum(w.size * w.dtype.itemsize for w in w_list)
    blk_bytes = 2 * rows * d_model * (4 + 2 + 4) + 4 * 4 * hp * 2 * d_ff
    tmp_bytes = 4 * (hp + rows) * 2 * d_ff + 4 * rows * (3 * d_model + 8 * col_blk)
    kern = functools.partial(_mixer_out_kernel, rows=rows, nb=nb, hp=hp, d_model=d_model, d_ff=d_ff,
                             conv_w=conv_w, col_blk=col_blk)
    return pl.pallas_call(
        kern,
        grid=(groups, n_t),
        in_specs=[row_spec(d_model), row_spec(d_model), group_spec(hp, 2 * d_ff)]
                 + [_const_spec(w.shape) for w in w_list],
        out_specs=(row_spec(d_model), group_spec(hp, 2 * d_ff)),
        out_shape=(jax.ShapeDtypeStruct((groups, tg, d_model), F32),
                   jax.ShapeDtypeStruct((groups, hp, 2 * d_ff), F32)),
        scratch_shapes=[pltpu.VMEM((hp + rows, 2 * d_ff), F32)],
        compiler_params=pltpu.CompilerParams(
            dimension_semantics=("parallel", "arbitrary"),
            vmem_limit_bytes=_vmem_limit(w_bytes + blk_bytes + tmp_bytes)),
        name="mixer_out",
    )(x, mixed, hist, *w_list)


def _head_pad(w, widths):
    lead = w.shape[:-1]
    per_head = sum(widths)
    w = w.reshape(lead + (N_HEADS, per_head))
    w = jnp.pad(w, [(0, 0)] * len(lead) + [(0, 0), (0, HEAD_PAD - per_head)])
    return w.reshape(lead + (N_HEADS * HEAD_PAD,))


def _prep_weights(l, g_mix_norm, w_in, g_q_a, w_q_b, g_kv_a, w_kv_b, g_qn, g_qr, g_kn, g_kr,
                  lru_conv_w, lru_conv_b, w_rg, b_rg, w_ig, b_ig, lru_lambda, w_out, g_ffn_norm,
                  w_up, ffn_conv_w, ffn_conv_b, w_down):
    d_model = w_in.shape[1]
    q_lora = w_q_b.shape[1]
    kv_lora = w_kv_b.shape[1]
    row = lambda a: a.reshape(1, -1)
    o_kr = q_lora + kv_lora
    wi = w_in[l]
    w_kr = jnp.pad(wi[:, o_kr:o_kr + ROPE_DIM], ((0, 0), (NOPE_DIM, HEAD_PAD - QK_DIM)))
    w_in_p = jnp.concatenate([wi[:, :o_kr], w_kr, wi[:, o_kr + ROPE_DIM:]], axis=1).astype(BF16)
    scale = QK_DIM ** -0.5 * LOG2_E
    gq = jnp.tile(jnp.pad(jnp.concatenate([g_qn[l], g_qr[l]]) * scale, (0, HEAD_PAD - QK_DIM)), N_HEADS)
    gk = jnp.tile(jnp.pad(g_kn[l], (0, HEAD_PAD - NOPE_DIM)), N_HEADS)
    kv = w_kv_b[l].reshape(kv_lora, N_HEADS, NOPE_DIM + V_DIM)
    wk = _head_pad(kv[:, :, :NOPE_DIM].reshape(kv_lora, N_HEADS * NOPE_DIM), (NOPE_DIM,))
    wv = kv[:, :, NOPE_DIM:].reshape(kv_lora, N_HEADS * V_DIM)
    wkg = jnp.pad(jnp.transpose(kv[:, :, :NOPE_DIM], (1, 2, 0)) * g_kn[l][None, :, None],
                  ((0, 0), (0, HEAD_PAD - NOPE_DIM), (0, 0)))
    seg = jnp.pad(jnp.repeat(jnp.eye(N_HEADS, dtype=F32), NOPE_DIM, axis=0), ((0, 0), (0, LANES - N_HEADS)))
    return {
        "wkg": wkg.astype(BF16), "wkc": kv[:, :, :NOPE_DIM].reshape(kv_lora, -1).astype(BF16),
        "seg": seg.astype(BF16), "wv": wv.astype(BF16),
        "g_mix": row(g_mix_norm[l]), "w_in": w_in_p, "g_qa": row(g_q_a[l]),
        "wq": _head_pad(w_q_b[l], (NOPE_DIM, ROPE_DIM)).astype(BF16), "gq": row(gq),
        "g_kva": row(g_kv_a[l]), "g_kr": row(jnp.pad(g_kr[l], (NOPE_DIM, HEAD_PAD - QK_DIM))),
        "wk": wk.astype(BF16), "gk": row(gk), "wvt": wv.T.astype(BF16),
        "cw": lru_conv_w[l], "cb": row(lru_conv_b[l]),
        "wlru": jnp.concatenate([w_rg[l], w_ig[l]], axis=-1).astype(BF16),
        "brg": row(b_rg[l]), "big": row(b_ig[l]), "lam": row(lru_lambda[l]),
        "w_out": w_out[l].astype(BF16), "g_ffn": row(g_ffn_norm[l]), "w_up": w_up[l].astype(BF16),
        "fw": ffn_conv_w[l], "fb": row(ffn_conv_b[l]), "w_down": w_down[l].astype(BF16),
    }


def _rope_tables(pos):
    inv = ROPE_THETA ** (-jnp.arange(0, ROPE_DIM, 2, dtype=F32) / ROPE_DIM)
    ang = pos.astype(F32)[:, None] * inv[None, :]
    cos, sin = jnp.cos(ang), jnp.sin(ang)
    n = pos.shape[0]
    ones_lo = jnp.ones((n, NOPE_DIM), F32)
    zeros_lo = jnp.zeros((n, NOPE_DIM), F32)
    zeros_half = jnp.zeros((n, HALF_ROPE), F32)
    tail = jnp.zeros((n, HEAD_PAD - QK_DIM), F32)
    c = jnp.concatenate([ones_lo, cos, cos, tail], axis=1)
    s_lo = jnp.concatenate([zeros_lo, -sin, zeros_half, tail], axis=1)
    s_hi = jnp.concatenate([zeros_lo, zeros_half, sin, tail], axis=1)
    return c, s_lo, s_hi


def _expand_matrix(t):
    one_part = jnp.pad(jnp.repeat(jnp.eye(N_HEADS, dtype=F32), t, axis=1), ((0, LANES - N_HEADS), (0, 0)))
    return jnp.concatenate([one_part, one_part], axis=0).astype(BF16)


def _front_pad_rows(a, hp):
    return jnp.pad(a, ((0, 0), (hp - a.shape[1], 0), (0, 0)))


def _layer_prompt(x, wts, *, rows, tq, tk):
    bsz, t, d_model = x.shape
    lru_w = wts["cw"].shape[0]
    ffn_w = wts["fw"].shape[0]
    d_ff2 = wts["w_up"].shape[1]
    hp1 = _round_up(lru_w - 1, SUBLANES)
    hp2 = _round_up(ffn_w - 1, SUBLANES)
    rope = _rope_tables(jnp.arange(t, dtype=jnp.int32))
    q, k, vt, lat, kr, ga, rg, h_last, cst = _mixer_in(
        x, jnp.zeros((bsz, hp1, d_model), F32), jnp.zeros((bsz, 1, d_model), F32), rope, wts, rows=rows, nb=1)
    mixed = _attention(q, k, vt, ga, rg, tq=tq, tk=tk, q_off=0, n_k=t)
    y, fst = _mixer_out(x, mixed, jnp.zeros((bsz, hp2, d_ff2), F32), wts, rows=rows, nb=1)
    return y, (lat, kr, h_last[:, 0], cst[:, hp1 - (lru_w - 1):], fst[:, hp2 - (ffn_w - 1):])


def _layer_sample(x, past_lat, past_kr, h0, lru_buf, ffn_buf, wts, *, tk):
    bsz, t, d_model = x.shape
    past = past_lat.shape[1]
    lru_w = wts["cw"].shape[0]
    ffn_w = wts["fw"].shape[0]
    hp1 = _round_up((lru_w - 1) * bsz, SUBLANES)
    hp2 = _round_up((ffn_w - 1) * bsz, SUBLANES)
    rows = t * bsz

    def to_tm(a):
        return jnp.swapaxes(a, 0, 1).reshape(1, a.shape[1] * bsz, a.shape[2])

    def from_tm(a):
        return jnp.swapaxes(a.reshape(a.shape[1] // bsz, bsz, a.shape[2]), 0, 1)

    pos = past + jnp.arange(t, dtype=jnp.int32)
    rope = _rope_tables(jnp.repeat(pos, bsz))
    q, k, vt, lat, kr, ga, rg, h_last, cst = _mixer_in(
        to_tm(x), _front_pad_rows(to_tm(lru_buf), hp1), h0[None], rope, wts, rows=rows, nb=bsz)

    assert (past % CHUNK) + t <= CHUNK, "cache attention assumes all keys visible to all queries"
    n_pad = _round_up(t, LANES)
    pad_rows = lambda a: jnp.pad(from_tm(a), ((0, 0), (0, n_pad - t), (0, 0)))
    cache_wts = dict(wts, expand=_expand_matrix(t))
    mixed = _cache_attention(from_tm(q), past_lat, past_kr, pad_rows(lat), pad_rows(kr),
                             from_tm(ga), from_tm(rg), cache_wts, tk=tk, n_new=t)

    y, fst = _mixer_out(to_tm(x), to_tm(mixed), _front_pad_rows(to_tm(ffn_buf), hp2), wts, rows=rows, nb=bsz)
    states = (from_tm(lat), from_tm(kr), h_last[0],
              from_tm(cst[:, hp1 - (lru_w - 1) * bsz:]), from_tm(fst[:, hp2 - (ffn_w - 1) * bsz:]))
    return from_tm(y), states


def kernel(x_prompt, x_sample, cache_kv_latent, cache_k_rope, state_lru_h, state_lru_conv, state_ffn_conv,
           g_mix_norm, w_in, g_q_a, w_q_b, g_kv_a, w_kv_b, g_qn, g_qr, g_kn, g_kr, lru_conv_w, lru_conv_b,
           w_rg, b_rg, w_ig, b_ig, lru_lambda, w_out, g_ffn_norm, w_up, ffn_conv_w, ffn_conv_b, w_down):
    depth = w_in.shape[0]
    yp, ys = x_prompt, x_sample
    p_states, s_states = [], []
    for l in range(depth):
        wts = _prep_weights(l, g_mix_norm, w_in, g_q_a, w_q_b, g_kv_a, w_kv_b, g_qn, g_qr, g_kn, g_kr,
                            lru_conv_w, lru_conv_b, w_rg, b_rg, w_ig, b_ig, lru_lambda, w_out, g_ffn_norm,
                            w_up, ffn_conv_w, ffn_conv_b, w_down)
        yp, st_p = _layer_prompt(yp, wts, rows=256, tq=512, tk=512)
        ys, st_s = _layer_sample(ys, cache_kv_latent[l], cache_k_rope[l], state_lru_h[l],
                                 state_lru_conv[l], state_ffn_conv[l], wts, tk=512)
        p_states.append(st_p)
        s_states.append(st_s)
    p_out = [jnp.stack([st[j] for st in p_states], axis=0) for j in range(5)]
    s_out = [jnp.stack([st[j] for st in s_states], axis=0) for j in range(5)]
    return (yp, ys, *p_out, *s_out)
```

```python
import functools

import jax
import jax.numpy as jnp
from jax import lax
from jax.experimental import pallas as pl
from jax.experimental.pallas import tpu as pltpu

CHUNK = 64
CHUNK_SHIFT = CHUNK.bit_length() - 1
assert CHUNK == 1 << CHUNK_SHIFT
N_HEADS = 16
NOPE_DIM = 64
ROPE_DIM = 32
V_DIM = 64
QK_DIM = NOPE_DIM + ROPE_DIM
ROPE_THETA = 10000.0
RG_C = 8.0
EPS = 1e-6
LOG2_E = 1.4426950408889634

LANES = 128
SUBLANES = 8
BF16_SUBLANES = 16
MXU_DIM = 256
VMEM_BYTES_V7X = 64 * 1024 * 1024

HEAD_PAD = LANES
HALF_ROPE = ROPE_DIM // 2
F32 = jnp.float32
BF16 = jnp.bfloat16


def _round_up(n, m):
    return (n + m - 1) // m * m


def _vmem_limit(nbytes):
    return int(min(2 * nbytes, VMEM_BYTES_V7X - 8 * 1024 * 1024))


def _const_spec(shape):
    nd = len(shape)
    return pl.BlockSpec(shape, lambda *_: (0,) * nd, pipeline_mode=pl.Buffered(1))


def _dot(a, b):
    return jnp.dot(a, b, preferred_element_type=F32)


def _dot_nt(a, b):
    return lax.dot_general(a, b, (((1,), (1,)), ((), ())), preferred_element_type=F32)


def _rms_rows(x, n):
    return lax.rsqrt(jnp.sum(x * x, axis=-1, keepdims=True) * (1.0 / n) + EPS)


def _rope_head(x, c, s_lo, s_hi):
    return (x * c + pltpu.roll(x, HALF_ROPE, 1) * s_hi
            + pltpu.roll(x, HEAD_PAD - HALF_ROPE, 1) * s_lo)


def _shift_rows(x, n, fill):
    rows, cols = x.shape
    if n % SUBLANES == 0:
        return jnp.concatenate([jnp.full((n, cols), fill, x.dtype), x[:rows - n]], axis=0)
    rolled = pltpu.roll(x, n, 0)
    row = lax.broadcasted_iota(jnp.int32, x.shape, 0)
    return jnp.where(row >= n, rolled, fill)


def _linear_scan(a, b, nb):
    rows = a.shape[0]
    s = nb
    while s < rows:
        b = a * _shift_rows(b, s, 0.0) + b
        if 2 * s < rows:
            a = a * _shift_rows(a, s, 1.0)
        s *= 2
    return b


def _mixer_in_kernel(x_ref, hist_ref, h0_ref, cos_ref, slo_ref, shi_ref,
                     g_mix_ref, w_in_ref, g_qa_ref, wq_ref, gq_ref, g_kva_ref, g_kr_ref,
                     wk_ref, gk_ref, wvt_ref, cw_ref, cb_ref, wlru_ref, brg_ref, big_ref, lam_ref,
                     q_ref, k_ref, vt_ref, lat_ref, kr_ref, ga_ref, rg_ref, hl_ref, cst_ref,
                     xc_ref, hprev_ref, *, rows, nb, hp, d_model, q_lora, kv_lora, conv_w):
    t = pl.program_id(1)

    @pl.when(t == 0)
    def _():
        xc_ref[0:hp, :] = hist_ref[...]
        hprev_ref[...] = h0_ref[...]

    x = x_ref[...]
    xn = (x * _rms_rows(x, d_model) * g_mix_ref[...]).astype(BF16)
    z = _dot(xn, w_in_ref[...])
    o_kv = q_lora
    o_kr = o_kv + kv_lora
    o_u = o_kr + HEAD_PAD
    o_ga = o_u + d_model
    o_gb = o_ga + d_model
    cq = z[:, 0:o_kv]
    ckv = z[:, o_kv:o_kr]
    krb = z[:, o_kr:o_u]
    u = z[:, o_u:o_ga]
    gate_a = z[:, o_ga:o_gb]
    gate_b = z[:, o_gb:o_gb + d_model]

    cos = cos_ref[...]
    s_lo = slo_ref[...]
    s_hi = shi_ref[...]
    lane = lax.broadcasted_iota(jnp.int32, (rows, HEAD_PAD), 1)
    is_nope = lane < NOPE_DIM

    lat = ckv * _rms_rows(ckv, kv_lora) * g_kva_ref[...]
    lat_ref[...] = lat
    kr = _rope_head(krb * _rms_rows(krb, ROPE_DIM) * g_kr_ref[...], cos, s_lo, s_hi)
    kr_ref[...] = kr[:, NOPE_DIM:NOPE_DIM + ROPE_DIM]

    cqn = (cq * _rms_rows(cq, q_lora) * g_qa_ref[...]).astype(BF16)
    qp = _dot(cqn, wq_ref[...])
    for h in range(N_HEADS):
        sl = slice(h * HEAD_PAD, (h + 1) * HEAD_PAD)
        qh = qp[:, sl]
        sq = qh * qh
        r_n = lax.rsqrt(jnp.sum(jnp.where(is_nope, sq, 0.0), axis=-1, keepdims=True) * (1.0 / NOPE_DIM) + EPS)
        r_r = lax.rsqrt(jnp.sum(jnp.where(is_nope, 0.0, sq), axis=-1, keepdims=True) * (1.0 / ROPE_DIM) + EPS)
        qh = qh * jnp.where(is_nope, r_n, r_r) * gq_ref[:, sl]
        q_ref[:, sl] = _rope_head(qh, cos, s_lo, s_hi).astype(BF16)

    latb = lat.astype(BF16)
    kp = _dot(latb, wk_ref[...])
    for h in range(N_HEADS):
        sl = slice(h * HEAD_PAD, (h + 1) * HEAD_PAD)
        kh = kp[:, sl]
        k_ref[:, sl] = (kh * _rms_rows(kh, NOPE_DIM) * gk_ref[:, sl] + kr).astype(BF16)
    vt_ref[...] = _dot_nt(wvt_ref[...], latb).astype(BF16)

    xc_ref[hp:hp + rows, :] = u
    u_c = cb_ref[...] + cw_ref[conv_w - 1:conv_w, :] * u
    for j in range(1, conv_w):
        u_c = u_c + cw_ref[conv_w - 1 - j:conv_w - j, :] * xc_ref[hp - j * nb:hp - j * nb + rows, :]
    tail = xc_ref[rows:rows + hp, :]
    cst_ref[...] = tail
    xc_ref[0:hp, :] = tail

    lam = lam_ref[...]
    softplus_neg = jnp.maximum(-lam, 0.0) + jnp.log1p(jnp.exp(-jnp.abs(lam)))
    n_blocks, blk_w, _ = wlru_ref.shape
    a_parts, b_parts = [], []
    for n in range(n_blocks):
        sl = slice(n * blk_w, (n + 1) * blk_w)
        ucn = u_c[:, sl]
        g = _dot(ucn.astype(BF16), wlru_ref[n])
        r = jax.nn.sigmoid(g[:, 0:blk_w] + brg_ref[:, sl])
        i = jax.nn.sigmoid(g[:, blk_w:2 * blk_w] + big_ref[:, sl])
        log_a = (-RG_C * r) * softplus_neg[:, sl]
        a_n = jnp.exp(log_a)
        a_parts.append(a_n)
        b_parts.append(jnp.sqrt(-jnp.tanh(log_a) * (1.0 + a_n * a_n)) * (i * ucn))
    a = jnp.concatenate(a_parts, axis=1)
    b = jnp.concatenate(b_parts, axis=1)
    if nb == 1:
        first = lax.broadcasted_iota(jnp.int32, a.shape, 0) < nb
        b = b + jnp.where(first, a * hprev_ref[...], 0.0)
    else:
        b = jnp.concatenate([b[0:nb] + a[0:nb] * hprev_ref[...], b[nb:]], axis=0)
    hseq = _linear_scan(a, b, nb)
    h_last = hseq[rows - nb:rows]
    hprev_ref[...] = h_last
    hl_ref[...] = h_last

    ga_ref[...] = jax.nn.sigmoid(gate_a).astype(BF16)
    rg_ref[...] = (jax.nn.sigmoid(gate_b) * hseq).astype(BF16)


def _mixer_in(x, hist, h0, rope, wts, *, rows, nb):
    groups, tg, d_model = x.shape
    hp = hist.shape[1]
    q_lora = wts["wq"].shape[0]
    kv_lora = wts["wk"].shape[0]
    conv_w = wts["cw"].shape[0]
    n_t = tg // rows
    qk_w = N_HEADS * HEAD_PAD
    v_w = N_HEADS * V_DIM

    def row_spec(width):
        return pl.BlockSpec((None, rows, width), lambda g, t: (g, t, 0))

    def group_spec(r, width):
        return pl.BlockSpec((None, r, width), lambda g, t: (g, 0, 0))

    tab_spec = pl.BlockSpec((rows, HEAD_PAD), lambda g, t: (t, 0))
    w_names = ["g_mix", "w_in", "g_qa", "wq", "gq", "g_kva", "g_kr", "wk", "gk", "wvt",
               "cw", "cb", "wlru", "brg", "big", "lam"]
    w_list = [wts[n] for n in w_names]
    in_specs = ([row_spec(d_model), group_spec(hp, d_model), group_spec(nb, d_model),
                 tab_spec, tab_spec, tab_spec] + [_const_spec(w.shape) for w in w_list])
    out_shape = (
        jax.ShapeDtypeStruct((groups, tg, qk_w), BF16),
        jax.ShapeDtypeStruct((groups, tg, qk_w), BF16),
        jax.ShapeDtypeStruct((groups, n_t, v_w, rows), BF16),
        jax.ShapeDtypeStruct((groups, tg, kv_lora), F32),
        jax.ShapeDtypeStruct((groups, tg, ROPE_DIM), F32),
        jax.ShapeDtypeStruct((groups, tg, d_model), BF16),
        jax.ShapeDtypeStruct((groups, tg, d_model), BF16),
        jax.ShapeDtypeStruct((groups, nb, d_model), F32),
        jax.ShapeDtypeStruct((groups, hp, d_model), F32),
    )
    vt_spec = pl.BlockSpec((None, None, v_w, rows), lambda g, t: (g, t, 0, 0))
    out_specs = (row_spec(qk_w), row_spec(qk_w), vt_spec, row_spec(kv_lora), row_spec(ROPE_DIM),
                 row_spec(d_model), row_spec(d_model), group_spec(nb, d_model), group_spec(hp, d_model))
    w_bytes = sum(w.size * w.dtype.itemsize for w in w_list)
    io_bytes = 2 * rows * (4 * d_model + 2 * (2 * qk_w + v_w + 2 * d_model) + 4 * (kv_lora + LANES) + 12 * LANES)
    tmp_bytes = 4 * rows * (wts["w_in"].shape[1] + 2 * qk_w + 6 * d_model) + 4 * (hp + rows) * d_model
    kern = functools.partial(_mixer_in_kernel, rows=rows, nb=nb, hp=hp, d_model=d_model,
                             q_lora=q_lora, kv_lora=kv_lora, conv_w=conv_w)
    return pl.pallas_call(
        kern,
        grid=(groups, n_t),
        in_specs=in_specs,
        out_specs=out_specs,
        out_shape=out_shape,
        scratch_shapes=[pltpu.VMEM((hp + rows, d_model), F32), pltpu.VMEM((nb, d_model), F32)],
        compiler_params=pltpu.CompilerParams(
            dimension_semantics=("parallel", "arbitrary"),
            vmem_limit_bytes=_vmem_limit(w_bytes + io_bytes + tmp_bytes)),
        name="mixer_in",
    )(x, hist, h0, *rope, *w_list)


def _attn_kernel(q_ref, k_ref, vt_ref, ga_ref, rg_ref, o_ref, m_ref, acc_ref, carry_ref, *, tile, cq, vblk):
    qi = pl.program_id(2)
    m_ref[...] = jnp.full(m_ref.shape, -jnp.inf, F32)
    acc_ref[...] = jnp.zeros(acc_ref.shape, F32)
    ones_rows = jnp.ones((acc_ref.shape[1] - V_DIM, vblk), BF16)

    chains = [(hh, c) for hh in range(2) for c in range(tile // cq)]

    def scores(kj, hh, c, n_keys):
        lanes = slice(hh * HEAD_PAD, (hh + 1) * HEAD_PAD)
        k0 = pl.multiple_of(kj * tile, tile)
        return _dot_nt(k_ref[pl.ds(k0, n_keys), lanes], q_ref[c * cq:(c + 1) * cq, lanes])

    def diag_mask(s, c):
        k_chunk = lax.broadcasted_iota(jnp.int32, s.shape, 0) >> CHUNK_SHIFT
        q_chunk = (c * cq + lax.broadcasted_iota(jnp.int32, s.shape, 1)) >> CHUNK_SHIFT
        return jnp.where(k_chunk <= q_chunk, s, -jnp.inf)

    def softmax(hh, c, s):
        qs = slice(c * cq, (c + 1) * cq)
        m_prev = m_ref[hh, :, qs]
        m_new = jnp.maximum(m_prev, jnp.max(s, axis=0, keepdims=True))
        m_ref[hh, :, qs] = m_new
        return jnp.exp2(s - m_new).astype(BF16), jnp.exp2(m_prev - m_new)

    def values(kj, hh, c, pb, alpha):
        qs = slice(c * cq, (c + 1) * cq)
        pv = None
        for j in range(pb.shape[0] // vblk):
            vt = vt_ref[kj * (tile // vblk) + j, hh * V_DIM:(hh + 1) * V_DIM, :]
            part = _dot(jnp.concatenate([vt, ones_rows], axis=0), pb[j * vblk:(j + 1) * vblk])
            pv = part if pv is None else pv + part
        acc_ref[hh, :, qs] = alpha * acc_ref[hh, :, qs] + pv

    def run_tiles(tiles, last_on_diagonal):
        items = [(kj, hh, c, last_on_diagonal and t == len(tiles) - 1)
                 for t, kj in enumerate(tiles) for hh, c in chains]

        def keys_of(item):
            return (item[2] + 1) * cq if item[3] else tile

        s_next = carry_ref[0:keys_of(items[0]), :]
        prev = None
        for i, (kj, hh, c, diag) in enumerate(items):
            s_cur = s_next
            if i + 1 < len(items):
                s_next = scores(*items[i + 1][:3], keys_of(items[i + 1]))
            elif not last_on_diagonal:
                carry_ref[...] = scores(kj + 1, *chains[0], tile)
            if diag:
                s_cur = diag_mask(s_cur, c)
            cur = softmax(hh, c, s_cur)
            if prev is not None:
                values(*items[i - 1][:3], *prev)
            prev = cur
        values(*items[-1][:3], *prev)

    carry_ref[...] = scores(0, *chains[0], tile)

    @pl.loop(0, qi >> 1)
    def _(j):
        run_tiles([2 * j, 2 * j + 1], False)

    @pl.when((qi & 1) == 1)
    def _():
        run_tiles([qi - 1], False)

    run_tiles([qi], True)

    attn_t = jnp.concatenate([acc_ref[hh, 0:V_DIM, :] / acc_ref[hh, V_DIM:V_DIM + 1, :] for hh in range(2)],
                             axis=0)
    o_ref[...] = (ga_ref[...].astype(F32) * attn_t.T + rg_ref[...].astype(F32)).astype(BF16)


def _attention(q, k, vt, ga, rg, *, tile):
    bsz, t, _ = q.shape
    n_vb, _, vblk = vt.shape[1:]
    assert LANES == 2 * V_DIM and N_HEADS % 2 == 0
    assert k.shape[1] == t == n_vb * vblk and tile % vblk == 0 and t % tile == 0
    assert tile % CHUNK == 0 and tile % MXU_DIM == 0
    n_hp = N_HEADS // 2
    qk_blk = 2 * HEAD_PAD

    def q_index(b, p, qi):
        return (b, qi, p)

    kern = functools.partial(_attn_kernel, tile=tile, cq=MXU_DIM, vblk=vblk)
    blk_bytes = 2 * 2 * (tile * qk_blk + t * qk_blk + t * LANES + 3 * tile * LANES)
    tmp_bytes = 4 * 8 * tile * MXU_DIM + 4 * 4 * tile * LANES
    return pl.pallas_call(
        kern,
        grid=(bsz, n_hp, t // tile),
        in_specs=[pl.BlockSpec((None, tile, qk_blk), q_index),
                  pl.BlockSpec((None, t, qk_blk), lambda b, p, qi: (b, 0, p)),
                  pl.BlockSpec((None, n_vb, LANES, vblk), lambda b, p, qi: (b, 0, p, 0)),
                  pl.BlockSpec((None, tile, LANES), q_index),
                  pl.BlockSpec((None, tile, LANES), q_index)],
        out_specs=pl.BlockSpec((None, tile, LANES), q_index),
        out_shape=jax.ShapeDtypeStruct((bsz, t, N_HEADS * V_DIM), BF16),
        scratch_shapes=[pltpu.VMEM((2, 1, tile), F32),
                        pltpu.VMEM((2, V_DIM + BF16_SUBLANES, tile), F32),
                        pltpu.VMEM((tile, MXU_DIM), F32)],
        compiler_params=pltpu.CompilerParams(
            dimension_semantics=("parallel", "parallel", "arbitrary"),
            vmem_limit_bytes=_vmem_limit(blk_bytes + tmp_bytes)),
        name="attention",
    )(q, k, vt, ga, rg)


def _cache_attn_kernel(q_ref, latc_ref, krc_ref, latn_ref, krn_ref, ga_ref, rg_ref,
                       wkg_ref, wkc_ref, seg_ref, expand_ref, wv_ref, o_ref,
                       qabs_ref, qr_ref, m_ref, l_ref, acc_ref, *, t, tk, n_new):
    past = latc_ref.shape[0]
    hq = N_HEADS * t
    for h in range(N_HEADS):
        qh = q_ref[:, h * HEAD_PAD:(h + 1) * HEAD_PAD]
        qabs_ref[h * t:(h + 1) * t, :] = _dot(qh, wkg_ref[h]).astype(BF16)
        qr_ref[h * t:(h + 1) * t, :] = qh[:, NOPE_DIM:QK_DIM]
    m_ref[...] = jnp.full(m_ref.shape, -jnp.inf, F32)
    l_ref[...] = jnp.zeros(l_ref.shape, F32)
    acc_ref[...] = jnp.zeros(acc_ref.shape, F32)

    def key_rows(lat, kr, n_valid):
        n = lat.shape[0]
        latb = lat.astype(BF16)
        kvk = _dot(latb, wkc_ref[...])
        ssum = _dot((kvk * kvk).astype(BF16), seg_ref[...])
        r = lax.rsqrt(ssum * (1.0 / NOPE_DIM) + EPS)
        r_hi = r.astype(BF16)
        r_lo = (r - r_hi.astype(F32)).astype(BF16)
        r_cols = _dot(jnp.concatenate([r_hi, r_lo], axis=1), expand_ref[...])
        s = _dot_nt(latb, qabs_ref[...]) * r_cols + _dot_nt(kr.astype(BF16), qr_ref[...])
        if n_valid < n:
            s = jnp.where(lax.broadcasted_iota(jnp.int32, (n, hq), 0) < n_valid, s, -jnp.inf)
        m_prev = m_ref[...]
        m_new = jnp.maximum(m_prev, jnp.max(s, axis=0, keepdims=True))
        alpha = jnp.exp2(m_prev - m_new)
        p = jnp.exp2(s - m_new)
        l_ref[...] = alpha * l_ref[...] + jnp.sum(p, axis=0, keepdims=True)
        m_ref[...] = m_new
        acc_ref[...] = alpha * acc_ref[...] + _dot(lat.T.astype(BF16), p.astype(BF16))

    def body(kj, carry):
        k0 = pl.multiple_of(kj * tk, tk)
        key_rows(latc_ref[pl.ds(k0, tk), :], krc_ref[pl.ds(k0, tk), :], tk)
        return carry

    lax.fori_loop(0, past // tk, body, 0)
    key_rows(latn_ref[...], krn_ref[...], n_new)

    ctx = (acc_ref[...] / l_ref[...]).T.astype(BF16)
    first_head = lax.broadcasted_iota(jnp.int32, (t, LANES), 1) < V_DIM
    for pair in range(N_HEADS // 2):
        cols = slice(pair * LANES, (pair + 1) * LANES)
        wv = wv_ref[:, cols]
        a0 = _dot(ctx[(2 * pair) * t:(2 * pair + 1) * t], wv)
        a1 = _dot(ctx[(2 * pair + 1) * t:(2 * pair + 2) * t], wv)
        attn = jnp.where(first_head, a0, a1)
        o_ref[:, cols] = (ga_ref[:, cols].astype(F32) * attn + rg_ref[:, cols].astype(F32)).astype(BF16)


def _cache_attention(q, lat_cache, kr_cache, lat_new, kr_new, ga, rg, wts, *, tk, n_new):
    bsz, t, qk_w = q.shape
    past, kv_lora = lat_cache.shape[1:]
    n_pad = lat_new.shape[1]
    d_model = ga.shape[2]
    hq = N_HEADS * t
    assert past % tk == 0 and LANES == 2 * V_DIM
    w_list = [wts["wkg"], wts["wkc"], wts["seg"], wts["expand"], wts["wv"]]
    w_bytes = sum(w.size * w.dtype.itemsize for w in w_list)
    blk_bytes = 2 * (4 * (past + n_pad) * (kv_lora + LANES) + 2 * t * (qk_w + 3 * d_model))
    tmp_bytes = 4 * tk * (2 * N_HEADS * NOPE_DIM + 4 * hq)

    def batch_spec(r, width):
        return pl.BlockSpec((None, r, width), lambda b: (b, 0, 0))

    kern = functools.partial(_cache_attn_kernel, t=t, tk=tk, n_new=n_new)
    return pl.pallas_call(
        kern,
        grid=(bsz,),
        in_specs=[batch_spec(t, qk_w), batch_spec(past, kv_lora), batch_spec(past, ROPE_DIM),
                  batch_spec(n_pad, kv_lora), batch_spec(n_pad, ROPE_DIM),
                  batch_spec(t, d_model), batch_spec(t, d_model)] + [_const_spec(w.shape) for w in w_list],
        out_specs=batch_spec(t, d_model),
        out_shape=jax.ShapeDtypeStruct((bsz, t, d_model), BF16),
        scratch_shapes=[pltpu.VMEM((hq, kv_lora), BF16), pltpu.VMEM((hq, ROPE_DIM), BF16),
                        pltpu.VMEM((1, hq), F32), pltpu.VMEM((1, hq), F32), pltpu.VMEM((kv_lora, hq), F32)],
        compiler_params=pltpu.CompilerParams(
            dimension_semantics=("parallel",),
            vmem_limit_bytes=_vmem_limit(w_bytes + blk_bytes + tmp_bytes)),
        name="cache_attention",
    )(q, lat_cache, kr_cache, lat_new, kr_new, ga, rg, *w_list)


def _mixer_out_kernel(x_ref, mix_ref, hist_ref, w_out_ref, g_ffn_ref, w_up_ref, fw_ref, fb_ref, w_down_ref,
                      y_ref, fst_ref, upc_ref, *, rows, nb, hp, d_model, d_ff, conv_w, col_blk):
    t = pl.program_id(1)

    @pl.when(t == 0)
    def _():
        upc_ref[0:hp, :] = hist_ref[...]

    x1 = x_ref[...] + _dot(mix_ref[...], w_out_ref[...])
    xn = (x1 * _rms_rows(x1, d_model) * g_ffn_ref[...]).astype(BF16)

    def conv_cols(c0):
        sl = slice(c0, c0 + col_blk)
        up = _dot(xn, w_up_ref[:, sl])
        upc_ref[hp:hp + rows, sl] = up
        out = fb_ref[:, sl] + fw_ref[conv_w - 1:conv_w, sl] * up
        for j in range(1, conv_w):
            out = out + fw_ref[conv_w - 1 - j:conv_w - j, sl] * upc_ref[hp - j * nb:hp - j * nb + rows, sl]
        return out

    y = x1
    for c in range(d_ff // col_blk):
        gate = conv_cols(c * col_blk)
        val = conv_cols(d_ff + c * col_blk)
        hmid = (gate * jax.nn.sigmoid(gate) * val).astype(BF16)
        y = y + _dot(hmid, w_down_ref[c * col_blk:(c + 1) * col_blk, :])
    y_ref[...] = y

    tail = upc_ref[rows:rows + hp, :]
    fst_ref[...] = tail
    upc_ref[0:hp, :] = tail


def _mixer_out(x, mixed, hist, wts, *, rows, nb):
    groups, tg, d_model = x.shape
    hp = hist.shape[1]
    d_ff = wts["w_down"].shape[0]
    conv_w = wts["fw"].shape[0]
    n_t = tg // rows
    col_blk = MXU_DIM

    def row_spec(width):
        return pl.BlockSpec((None, rows, width), lambda g, t: (g, t, 0))

    def group_spec(r, width):
        return pl.BlockSpec((None, r, width), lambda g, t: (g, 0, 0))

    w_names = ["w_out", "g_ffn", "w_up", "fw", "fb", "w_down"]
    w_list = [wts[n] for n in w_names]
    w_bytes = sum(w.size * w.dtype.itemsize for w in w_list)
    blk_bytes = 2 * rows * d_model * (4 + 2 + 4) + 4 * 4 * hp * 2 * d_ff
    tmp_bytes = 4 * (hp + rows) * 2 * d_ff + 4 * rows * (3 * d_model + 8 * col_blk)
    kern = functools.partial(_mixer_out_kernel, rows=rows, nb=nb, hp=hp, d_model=d_model, d_ff=d_ff,
                             conv_w=conv_w, col_blk=col_blk)
    return pl.pallas_call(
        kern,
        grid=(groups, n_t),
        in_specs=[row_spec(d_model), row_spec(d_model), group_spec(hp, 2 * d_ff)]
                 + [_const_spec(w.shape) for w in w_list],
        out_specs=(row_spec(d_model), group_spec(hp, 2 * d_ff)),
        out_shape=(jax.ShapeDtypeStruct((groups, tg, d_model), F32),
                   jax.ShapeDtypeStruct((groups, hp, 2 * d_ff), F32)),
        scratch_shapes=[pltpu.VMEM((hp + rows, 2 * d_ff), F32)],
        compiler_params=pltpu.CompilerParams(
            dimension_semantics=("parallel", "arbitrary"),
            vmem_limit_bytes=_vmem_limit(w_bytes + blk_bytes + tmp_bytes)),
        name="mixer_out",
    )(x, mixed, hist, *w_list)


def _head_pad(w, widths):
    lead = w.shape[:-1]
    per_head = sum(widths)
    w = w.reshape(lead + (N_HEADS, per_head))
    w = jnp.pad(w, [(0, 0)] * len(lead) + [(0, 0), (0, HEAD_PAD - per_head)])
    return w.reshape(lead + (N_HEADS * HEAD_PAD,))


def _prep_weights(l, g_mix_norm, w_in, g_q_a, w_q_b, g_kv_a, w_kv_b, g_qn, g_qr, g_kn, g_kr,
                  lru_conv_w, lru_conv_b, w_rg, b_rg, w_ig, b_ig, lru_lambda, w_out, g_ffn_norm,
                  w_up, ffn_conv_w, ffn_conv_b, w_down):
    d_model = w_in.shape[1]
    q_lora = w_q_b.shape[1]
    kv_lora = w_kv_b.shape[1]
    row = lambda a: a.reshape(1, -1)
    o_kr = q_lora + kv_lora
    wi = w_in[l]
    w_kr = jnp.pad(wi[:, o_kr:o_kr + ROPE_DIM], ((0, 0), (NOPE_DIM, HEAD_PAD - QK_DIM)))
    w_in_p = jnp.concatenate([wi[:, :o_kr], w_kr, wi[:, o_kr + ROPE_DIM:]], axis=1).astype(BF16)
    scale = QK_DIM ** -0.5 * LOG2_E
    gq = jnp.tile(jnp.pad(jnp.concatenate([g_qn[l], g_qr[l]]) * scale, (0, HEAD_PAD - QK_DIM)), N_HEADS)
    gk = jnp.tile(jnp.pad(g_kn[l], (0, HEAD_PAD - NOPE_DIM)), N_HEADS)
    kv = w_kv_b[l].reshape(kv_lora, N_HEADS, NOPE_DIM + V_DIM)
    wk = _head_pad(kv[:, :, :NOPE_DIM].reshape(kv_lora, N_HEADS * NOPE_DIM), (NOPE_DIM,))
    wv = kv[:, :, NOPE_DIM:].reshape(kv_lora, N_HEADS * V_DIM)
    wkg = jnp.pad(jnp.transpose(kv[:, :, :NOPE_DIM], (1, 2, 0)) * g_kn[l][None, :, None],
                  ((0, 0), (0, HEAD_PAD - NOPE_DIM), (0, 0)))
    seg = jnp.pad(jnp.repeat(jnp.eye(N_HEADS, dtype=F32), NOPE_DIM, axis=0), ((0, 0), (0, LANES - N_HEADS)))
    return {
        "wkg": wkg.astype(BF16), "wkc": kv[:, :, :NOPE_DIM].reshape(kv_lora, -1).astype(BF16),
        "seg": seg.astype(BF16), "wv": wv.astype(BF16),
        "g_mix": row(g_mix_norm[l]), "w_in": w_in_p, "g_qa": row(g_q_a[l]),
        "wq": _head_pad(w_q_b[l], (NOPE_DIM, ROPE_DIM)).astype(BF16), "gq": row(gq),
        "g_kva": row(g_kv_a[l]), "g_kr": row(jnp.pad(g_kr[l], (NOPE_DIM, HEAD_PAD - QK_DIM))),
        "wk": wk.astype(BF16), "gk": row(gk), "wvt": wv.T.astype(BF16),
        "cw": lru_conv_w[l], "cb": row(lru_conv_b[l]),
        "wlru": jnp.concatenate([w_rg[l], w_ig[l]], axis=-1).astype(BF16),
        "brg": row(b_rg[l]), "big": row(b_ig[l]), "lam": row(lru_lambda[l]),
        "w_out": w_out[l].astype(BF16), "g_ffn": row(g_ffn_norm[l]), "w_up": w_up[l].astype(BF16),
        "fw": ffn_conv_w[l], "fb": row(ffn_conv_b[l]), "w_down": w_down[l].astype(BF16),
    }


def _rope_tables(pos):
    inv = ROPE_THETA ** (-jnp.arange(0, ROPE_DIM, 2, dtype=F32) / ROPE_DIM)
    ang = pos.astype(F32)[:, None] * inv[None, :]
    cos, sin = jnp.cos(ang), jnp.sin(ang)
    n = pos.shape[0]
    ones_lo = jnp.ones((n, NOPE_DIM), F32)
    zeros_lo = jnp.zeros((n, NOPE_DIM), F32)
    zeros_half = jnp.zeros((n, HALF_ROPE), F32)
    tail = jnp.zeros((n, HEAD_PAD - QK_DIM), F32)
    c = jnp.concatenate([ones_lo, cos, cos, tail], axis=1)
    s_lo = jnp.concatenate([zeros_lo, -sin, zeros_half, tail], axis=1)
    s_hi = jnp.concatenate([zeros_lo, zeros_half, sin, tail], axis=1)
    return c, s_lo, s_hi


def _expand_matrix(t):
    one_part = jnp.pad(jnp.repeat(jnp.eye(N_HEADS, dtype=F32), t, axis=1), ((0, LANES - N_HEADS), (0, 0)))
    return jnp.concatenate([one_part, one_part], axis=0).astype(BF16)


def _front_pad_rows(a, hp):
    return jnp.pad(a, ((0, 0), (hp - a.shape[1], 0), (0, 0)))


def _layer_prompt(x, wts, *, rows, attn_tile):
    bsz, t, d_model = x.shape
    lru_w = wts["cw"].shape[0]
    ffn_w = wts["fw"].shape[0]
    d_ff2 = wts["w_up"].shape[1]
    hp1 = _round_up(lru_w - 1, SUBLANES)
    hp2 = _round_up(ffn_w - 1, SUBLANES)
    rope = _rope_tables(jnp.arange(t, dtype=jnp.int32))
    q, k, vt, lat, kr, ga, rg, h_last, cst = _mixer_in(
        x, jnp.zeros((bsz, hp1, d_model), F32), jnp.zeros((bsz, 1, d_model), F32), rope, wts, rows=rows, nb=1)
    mixed = _attention(q, k, vt, ga, rg, tile=attn_tile)
    y, fst = _mixer_out(x, mixed, jnp.zeros((bsz, hp2, d_ff2), F32), wts, rows=rows, nb=1)
    return y, (lat, kr, h_last[:, 0], cst[:, hp1 - (lru_w - 1):], fst[:, hp2 - (ffn_w - 1):])


def _layer_sample(x, past_lat, past_kr, h0, lru_buf, ffn_buf, wts, *, tk):
    bsz, t, d_model = x.shape
    past = past_lat.shape[1]
    lru_w = wts["cw"].shape[0]
    ffn_w = wts["fw"].shape[0]
    hp1 = _round_up((lru_w - 1) * bsz, SUBLANES)
    hp2 = _round_up((ffn_w - 1) * bsz, SUBLANES)
    rows = t * bsz

    def to_tm(a):
        return jnp.swapaxes(a, 0, 1).reshape(1, a.shape[1] * bsz, a.shape[2])

    def from_tm(a):
        return jnp.swapaxes(a.reshape(a.shape[1] // bsz, bsz, a.shape[2]), 0, 1)

    pos = past + jnp.arange(t, dtype=jnp.int32)
    rope = _rope_tables(jnp.repeat(pos, bsz))
    q, k, vt, lat, kr, ga, rg, h_last, cst = _mixer_in(
        to_tm(x), _front_pad_rows(to_tm(lru_buf), hp1), h0[None], rope, wts, rows=rows, nb=bsz)

    assert (past % CHUNK) + t <= CHUNK, "cache attention assumes all keys visible to all queries"
    n_pad = _round_up(t, LANES)
    pad_rows = lambda a: jnp.pad(from_tm(a), ((0, 0), (0, n_pad - t), (0, 0)))
    cache_wts = dict(wts, expand=_expand_matrix(t))
    mixed = _cache_attention(from_tm(q), past_lat, past_kr, pad_rows(lat), pad_rows(kr),
                             from_tm(ga), from_tm(rg), cache_wts, tk=tk, n_new=t)

    y, fst = _mixer_out(to_tm(x), to_tm(mixed), _front_pad_rows(to_tm(ffn_buf), hp2), wts, rows=rows, nb=bsz)
    states = (from_tm(lat), from_tm(kr), h_last[0],
              from_tm(cst[:, hp1 - (lru_w - 1) * bsz:]), from_tm(fst[:, hp2 - (ffn_w - 1) * bsz:]))
    return from_tm(y), states


def kernel(x_prompt, x_sample, cache_kv_latent, cache_k_rope, state_lru_h, state_lru_conv, state_ffn_conv,
           g_mix_norm, w_in, g_q_a, w_q_b, g_kv_a, w_kv_b, g_qn, g_qr, g_kn, g_kr, lru_conv_w, lru_conv_b,
           w_rg, b_rg, w_ig, b_ig, lru_lambda, w_out, g_ffn_norm, w_up, ffn_conv_w, ffn_conv_b, w_down):
    depth = w_in.shape[0]
    yp, ys = x_prompt, x_sample
    p_states, s_states = [], []
    for l in range(depth):
        wts = _prep_weights(l, g_mix_norm, w_in, g_q_a, w_q_b, g_kv_a, w_kv_b, g_qn, g_qr, g_kn, g_kr,
                            lru_conv_w, lru_conv_b, w_rg, b_rg, w_ig, b_ig, lru_lambda, w_out, g_ffn_norm,
                            w_up, ffn_conv_w, ffn_conv_b, w_down)
        yp, st_p = _layer_prompt(yp, wts, rows=256, attn_tile=512)
        ys, st_s = _layer_sample(ys, cache_kv_latent[l], cache_k_rope[l], state_lru_h[l],
                                 state_lru_conv[l], state_ffn_conv[l], wts, tk=512)
        p_states.append(st_p)
        s_states.append(st_s)
    p_out = [jnp.stack([st[j] for st in p_states], axis=0) for j in range(5)]
    s_out = [jnp.stack([st[j] for st in s_states], axis=0) for j in range(5)]
    return (yp, ys, *p_out, *s_out)
```

```python
import functools

import jax
import jax.numpy as jnp
from jax import lax
from jax.experimental import pallas as pl
from jax.experimental.pallas import tpu as pltpu

CHUNK = 64
CHUNK_SHIFT = CHUNK.bit_length() - 1
assert CHUNK == 1 << CHUNK_SHIFT
N_HEADS = 16
NOPE_DIM = 64
ROPE_DIM = 32
V_DIM = 64
QK_DIM = NOPE_DIM + ROPE_DIM
ROPE_THETA = 10000.0
RG_C = 8.0
EPS = 1e-6
LOG2_E = 1.4426950408889634

LANES = 128
SUBLANES = 8
BF16_SUBLANES = 16
MXU_DIM = 256
VMEM_BYTES_V7X = 64 * 1024 * 1024

HEAD_PAD = LANES
HALF_ROPE = ROPE_DIM // 2
F32 = jnp.float32
BF16 = jnp.bfloat16


def _round_up(n, m):
    return (n + m - 1) // m * m


def _vmem_limit(nbytes):
    return int(min(2 * nbytes, VMEM_BYTES_V7X - 8 * 1024 * 1024))


def _const_spec(shape):
    nd = len(shape)
    return pl.BlockSpec(shape, lambda *_: (0,) * nd, pipeline_mode=pl.Buffered(1))


def _dot(a, b):
    return jnp.dot(a, b, preferred_element_type=F32)


def _dot_nt(a, b):
    return lax.dot_general(a, b, (((1,), (1,)), ((), ())), preferred_element_type=F32)


def _sigmoid(x):
    return 0.5 * jnp.tanh(0.5 * x) + 0.5


def _rms_rows(x, n):
    return lax.rsqrt(jnp.sum(x * x, axis=-1, keepdims=True) * (1.0 / n) + EPS)


def _rope_head(x, c, s_lo, s_hi):
    return (x * c + pltpu.roll(x, HALF_ROPE, 1) * s_hi
            + pltpu.roll(x, HEAD_PAD - HALF_ROPE, 1) * s_lo)


def _shift_rows(x, n, fill):
    rows, cols = x.shape
    if n % SUBLANES == 0:
        return jnp.concatenate([jnp.full((n, cols), fill, x.dtype), x[:rows - n]], axis=0)
    rolled = pltpu.roll(x, n, 0)
    row = lax.broadcasted_iota(jnp.int32, x.shape, 0)
    return jnp.where(row >= n, rolled, fill)


def _linear_scan(a, b, nb):
    rows = a.shape[0]
    s = nb
    while s < rows:
        b = a * _shift_rows(b, s, 0.0) + b
        if 2 * s < rows:
            a = a * _shift_rows(a, s, 1.0)
        s *= 2
    return b


def _mixer_in_kernel(x_ref, hist_ref, h0_ref, cos_ref, slo_ref, shi_ref,
                     g_mix_ref, w_in_ref, g_qa_ref, wq_ref, gq_ref, g_kva_ref, g_kr_ref,
                     wk_ref, gk_ref, wvt_ref, cw_ref, cb_ref, wlru_ref, brg_ref, big_ref, lam_ref, *rest,
                     rows, nb, hp, d_model, q_lora, kv_lora, conv_w):
    if nb == 1:
        perm_ref, *rest = rest
    q_ref, k_ref, vt_ref, lat_ref, kr_ref, ga_ref, rg_ref, hl_ref, cst_ref, xh_ref, hprev_ref = rest
    t = pl.program_id(1)

    @pl.when(t == 0)
    def _():
        xh_ref[...] = hist_ref[...]
        hprev_ref[...] = h0_ref[...]

    o_kv = q_lora
    o_kr = o_kv + kv_lora
    o_u = o_kr + HEAD_PAD
    o_ga = o_u + d_model
    o_gb = o_ga + d_model
    cos = cos_ref[...]
    s_lo = slo_ref[...]
    s_hi = shi_ref[...]
    is_nope = lax.broadcasted_iota(jnp.int32, (rows, HEAD_PAD), 1) < NOPE_DIM

    x = x_ref[...]
    xn = (x * _rms_rows(x, d_model) * g_mix_ref[...]).astype(BF16)

    xs = _dot(perm_ref[0], xn).astype(BF16) if nb == 1 else xn

    def in_proj(lhs, c0, c1):
        return _dot(lhs, w_in_ref[:, c0:c1])

    z_lat = in_proj(xn, 0, o_u)
    u = in_proj(xs, o_u, o_ga)

    cq = z_lat[:, 0:o_kv]
    ckv = z_lat[:, o_kv:o_kr]
    krb = z_lat[:, o_kr:o_u]
    lat = ckv * _rms_rows(ckv, kv_lora) * g_kva_ref[...]
    lat_ref[...] = lat
    latb = lat.astype(BF16)
    kr = _rope_head(krb * _rms_rows(krb, ROPE_DIM) * g_kr_ref[...], cos, s_lo, s_hi)
    kr_ref[...] = kr[:, NOPE_DIM:NOPE_DIM + ROPE_DIM]
    cqn = (cq * _rms_rows(cq, q_lora) * g_qa_ref[...]).astype(BF16)

    qp = _dot(cqn, wq_ref[...])
    gate_a = in_proj(xn, o_ga, o_gb)

    def q_heads():
        for h in range(N_HEADS):
            sl = slice(h * HEAD_PAD, (h + 1) * HEAD_PAD)
            qh = qp[:, sl]
            sq = qh * qh
            r_n = lax.rsqrt(jnp.sum(jnp.where(is_nope, sq, 0.0), axis=-1, keepdims=True)
                            * (1.0 / NOPE_DIM) + EPS)
            r_r = lax.rsqrt(jnp.sum(jnp.where(is_nope, 0.0, sq), axis=-1, keepdims=True)
                            * (1.0 / ROPE_DIM) + EPS)
            qh = qh * jnp.where(is_nope, r_n, r_r) * gq_ref[:, sl]
            q_ref[:, sl] = _rope_head(qh, cos, s_lo, s_hi).astype(BF16)

    def k_heads(kp):
        for h in range(N_HEADS):
            sl = slice(h * HEAD_PAD, (h + 1) * HEAD_PAD)
            kh = kp[:, sl]
            k_ref[:, sl] = (kh * _rms_rows(kh, NOPE_DIM) * gk_ref[:, sl] + kr).astype(BF16)

    hgt = SUBLANES if nb == 1 else nb
    steps = rows // hgt
    x_slabs = {g: u[g * hgt:(g + 1) * hgt] for g in range(steps)}
    first_run = lax.broadcasted_iota(jnp.int32, (hgt, d_model), 0) == 0
    for j in range(1, conv_w):
        if nb == 1:
            x_slabs[-j] = jnp.where(first_run, xh_ref[hp - j:hp - j + 1, :], pltpu.roll(x_slabs[steps - j], 1, 0))
        else:
            x_slabs[-j] = xh_ref[hp - j * nb:hp - (j - 1) * nb, :]
    uc_slabs = []
    for g in range(steps):
        acc = cb_ref[...] + cw_ref[conv_w - 1:conv_w, :] * x_slabs[g]
        for j in range(1, conv_w):
            acc = acc + cw_ref[conv_w - 1 - j:conv_w - j, :] * x_slabs[g - j]
        uc_slabs.append(acc)
    u_c = jnp.concatenate(uc_slabs, axis=0)
    if nb == 1:
        row = lax.broadcasted_iota(jnp.int32, (hp, d_model), 0)
        tail = jnp.zeros((hp, d_model), F32)
        for j in range(1, conv_w):
            last = jnp.broadcast_to(x_slabs[steps - j][hgt - 1:hgt], (hp, d_model))
            tail = jnp.where(row == hp - j, last, tail)
    else:
        tail = u[rows - hp:rows]
    cst_ref[...] = tail
    xh_ref[...] = tail

    n_blocks, blk_w, _ = wlru_ref.shape
    gate_proj = [_dot(u_c[:, n * blk_w:(n + 1) * blk_w].astype(BF16), wlru_ref[n])
                 for n in range(n_blocks)]
    q_heads()

    kp = _dot(latb, wk_ref[...])
    vt_ref[...] = _dot_nt(wvt_ref[...], latb).astype(BF16)
    gate_b = in_proj(xs, o_gb, o_gb + d_model)

    lam = lam_ref[...]
    softplus_neg = jnp.maximum(-lam, 0.0) + jnp.log1p(jnp.exp(-jnp.abs(lam)))
    a_parts, b_parts = [], []
    for n in range(n_blocks):
        sl = slice(n * blk_w, (n + 1) * blk_w)
        ucn = u_c[:, sl]
        g = gate_proj[n]
        r = _sigmoid(g[:, 0:blk_w] + brg_ref[:, sl])
        i = _sigmoid(g[:, blk_w:2 * blk_w] + big_ref[:, sl])
        log_a = (-RG_C * r) * softplus_neg[:, sl]
        a_n = jnp.exp(log_a)
        a_parts.append(a_n)
        b_parts.append(jnp.sqrt(-jnp.tanh(log_a) * (1.0 + a_n * a_n)) * (i * ucn))
    a = jnp.concatenate(a_parts, axis=1)
    b = jnp.concatenate(b_parts, axis=1)

    a_g = a[0:hgt]
    h_in = a_g * hprev_ref[...]
    if nb == 1:
        h_in = jnp.where(lax.broadcasted_iota(jnp.int32, h_in.shape, 0) == 0, h_in, 0.0)
    h = b[0:hgt] + h_in
    a_run = a_g
    h_slabs, a_slabs = [h], [a_run]
    for g in range(1, steps):
        a_g = a[g * hgt:(g + 1) * hgt]
        h = a_g * h + b[g * hgt:(g + 1) * hgt]
        h_slabs.append(h)
        if nb == 1:
            a_run = a_g * a_run
            a_slabs.append(a_run)
    if nb == 1:
        run_end = _linear_scan(a_run, h, 1)
        run_in = _shift_rows(run_end, 1, 0.0)
        h_slabs = [h_g + a_r * run_in for h_g, a_r in zip(h_slabs, a_slabs)]
        h_last = run_end[hgt - 1:hgt]
    else:
        h_last = h_slabs[-1]
    hprev_ref[...] = h_last
    hl_ref[...] = h_last
    rg = (_sigmoid(gate_b) * jnp.concatenate(h_slabs, axis=0)).astype(BF16)
    if nb == 1:
        rg = _dot(perm_ref[1], rg).astype(BF16)
    rg_ref[...] = rg

    k_heads(kp)
    ga_ref[...] = _sigmoid(gate_a).astype(BF16)


def _slab_permutation(rows):
    steps = rows // SUBLANES
    slab_row = jnp.arange(rows)
    time = (slab_row % SUBLANES) * steps + slab_row // SUBLANES
    gather = (time[:, None] == jnp.arange(rows)[None, :]).astype(BF16)
    return jnp.stack([gather, gather.T])


def _mixer_in(x, hist, h0, rope, wts, *, rows, nb):
    groups, tg, d_model = x.shape
    hp = hist.shape[1]
    q_lora = wts["wq"].shape[0]
    kv_lora = wts["wk"].shape[0]
    conv_w = wts["cw"].shape[0]
    n_t = tg // rows
    qk_w = N_HEADS * HEAD_PAD
    v_w = N_HEADS * V_DIM

    def row_spec(width):
        return pl.BlockSpec((None, rows, width), lambda g, t: (g, t, 0))

    def group_spec(r, width):
        return pl.BlockSpec((None, r, width), lambda g, t: (g, 0, 0))

    tab_spec = pl.BlockSpec((rows, HEAD_PAD), lambda g, t: (t, 0))
    w_names = ["g_mix", "w_in", "g_qa", "wq", "gq", "g_kva", "g_kr", "wk", "gk", "wvt",
               "cw", "cb", "wlru", "brg", "big", "lam"]
    w_list = [wts[n] for n in w_names] + ([_slab_permutation(rows)] if nb == 1 else [])
    in_specs = ([row_spec(d_model), group_spec(hp, d_model), group_spec(nb, d_model),
                 tab_spec, tab_spec, tab_spec] + [_const_spec(w.shape) for w in w_list])
    out_shape = (
        jax.ShapeDtypeStruct((groups, tg, qk_w), BF16),
        jax.ShapeDtypeStruct((groups, tg, qk_w), BF16),
        jax.ShapeDtypeStruct((groups, n_t, v_w, rows), BF16),
        jax.ShapeDtypeStruct((groups, tg, kv_lora), F32),
        jax.ShapeDtypeStruct((groups, tg, ROPE_DIM), F32),
        jax.ShapeDtypeStruct((groups, tg, d_model), BF16),
        jax.ShapeDtypeStruct((groups, tg, d_model), BF16),
        jax.ShapeDtypeStruct((groups, nb, d_model), F32),
        jax.ShapeDtypeStruct((groups, hp, d_model), F32),
    )
    vt_spec = pl.BlockSpec((None, None, v_w, rows), lambda g, t: (g, t, 0, 0))
    out_specs = (row_spec(qk_w), row_spec(qk_w), vt_spec, row_spec(kv_lora), row_spec(ROPE_DIM),
                 row_spec(d_model), row_spec(d_model), group_spec(nb, d_model), group_spec(hp, d_model))
    w_bytes = sum(w.size * w.dtype.itemsize for w in w_list)
    io_bytes = 2 * rows * (4 * d_model + 2 * (2 * qk_w + v_w + 2 * d_model) + 4 * (kv_lora + LANES) + 12 * LANES)
    tmp_bytes = 4 * rows * (wts["w_in"].shape[1] + 2 * qk_w + 6 * d_model) + 4 * (hp + rows) * d_model
    kern = functools.partial(_mixer_in_kernel, rows=rows, nb=nb, hp=hp, d_model=d_model,
                             q_lora=q_lora, kv_lora=kv_lora, conv_w=conv_w)
    return pl.pallas_call(
        kern,
        grid=(groups, n_t),
        in_specs=in_specs,
        out_specs=out_specs,
        out_shape=out_shape,
        scratch_shapes=[pltpu.VMEM((hp, d_model), F32), pltpu.VMEM((nb, d_model), F32)],
        compiler_params=pltpu.CompilerParams(
            dimension_semantics=("parallel", "arbitrary"),
            vmem_limit_bytes=_vmem_limit(w_bytes + io_bytes + tmp_bytes)),
        name="mixer_in",
    )(x, hist, h0, *rope, *w_list)


def _attn_kernel(q_ref, k_ref, vt_ref, ga_ref, rg_ref, o_ref, m_ref, acc_ref, carry_ref, *, tile, cq, vblk):
    qi = pl.program_id(2)
    m_ref[...] = jnp.full(m_ref.shape, -jnp.inf, F32)
    acc_ref[...] = jnp.zeros(acc_ref.shape, F32)
    ones_rows = jnp.ones((acc_ref.shape[1] - V_DIM, vblk), BF16)

    chains = [(hh, c) for hh in range(2) for c in range(tile // cq)]

    def scores(kj, hh, c, n_keys):
        lanes = slice(hh * HEAD_PAD, (hh + 1) * HEAD_PAD)
        k0 = pl.multiple_of(kj * tile, tile)
        return _dot_nt(k_ref[pl.ds(k0, n_keys), lanes], q_ref[c * cq:(c + 1) * cq, lanes])

    def diag_mask(s, c):
        k_chunk = lax.broadcasted_iota(jnp.int32, s.shape, 0) >> CHUNK_SHIFT
        q_chunk = (c * cq + lax.broadcasted_iota(jnp.int32, s.shape, 1)) >> CHUNK_SHIFT
        return jnp.where(k_chunk <= q_chunk, s, -jnp.inf)

    def softmax(hh, c, s):
        qs = slice(c * cq, (c + 1) * cq)
        m_prev = m_ref[hh, :, qs]
        m_new = jnp.maximum(m_prev, jnp.max(s, axis=0, keepdims=True))
        m_ref[hh, :, qs] = m_new
        return jnp.exp2(s - m_new).astype(BF16), jnp.exp2(m_prev - m_new)

    def values(kj, hh, c, pb, alpha):
        qs = slice(c * cq, (c + 1) * cq)
        pv = None
        for j in range(pb.shape[0] // vblk):
            vt = vt_ref[kj * (tile // vblk) + j, hh * V_DIM:(hh + 1) * V_DIM, :]
            part = _dot(jnp.concatenate([vt, ones_rows], axis=0), pb[j * vblk:(j + 1) * vblk])
            pv = part if pv is None else pv + part
        acc_ref[hh, :, qs] = alpha * acc_ref[hh, :, qs] + pv

    def run_tiles(tiles, last_on_diagonal):
        items = [(kj, hh, c, last_on_diagonal and t == len(tiles) - 1)
                 for t, kj in enumerate(tiles) for hh, c in chains]

        def keys_of(item):
            return (item[2] + 1) * cq if item[3] else tile

        s_next = carry_ref[0:keys_of(items[0]), :]
        prev = None
        for i, (kj, hh, c, diag) in enumerate(items):
            s_cur = s_next
            if i + 1 < len(items):
                s_next = scores(*items[i + 1][:3], keys_of(items[i + 1]))
            elif not last_on_diagonal:
                carry_ref[...] = scores(kj + 1, *chains[0], tile)
            if diag:
                s_cur = diag_mask(s_cur, c)
            cur = softmax(hh, c, s_cur)
            if prev is not None:
                values(*items[i - 1][:3], *prev)
            prev = cur
        values(*items[-1][:3], *prev)

    carry_ref[...] = scores(0, *chains[0], tile)

    @pl.loop(0, qi >> 1)
    def _(j):
        run_tiles([2 * j, 2 * j + 1], False)

    @pl.when((qi & 1) == 1)
    def _():
        run_tiles([qi - 1], False)

    run_tiles([qi], True)

    attn_t = jnp.concatenate([acc_ref[hh, 0:V_DIM, :] / acc_ref[hh, V_DIM:V_DIM + 1, :] for hh in range(2)],
                             axis=0)
    o_ref[...] = (ga_ref[...].astype(F32) * attn_t.T + rg_ref[...].astype(F32)).astype(BF16)


def _attention(q, k, vt, ga, rg, *, tile):
    bsz, t, _ = q.shape
    n_vb, _, vblk = vt.shape[1:]
    assert LANES == 2 * V_DIM and N_HEADS % 2 == 0
    assert k.shape[1] == t == n_vb * vblk and tile % vblk == 0 and t % tile == 0
    assert tile % CHUNK == 0 and tile % MXU_DIM == 0
    n_hp = N_HEADS // 2
    qk_blk = 2 * HEAD_PAD

    def q_index(b, p, qi):
        return (b, qi, p)

    kern = functools.partial(_attn_kernel, tile=tile, cq=MXU_DIM, vblk=vblk)
    blk_bytes = 2 * 2 * (tile * qk_blk + t * qk_blk + t * LANES + 3 * tile * LANES)
    tmp_bytes = 4 * 8 * tile * MXU_DIM + 4 * 4 * tile * LANES
    return pl.pallas_call(
        kern,
        grid=(bsz, n_hp, t // tile),
        in_specs=[pl.BlockSpec((None, tile, qk_blk), q_index),
                  pl.BlockSpec((None, t, qk_blk), lambda b, p, qi: (b, 0, p)),
                  pl.BlockSpec((None, n_vb, LANES, vblk), lambda b, p, qi: (b, 0, p, 0)),
                  pl.BlockSpec((None, tile, LANES), q_index),
                  pl.BlockSpec((None, tile, LANES), q_index)],
        out_specs=pl.BlockSpec((None, tile, LANES), q_index),
        out_shape=jax.ShapeDtypeStruct((bsz, t, N_HEADS * V_DIM), BF16),
        scratch_shapes=[pltpu.VMEM((2, 1, tile), F32),
                        pltpu.VMEM((2, V_DIM + BF16_SUBLANES, tile), F32),
                        pltpu.VMEM((tile, MXU_DIM), F32)],
        compiler_params=pltpu.CompilerParams(
            dimension_semantics=("parallel", "parallel", "arbitrary"),
            vmem_limit_bytes=_vmem_limit(blk_bytes + tmp_bytes)),
        name="attention",
    )(q, k, vt, ga, rg)


def _cache_attn_kernel(q_ref, latc_ref, krc_ref, latn_ref, krn_ref, ga_ref, rg_ref,
                       wkg_ref, wkc_ref, seg_ref, expand_ref, wv_ref, o_ref,
                       qabs_ref, qr_ref, m_ref, l_ref, acc_ref, *, t, tk, n_new):
    past = latc_ref.shape[0]
    hq = N_HEADS * t
    for h in range(N_HEADS):
        qh = q_ref[:, h * HEAD_PAD:(h + 1) * HEAD_PAD]
        qabs_ref[h * t:(h + 1) * t, :] = _dot(qh, wkg_ref[h]).astype(BF16)
        qr_ref[h * t:(h + 1) * t, :] = qh[:, NOPE_DIM:QK_DIM]
    m_ref[...] = jnp.full(m_ref.shape, -jnp.inf, F32)
    l_ref[...] = jnp.zeros(l_ref.shape, F32)
    acc_ref[...] = jnp.zeros(acc_ref.shape, F32)

    def key_rows(lat, kr, n_valid):
        n = lat.shape[0]
        latb = lat.astype(BF16)
        kvk = _dot(latb, wkc_ref[...])
        ssum = _dot((kvk * kvk).astype(BF16), seg_ref[...])
        r = lax.rsqrt(ssum * (1.0 / NOPE_DIM) + EPS)
        r_hi = r.astype(BF16)
        r_lo = (r - r_hi.astype(F32)).astype(BF16)
        r_cols = _dot(jnp.concatenate([r_hi, r_lo], axis=1), expand_ref[...])
        s = _dot_nt(latb, qabs_ref[...]) * r_cols + _dot_nt(kr.astype(BF16), qr_ref[...])
        if n_valid < n:
            s = jnp.where(lax.broadcasted_iota(jnp.int32, (n, hq), 0) < n_valid, s, -jnp.inf)
        m_prev = m_ref[...]
        m_new = jnp.maximum(m_prev, jnp.max(s, axis=0, keepdims=True))
        alpha = jnp.exp2(m_prev - m_new)
        p = jnp.exp2(s - m_new)
        l_ref[...] = alpha * l_ref[...] + jnp.sum(p, axis=0, keepdims=True)
        m_ref[...] = m_new
        acc_ref[...] = alpha * acc_ref[...] + _dot(lat.T.astype(BF16), p.astype(BF16))

    def body(kj, carry):
        k0 = pl.multiple_of(kj * tk, tk)
        key_rows(latc_ref[pl.ds(k0, tk), :], krc_ref[pl.ds(k0, tk), :], tk)
        return carry

    lax.fori_loop(0, past // tk, body, 0)
    key_rows(latn_ref[...], krn_ref[...], n_new)

    ctx = (acc_ref[...] / l_ref[...]).T.astype(BF16)
    first_head = lax.broadcasted_iota(jnp.int32, (t, LANES), 1) < V_DIM
    for pair in range(N_HEADS // 2):
        cols = slice(pair * LANES, (pair + 1) * LANES)
        wv = wv_ref[:, cols]
        a0 = _dot(ctx[(2 * pair) * t:(2 * pair + 1) * t], wv)
        a1 = _dot(ctx[(2 * pair + 1) * t:(2 * pair + 2) * t], wv)
        attn = jnp.where(first_head, a0, a1)
        o_ref[:, cols] = (ga_ref[:, cols].astype(F32) * attn + rg_ref[:, cols].astype(F32)).astype(BF16)


def _cache_attention(q, lat_cache, kr_cache, lat_new, kr_new, ga, rg, wts, *, tk, n_new):
    bsz, t, qk_w = q.shape
    past, kv_lora = lat_cache.shape[1:]
    n_pad = lat_new.shape[1]
    d_model = ga.shape[2]
    hq = N_HEADS * t
    assert past % tk == 0 and LANES == 2 * V_DIM
    w_list = [wts["wkg"], wts["wkc"], wts["seg"], wts["expand"], wts["wv"]]
    w_bytes = sum(w.size * w.dtype.itemsize for w in w_list)
    blk_bytes = 2 * (4 * (past + n_pad) * (kv_lora + LANES) + 2 * t * (qk_w + 3 * d_model))
    tmp_bytes = 4 * tk * (2 * N_HEADS * NOPE_DIM + 4 * hq)

    def batch_spec(r, width):
        return pl.BlockSpec((None, r, width), lambda b: (b, 0, 0))

    kern = functools.partial(_cache_attn_kernel, t=t, tk=tk, n_new=n_new)
    return pl.pallas_call(
        kern,
        grid=(bsz,),
        in_specs=[batch_spec(t, qk_w), batch_spec(past, kv_lora), batch_spec(past, ROPE_DIM),
                  batch_spec(n_pad, kv_lora), batch_spec(n_pad, ROPE_DIM),
                  batch_spec(t, d_model), batch_spec(t, d_model)] + [_const_spec(w.shape) for w in w_list],
        out_specs=batch_spec(t, d_model),
        out_shape=jax.ShapeDtypeStruct((bsz, t, d_model), BF16),
        scratch_shapes=[pltpu.VMEM((hq, kv_lora), BF16), pltpu.VMEM((hq, ROPE_DIM), BF16),
                        pltpu.VMEM((1, hq), F32), pltpu.VMEM((1, hq), F32), pltpu.VMEM((kv_lora, hq), F32)],
        compiler_params=pltpu.CompilerParams(
            dimension_semantics=("parallel",),
            vmem_limit_bytes=_vmem_limit(w_bytes + blk_bytes + tmp_bytes)),
        name="cache_attention",
    )(q, lat_cache, kr_cache, lat_new, kr_new, ga, rg, *w_list)


def _mixer_out_kernel(x_ref, mix_ref, hist_ref, w_out_ref, g_ffn_ref, w_up_ref, fw_ref, fb_ref, w_down_ref,
                      y_ref, fst_ref, upc_ref, *, rows, nb, hp, d_model, d_ff, conv_w, col_blk):
    t = pl.program_id(1)

    @pl.when(t == 0)
    def _():
        upc_ref[0:hp, :] = hist_ref[...]

    x1 = x_ref[...] + _dot(mix_ref[...], w_out_ref[...])
    xn = (x1 * _rms_rows(x1, d_model) * g_ffn_ref[...]).astype(BF16)

    def up_proj(c):
        for c0 in (c * col_blk, d_ff + c * col_blk):
            sl = slice(c0, c0 + col_blk)
            upc_ref[hp:hp + rows, sl] = _dot(xn, w_up_ref[:, sl])

    def conv_cols(c0):
        sl = slice(c0, c0 + col_blk)
        out = fb_ref[:, sl] + fw_ref[conv_w - 1:conv_w, sl] * upc_ref[hp:hp + rows, sl]
        for j in range(1, conv_w):
            out = out + fw_ref[conv_w - 1 - j:conv_w - j, sl] * upc_ref[hp - j * nb:hp - j * nb + rows, sl]
        return out

    def gated(c):
        gate = conv_cols(c * col_blk)
        val = conv_cols(d_ff + c * col_blk)
        return (gate * _sigmoid(gate) * val).astype(BF16)

    def down_proj(c, hmid):
        return _dot(hmid, w_down_ref[c * col_blk:(c + 1) * col_blk, :])

    n_chunks = d_ff // col_blk
    y = x1
    up_proj(0)
    prev = None
    for c in range(n_chunks):
        if c + 1 < n_chunks:
            up_proj(c + 1)
        hmid = gated(c)
        if prev is not None:
            y = y + down_proj(c - 1, prev)
        prev = hmid
    y_ref[...] = y + down_proj(n_chunks - 1, prev)

    tail = upc_ref[rows:rows + hp, :]
    fst_ref[...] = tail
    upc_ref[0:hp, :] = tail


def _mixer_out(x, mixed, hist, wts, *, rows, nb):
    groups, tg, d_model = x.shape
    hp = hist.shape[1]
    d_ff = wts["w_down"].shape[0]
    conv_w = wts["fw"].shape[0]
    n_t = tg // rows
    col_blk = MXU_DIM

    def row_spec(width):
        return pl.BlockSpec((None, rows, width), lambda g, t: (g, t, 0))

    def group_spec(r, width):
        return pl.BlockSpec((None, r, width), lambda g, t: (g, 0, 0))

    w_names = ["w_out", "g_ffn", "w_up", "fw", "fb", "w_down"]
    w_list = [wts[n] for n in w_names]
    w_bytes = sum(w.size * w.dtype.itemsize for w in w_list)
    blk_bytes = 2 * rows * d_model * (4 + 2 + 4) + 4 * 4 * hp * 2 * d_ff
    tmp_bytes = 4 * (hp + rows) * 2 * d_ff + 4 * rows * (3 * d_model + 8 * col_blk)
    kern = functools.partial(_mixer_out_kernel, rows=rows, nb=nb, hp=hp, d_model=d_model, d_ff=d_ff,
                             conv_w=conv_w, col_blk=col_blk)
    return pl.pallas_call(
        kern,
        grid=(groups, n_t),
        in_specs=[row_spec(d_model), row_spec(d_model), group_spec(hp, 2 * d_ff)]
                 + [_const_spec(w.shape) for w in w_list],
        out_specs=(row_spec(d_model), group_spec(hp, 2 * d_ff)),
        out_shape=(jax.ShapeDtypeStruct((groups, tg, d_model), F32),
                   jax.ShapeDtypeStruct((groups, hp, 2 * d_ff), F32)),
        scratch_shapes=[pltpu.VMEM((hp + rows, 2 * d_ff), F32)],
        compiler_params=pltpu.CompilerParams(
            dimension_semantics=("parallel", "arbitrary"),
            vmem_limit_bytes=_vmem_limit(w_bytes + blk_bytes + tmp_bytes)),
        name="mixer_out",
    )(x, mixed, hist, *w_list)


def _head_pad(w, widths):
    lead = w.shape[:-1]
    per_head = sum(widths)
    w = w.reshape(lead + (N_HEADS, per_head))
    w = jnp.pad(w, [(0, 0)] * len(lead) + [(0, 0), (0, HEAD_PAD - per_head)])
    return w.reshape(lead + (N_HEADS * HEAD_PAD,))


def _prep_weights(l, g_mix_norm, w_in, g_q_a, w_q_b, g_kv_a, w_kv_b, g_qn, g_qr, g_kn, g_kr,
                  lru_conv_w, lru_conv_b, w_rg, b_rg, w_ig, b_ig, lru_lambda, w_out, g_ffn_norm,
                  w_up, ffn_conv_w, ffn_conv_b, w_down):
    d_model = w_in.shape[1]
    q_lora = w_q_b.shape[1]
    kv_lora = w_kv_b.shape[1]
    row = lambda a: a.reshape(1, -1)
    o_kr = q_lora + kv_lora
    wi = w_in[l]
    w_kr = jnp.pad(wi[:, o_kr:o_kr + ROPE_DIM], ((0, 0), (NOPE_DIM, HEAD_PAD - QK_DIM)))
    w_in_p = jnp.concatenate([wi[:, :o_kr], w_kr, wi[:, o_kr + ROPE_DIM:]], axis=1).astype(BF16)
    scale = QK_DIM ** -0.5 * LOG2_E
    gq = jnp.tile(jnp.pad(jnp.concatenate([g_qn[l], g_qr[l]]) * scale, (0, HEAD_PAD - QK_DIM)), N_HEADS)
    gk = jnp.tile(jnp.pad(g_kn[l], (0, HEAD_PAD - NOPE_DIM)), N_HEADS)
    kv = w_kv_b[l].reshape(kv_lora, N_HEADS, NOPE_DIM + V_DIM)
    wk = _head_pad(kv[:, :, :NOPE_DIM].reshape(kv_lora, N_HEADS * NOPE_DIM), (NOPE_DIM,))
    wv = kv[:, :, NOPE_DIM:].reshape(kv_lora, N_HEADS * V_DIM)
    wkg = jnp.pad(jnp.transpose(kv[:, :, :NOPE_DIM], (1, 2, 0)) * g_kn[l][None, :, None],
                  ((0, 0), (0, HEAD_PAD - NOPE_DIM), (0, 0)))
    seg = jnp.pad(jnp.repeat(jnp.eye(N_HEADS, dtype=F32), NOPE_DIM, axis=0), ((0, 0), (0, LANES - N_HEADS)))
    return {
        "wkg": wkg.astype(BF16), "wkc": kv[:, :, :NOPE_DIM].reshape(kv_lora, -1).astype(BF16),
        "seg": seg.astype(BF16), "wv": wv.astype(BF16),
        "g_mix": row(g_mix_norm[l]), "w_in": w_in_p, "g_qa": row(g_q_a[l]),
        "wq": _head_pad(w_q_b[l], (NOPE_DIM, ROPE_DIM)).astype(BF16), "gq": row(gq),
        "g_kva": row(g_kv_a[l]), "g_kr": row(jnp.pad(g_kr[l], (NOPE_DIM, HEAD_PAD - QK_DIM))),
        "wk": wk.astype(BF16), "gk": row(gk), "wvt": wv.T.astype(BF16),
        "cw": lru_conv_w[l], "cb": row(lru_conv_b[l]),
        "wlru": jnp.concatenate([w_rg[l], w_ig[l]], axis=-1).astype(BF16),
        "brg": row(b_rg[l]), "big": row(b_ig[l]), "lam": row(lru_lambda[l]),
        "w_out": w_out[l].astype(BF16), "g_ffn": row(g_ffn_norm[l]), "w_up": w_up[l].astype(BF16),
        "fw": ffn_conv_w[l], "fb": row(ffn_conv_b[l]), "w_down": w_down[l].astype(BF16),
    }


def _rope_tables(pos):
    inv = ROPE_THETA ** (-jnp.arange(0, ROPE_DIM, 2, dtype=F32) / ROPE_DIM)
    ang = pos.astype(F32)[:, None] * inv[None, :]
    cos, sin = jnp.cos(ang), jnp.sin(ang)
    n = pos.shape[0]
    ones_lo = jnp.ones((n, NOPE_DIM), F32)
    zeros_lo = jnp.zeros((n, NOPE_DIM), F32)
    zeros_half = jnp.zeros((n, HALF_ROPE), F32)
    tail = jnp.zeros((n, HEAD_PAD - QK_DIM), F32)
    c = jnp.concatenate([ones_lo, cos, cos, tail], axis=1)
    s_lo = jnp.concatenate([zeros_lo, -sin, zeros_half, tail], axis=1)
    s_hi = jnp.concatenate([zeros_lo, zeros_half, sin, tail], axis=1)
    return c, s_lo, s_hi


def _expand_matrix(t):
    one_part = jnp.pad(jnp.repeat(jnp.eye(N_HEADS, dtype=F32), t, axis=1), ((0, LANES - N_HEADS), (0, 0)))
    return jnp.concatenate([one_part, one_part], axis=0).astype(BF16)


def _front_pad_rows(a, hp):
    return jnp.pad(a, ((0, 0), (hp - a.shape[1], 0), (0, 0)))


def _layer_prompt(x, wts, *, rows, out_rows, attn_tile):
    bsz, t, d_model = x.shape
    lru_w = wts["cw"].shape[0]
    ffn_w = wts["fw"].shape[0]
    d_ff2 = wts["w_up"].shape[1]
    hp1 = _round_up(lru_w - 1, SUBLANES)
    hp2 = _round_up(ffn_w - 1, SUBLANES)
    rope = _rope_tables(jnp.arange(t, dtype=jnp.int32))
    q, k, vt, lat, kr, ga, rg, h_last, cst = _mixer_in(
        x, jnp.zeros((bsz, hp1, d_model), F32), jnp.zeros((bsz, 1, d_model), F32), rope, wts, rows=rows, nb=1)
    mixed = _attention(q, k, vt, ga, rg, tile=attn_tile)
    y, fst = _mixer_out(x, mixed, jnp.zeros((bsz, hp2, d_ff2), F32), wts, rows=out_rows, nb=1)
    return y, (lat, kr, h_last[:, 0], cst[:, hp1 - (lru_w - 1):], fst[:, hp2 - (ffn_w - 1):])


def _layer_sample(x, past_lat, past_kr, h0, lru_buf, ffn_buf, wts, *, tk):
    bsz, t, d_model = x.shape
    past = past_lat.shape[1]
    lru_w = wts["cw"].shape[0]
    ffn_w = wts["fw"].shape[0]
    hp1 = _round_up((lru_w - 1) * bsz, SUBLANES)
    hp2 = _round_up((ffn_w - 1) * bsz, SUBLANES)
    rows = t * bsz

    def to_tm(a):
        return jnp.swapaxes(a, 0, 1).reshape(1, a.shape[1] * bsz, a.shape[2])

    def from_tm(a):
        return jnp.swapaxes(a.reshape(a.shape[1] // bsz, bsz, a.shape[2]), 0, 1)

    pos = past + jnp.arange(t, dtype=jnp.int32)
    rope = _rope_tables(jnp.repeat(pos, bsz))
    q, k, vt, lat, kr, ga, rg, h_last, cst = _mixer_in(
        to_tm(x), _front_pad_rows(to_tm(lru_buf), hp1), h0[None], rope, wts, rows=rows, nb=bsz)

    assert (past % CHUNK) + t <= CHUNK, "cache attention assumes all keys visible to all queries"
    n_pad = _round_up(t, LANES)
    pad_rows = lambda a: jnp.pad(from_tm(a), ((0, 0), (0, n_pad - t), (0, 0)))
    cache_wts = dict(wts, expand=_expand_matrix(t))
    mixed = _cache_attention(from_tm(q), past_lat, past_kr, pad_rows(lat), pad_rows(kr),
                             from_tm(ga), from_tm(rg), cache_wts, tk=tk, n_new=t)

    y, fst = _mixer_out(to_tm(x), to_tm(mixed), _front_pad_rows(to_tm(ffn_buf), hp2), wts, rows=rows, nb=bsz)
    states = (from_tm(lat), from_tm(kr), h_last[0],
              from_tm(cst[:, hp1 - (lru_w - 1) * bsz:]), from_tm(fst[:, hp2 - (ffn_w - 1) * bsz:]))
    return from_tm(y), states


def kernel(x_prompt, x_sample, cache_kv_latent, cache_k_rope, state_lru_h, state_lru_conv, state_ffn_conv,
           g_mix_norm, w_in, g_q_a, w_q_b, g_kv_a, w_kv_b, g_qn, g_qr, g_kn, g_kr, lru_conv_w, lru_conv_b,
           w_rg, b_rg, w_ig, b_ig, lru_lambda, w_out, g_ffn_norm, w_up, ffn_conv_w, ffn_conv_b, w_down):
    depth = w_in.shape[0]
    yp, ys = x_prompt, x_sample
    p_states, s_states = [], []
    for l in range(depth):
        wts = _prep_weights(l, g_mix_norm, w_in, g_q_a, w_q_b, g_kv_a, w_kv_b, g_qn, g_qr, g_kn, g_kr,
                            lru_conv_w, lru_conv_b, w_rg, b_rg, w_ig, b_ig, lru_lambda, w_out, g_ffn_norm,
                            w_up, ffn_conv_w, ffn_conv_b, w_down)
        yp, st_p = _layer_prompt(yp, wts, rows=256, out_rows=512, attn_tile=512)
        ys, st_s = _layer_sample(ys, cache_kv_latent[l], cache_k_rope[l], state_lru_h[l],
                                 state_lru_conv[l], state_ffn_conv[l], wts, tk=512)
        p_states.append(st_p)
        s_states.append(st_s)
    p_out = [jnp.stack([st[j] for st in p_states], axis=0) for j in range(5)]
    s_out = [jnp.stack([st[j] for st in s_states], axis=0) for j in range(5)]
    return (yp, ys, *p_out, *s_out)
```

```python
import functools

import jax
import jax.numpy as jnp
import numpy as np
from jax import lax
from jax.experimental import pallas as pl
from jax.experimental.pallas import tpu as pltpu

CHUNK = 64
CHUNK_SHIFT = CHUNK.bit_length() - 1
assert CHUNK == 1 << CHUNK_SHIFT
N_HEADS = 16
NOPE_DIM = 64
ROPE_DIM = 32
V_DIM = 64
QK_DIM = NOPE_DIM + ROPE_DIM
ROPE_THETA = 10000.0
RG_C = 8.0
EPS = 1e-6
LOG2_E = 1.4426950408889634

LANES = 128
SUBLANES = 8
BF16_SUBLANES = 16
MXU_DIM = 256
VMEM_BYTES_V7X = 64 * 1024 * 1024

HEAD_PAD = LANES
HALF_ROPE = ROPE_DIM // 2
F32 = jnp.float32
BF16 = jnp.bfloat16


def _round_up(n, m):
    return (n + m - 1) // m * m


def _vmem_limit(nbytes):
    return int(min(2 * nbytes, VMEM_BYTES_V7X - 8 * 1024 * 1024))


def _const_spec(shape):
    nd = len(shape)
    return pl.BlockSpec(shape, lambda *_: (0,) * nd, pipeline_mode=pl.Buffered(1))


def _dot(a, b):
    return jnp.dot(a, b, preferred_element_type=F32)


def _dot_nt(a, b):
    return lax.dot_general(a, b, (((1,), (1,)), ((), ())), preferred_element_type=F32)


def _sigmoid(x):
    return 0.5 * jnp.tanh(0.5 * x) + 0.5


def _rms_rows(x, n):
    return lax.rsqrt(jnp.sum(x * x, axis=-1, keepdims=True) * (1.0 / n) + EPS)


def _rope_head(x, c, s_lo, s_hi):
    return (x * c + pltpu.roll(x, HALF_ROPE, 1) * s_hi
            + pltpu.roll(x, HEAD_PAD - HALF_ROPE, 1) * s_lo)


def _shift_rows(x, n, fill):
    rows, cols = x.shape
    if n % SUBLANES == 0:
        return jnp.concatenate([jnp.full((n, cols), fill, x.dtype), x[:rows - n]], axis=0)
    rolled = pltpu.roll(x, n, 0)
    row = lax.broadcasted_iota(jnp.int32, x.shape, 0)
    return jnp.where(row >= n, rolled, fill)


def _linear_scan(a, b, nb):
    rows = a.shape[0]
    s = nb
    while s < rows:
        b = a * _shift_rows(b, s, 0.0) + b
        if 2 * s < rows:
            a = a * _shift_rows(a, s, 1.0)
        s *= 2
    return b


def _mixer_in_kernel(x_ref, hist_ref, h0_ref, cos_ref, slo_ref, shi_ref,
                     g_mix_ref, w_in_ref, g_qa_ref, wq_ref, gq_ref, g_kva_ref, g_kr_ref,
                     wk_ref, gk_ref, wvt_ref, cw_ref, cb_ref, wlru_ref, brg_ref, big_ref, lam_ref, *rest,
                     rows, nb, hp, d_model, q_lora, kv_lora, conv_w):
    if nb == 1:
        perm_ref, *rest = rest
    q_ref, k_ref, vt_ref, lat_ref, kr_ref, ga_ref, rg_ref, hl_ref, cst_ref, xh_ref, hprev_ref = rest
    t = pl.program_id(1)

    @pl.when(t == 0)
    def _():
        xh_ref[...] = hist_ref[...]
        hprev_ref[...] = h0_ref[...]

    o_kv = q_lora
    o_kr = o_kv + kv_lora
    o_u = o_kr + HEAD_PAD
    o_ga = o_u + d_model
    o_gb = o_ga + d_model
    cos = cos_ref[...]
    s_lo = slo_ref[...]
    s_hi = shi_ref[...]
    is_nope = lax.broadcasted_iota(jnp.int32, (rows, HEAD_PAD), 1) < NOPE_DIM

    x = x_ref[...]
    xn = (x * _rms_rows(x, d_model) * g_mix_ref[...]).astype(BF16)

    xs = _dot(perm_ref[0], xn).astype(BF16) if nb == 1 else xn

    def in_proj(lhs, c0, c1):
        return _dot(lhs, w_in_ref[:, c0:c1])

    z_lat = in_proj(xn, 0, o_u)
    u = in_proj(xs, o_u, o_ga)

    cq = z_lat[:, 0:o_kv]
    ckv = z_lat[:, o_kv:o_kr]
    krb = z_lat[:, o_kr:o_u]
    lat = ckv * _rms_rows(ckv, kv_lora) * g_kva_ref[...]
    lat_ref[...] = lat
    latb = lat.astype(BF16)
    kr = _rope_head(krb * _rms_rows(krb, ROPE_DIM) * g_kr_ref[...], cos, s_lo, s_hi)
    kr_ref[...] = kr[:, NOPE_DIM:NOPE_DIM + ROPE_DIM]
    cqn = (cq * _rms_rows(cq, q_lora) * g_qa_ref[...]).astype(BF16)

    qp = _dot(cqn, wq_ref[...])
    gate_a = in_proj(xn, o_ga, o_gb)

    def q_heads():
        for h in range(N_HEADS):
            sl = slice(h * HEAD_PAD, (h + 1) * HEAD_PAD)
            qh = qp[:, sl]
            sq = qh * qh
            r_n = lax.rsqrt(jnp.sum(jnp.where(is_nope, sq, 0.0), axis=-1, keepdims=True)
                            * (1.0 / NOPE_DIM) + EPS)
            r_r = lax.rsqrt(jnp.sum(jnp.where(is_nope, 0.0, sq), axis=-1, keepdims=True)
                            * (1.0 / ROPE_DIM) + EPS)
            qh = qh * jnp.where(is_nope, r_n, r_r) * gq_ref[:, sl]
            q_ref[:, sl] = _rope_head(qh, cos, s_lo, s_hi).astype(BF16)

    def k_heads(kp):
        for h in range(N_HEADS):
            sl = slice(h * HEAD_PAD, (h + 1) * HEAD_PAD)
            kh = kp[:, sl]
            k_ref[:, sl] = (kh * _rms_rows(kh, NOPE_DIM) * gk_ref[:, sl] + kr).astype(BF16)

    hgt = SUBLANES if nb == 1 else nb
    steps = rows // hgt
    x_slabs = {g: u[g * hgt:(g + 1) * hgt] for g in range(steps)}
    first_run = lax.broadcasted_iota(jnp.int32, (hgt, d_model), 0) == 0
    for j in range(1, conv_w):
        if nb == 1:
            x_slabs[-j] = jnp.where(first_run, xh_ref[hp - j:hp - j + 1, :], pltpu.roll(x_slabs[steps - j], 1, 0))
        else:
            x_slabs[-j] = xh_ref[hp - j * nb:hp - (j - 1) * nb, :]
    uc_slabs = []
    for g in range(steps):
        acc = cb_ref[...] + cw_ref[conv_w - 1:conv_w, :] * x_slabs[g]
        for j in range(1, conv_w):
            acc = acc + cw_ref[conv_w - 1 - j:conv_w - j, :] * x_slabs[g - j]
        uc_slabs.append(acc)
    u_c = jnp.concatenate(uc_slabs, axis=0)
    if nb == 1:
        row = lax.broadcasted_iota(jnp.int32, (hp, d_model), 0)
        tail = jnp.zeros((hp, d_model), F32)
        for j in range(1, conv_w):
            last = jnp.broadcast_to(x_slabs[steps - j][hgt - 1:hgt], (hp, d_model))
            tail = jnp.where(row == hp - j, last, tail)
    else:
        tail = u[rows - hp:rows]
    cst_ref[...] = tail
    xh_ref[...] = tail

    n_blocks, blk_w, _ = wlru_ref.shape
    gate_proj = [_dot(u_c[:, n * blk_w:(n + 1) * blk_w].astype(BF16), wlru_ref[n])
                 for n in range(n_blocks)]
    q_heads()

    kp = _dot(latb, wk_ref[...])
    vt_ref[...] = _dot_nt(wvt_ref[...], latb).astype(BF16)
    gate_b = in_proj(xs, o_gb, o_gb + d_model)

    lam = lam_ref[...]
    softplus_neg = jnp.maximum(-lam, 0.0) + jnp.log1p(jnp.exp(-jnp.abs(lam)))
    a_parts, b_parts = [], []
    for n in range(n_blocks):
        sl = slice(n * blk_w, (n + 1) * blk_w)
        ucn = u_c[:, sl]
        g = gate_proj[n]
        r = _sigmoid(g[:, 0:blk_w] + brg_ref[:, sl])
        i = _sigmoid(g[:, blk_w:2 * blk_w] + big_ref[:, sl])
        log_a = (-RG_C * r) * softplus_neg[:, sl]
        a_n = jnp.exp(log_a)
        a_parts.append(a_n)
        b_parts.append(jnp.sqrt(-jnp.tanh(log_a) * (1.0 + a_n * a_n)) * (i * ucn))
    a = jnp.concatenate(a_parts, axis=1)
    b = jnp.concatenate(b_parts, axis=1)

    a_g = a[0:hgt]
    h_in = a_g * hprev_ref[...]
    if nb == 1:
        h_in = jnp.where(lax.broadcasted_iota(jnp.int32, h_in.shape, 0) == 0, h_in, 0.0)
    h = b[0:hgt] + h_in
    a_run = a_g
    h_slabs, a_slabs = [h], [a_run]
    for g in range(1, steps):
        a_g = a[g * hgt:(g + 1) * hgt]
        h = a_g * h + b[g * hgt:(g + 1) * hgt]
        h_slabs.append(h)
        if nb == 1:
            a_run = a_g * a_run
            a_slabs.append(a_run)
    if nb == 1:
        run_end = _linear_scan(a_run, h, 1)
        run_in = _shift_rows(run_end, 1, 0.0)
        h_slabs = [h_g + a_r * run_in for h_g, a_r in zip(h_slabs, a_slabs)]
        h_last = run_end[hgt - 1:hgt]
    else:
        h_last = h_slabs[-1]
    hprev_ref[...] = h_last
    hl_ref[...] = h_last
    rg = (_sigmoid(gate_b) * jnp.concatenate(h_slabs, axis=0)).astype(BF16)
    if nb == 1:
        rg = _dot(perm_ref[1], rg).astype(BF16)
    rg_ref[...] = rg

    k_heads(kp)
    ga_ref[...] = _sigmoid(gate_a).astype(BF16)


def _slab_permutation(rows):
    steps = rows // SUBLANES
    slab_row = jnp.arange(rows)
    time = (slab_row % SUBLANES) * steps + slab_row // SUBLANES
    gather = (time[:, None] == jnp.arange(rows)[None, :]).astype(BF16)
    return jnp.stack([gather, gather.T])


def _mixer_in(x, hist, h0, rope, wts, *, rows, nb):
    groups, tg, d_model = x.shape
    hp = hist.shape[1]
    q_lora = wts["wq"].shape[0]
    kv_lora = wts["wk"].shape[0]
    conv_w = wts["cw"].shape[0]
    n_t = tg // rows
    qk_w = N_HEADS * HEAD_PAD
    v_w = N_HEADS * V_DIM

    def row_spec(width):
        return pl.BlockSpec((None, rows, width), lambda g, t: (g, t, 0))

    def group_spec(r, width):
        return pl.BlockSpec((None, r, width), lambda g, t: (g, 0, 0))

    tab_spec = pl.BlockSpec((rows, HEAD_PAD), lambda g, t: (t, 0))
    w_names = ["g_mix", "w_in", "g_qa", "wq", "gq", "g_kva", "g_kr", "wk", "gk", "wvt",
               "cw", "cb", "wlru", "brg", "big", "lam"]
    w_list = [wts[n] for n in w_names] + ([_slab_permutation(rows)] if nb == 1 else [])
    in_specs = ([row_spec(d_model), group_spec(hp, d_model), group_spec(nb, d_model),
                 tab_spec, tab_spec, tab_spec] + [_const_spec(w.shape) for w in w_list])
    out_shape = (
        jax.ShapeDtypeStruct((groups, tg, qk_w), BF16),
        jax.ShapeDtypeStruct((groups, tg, qk_w), BF16),
        jax.ShapeDtypeStruct((groups, n_t, v_w, rows), BF16),
        jax.ShapeDtypeStruct((groups, tg, kv_lora), F32),
        jax.ShapeDtypeStruct((groups, tg, ROPE_DIM), F32),
        jax.ShapeDtypeStruct((groups, tg, d_model), BF16),
        jax.ShapeDtypeStruct((groups, tg, d_model), BF16),
        jax.ShapeDtypeStruct((groups, nb, d_model), F32),
        jax.ShapeDtypeStruct((groups, hp, d_model), F32),
    )
    vt_spec = pl.BlockSpec((None, None, v_w, rows), lambda g, t: (g, t, 0, 0))
    out_specs = (row_spec(qk_w), row_spec(qk_w), vt_spec, row_spec(kv_lora), row_spec(ROPE_DIM),
                 row_spec(d_model), row_spec(d_model), group_spec(nb, d_model), group_spec(hp, d_model))
    w_bytes = sum(w.size * w.dtype.itemsize for w in w_list)
    io_bytes = 2 * rows * (4 * d_model + 2 * (2 * qk_w + v_w + 2 * d_model) + 4 * (kv_lora + LANES) + 12 * LANES)
    tmp_bytes = 4 * rows * (wts["w_in"].shape[1] + 2 * qk_w + 6 * d_model) + 4 * (hp + rows) * d_model
    kern = functools.partial(_mixer_in_kernel, rows=rows, nb=nb, hp=hp, d_model=d_model,
                             q_lora=q_lora, kv_lora=kv_lora, conv_w=conv_w)
    return pl.pallas_call(
        kern,
        grid=(groups, n_t),
        in_specs=in_specs,
        out_specs=out_specs,
        out_shape=out_shape,
        scratch_shapes=[pltpu.VMEM((hp, d_model), F32), pltpu.VMEM((nb, d_model), F32)],
        compiler_params=pltpu.CompilerParams(
            dimension_semantics=("parallel", "arbitrary"),
            vmem_limit_bytes=_vmem_limit(w_bytes + io_bytes + tmp_bytes)),
        name="mixer_in",
    )(x, hist, h0, *rope, *w_list)


def _attn_kernel(q_ref, k_ref, vt_ref, ga_ref, rg_ref, o_ref, m_ref, acc_ref, carry_ref, *, tile, cq, vblk):
    qi = pl.program_id(2)
    m_ref[...] = jnp.full(m_ref.shape, -jnp.inf, F32)
    acc_ref[...] = jnp.zeros(acc_ref.shape, F32)
    ones_rows = jnp.ones((acc_ref.shape[1] - V_DIM, vblk), BF16)

    wide = [(hh, 0, tile) for hh in range(2)]
    narrow = [(hh, q0, cq) for hh in range(2) for q0 in range(0, tile, cq)]

    def scores(kj, hh, q0, qw, n_keys):
        lanes = slice(hh * HEAD_PAD, (hh + 1) * HEAD_PAD)
        k0 = pl.multiple_of(kj * tile, tile)
        return _dot_nt(k_ref[pl.ds(k0, n_keys), lanes], q_ref[q0:q0 + qw, lanes])

    def diag_mask(s, q0):
        k_chunk = lax.broadcasted_iota(jnp.int32, s.shape, 0) >> CHUNK_SHIFT
        q_chunk = (q0 + lax.broadcasted_iota(jnp.int32, s.shape, 1)) >> CHUNK_SHIFT
        return jnp.where(k_chunk <= q_chunk, s, -jnp.inf)

    def softmax(hh, q0, qw, s):
        qs = slice(q0, q0 + qw)
        m_prev = m_ref[hh, :, qs]
        m_new = jnp.maximum(m_prev, jnp.max(s, axis=0, keepdims=True))
        m_ref[hh, :, qs] = m_new
        return jnp.exp2(s - m_new).astype(BF16), jnp.exp2(m_prev - m_new)

    def values(kj, hh, q0, qw, pb, alpha):
        qs = slice(q0, q0 + qw)
        pv = None
        for j in range(pb.shape[0] // vblk):
            vt = vt_ref[kj * (tile // vblk) + j, hh * V_DIM:(hh + 1) * V_DIM, :]
            part = _dot(jnp.concatenate([vt, ones_rows], axis=0), pb[j * vblk:(j + 1) * vblk])
            pv = part if pv is None else pv + part
        acc_ref[hh, :, qs] = alpha * acc_ref[hh, :, qs] + pv

    def run_tiles(tiles, last_on_diagonal):
        items = []
        for t, kj in enumerate(tiles):
            diag = last_on_diagonal and t == len(tiles) - 1
            items += [(kj, hh, q0, qw, (q0 + qw) if diag else tile, diag)
                      for hh, q0, qw in (narrow if diag else wide)]

        first = items[0]
        s_next = carry_ref[0:first[4], first[2]:first[2] + first[3]]
        prev = None
        for i, (kj, hh, q0, qw, n_keys, diag) in enumerate(items):
            s_cur = s_next
            if i + 1 < len(items):
                s_next = scores(*items[i + 1][:5])
            elif not last_on_diagonal:
                carry_ref[...] = scores(kj + 1, *wide[0], tile)
            if diag:
                s_cur = diag_mask(s_cur, q0)
            cur = softmax(hh, q0, qw, s_cur)
            if prev is not None:
                values(*items[i - 1][:4], *prev)
            prev = cur
        values(*items[-1][:4], *prev)

    carry_ref[...] = scores(0, *wide[0], tile)

    @pl.loop(0, qi >> 1)
    def _(j):
        run_tiles([2 * j, 2 * j + 1], False)

    @pl.when((qi & 1) == 1)
    def _():
        run_tiles([qi - 1], False)

    run_tiles([qi], True)

    attn_t = jnp.concatenate([acc_ref[hh, 0:V_DIM, :] / acc_ref[hh, V_DIM:V_DIM + 1, :] for hh in range(2)],
                             axis=0)
    o_ref[...] = (ga_ref[...].astype(F32) * attn_t.T + rg_ref[...].astype(F32)).astype(BF16)


def _attention(q, k, vt, ga, rg, *, tile):
    bsz, t, _ = q.shape
    n_vb, _, vblk = vt.shape[1:]
    assert LANES == 2 * V_DIM and N_HEADS % 2 == 0
    assert k.shape[1] == t == n_vb * vblk and tile % vblk == 0 and t % tile == 0
    assert tile % CHUNK == 0 and tile % MXU_DIM == 0
    n_hp = N_HEADS // 2
    qk_blk = 2 * HEAD_PAD

    def q_index(b, p, qi):
        return (b, qi, p)

    kern = functools.partial(_attn_kernel, tile=tile, cq=MXU_DIM, vblk=vblk)
    blk_bytes = 2 * 2 * (tile * qk_blk + t * qk_blk + t * LANES + 3 * tile * LANES)
    tmp_bytes = 4 * 8 * tile * MXU_DIM + 4 * 4 * tile * LANES
    return pl.pallas_call(
        kern,
        grid=(bsz, n_hp, t // tile),
        in_specs=[pl.BlockSpec((None, tile, qk_blk), q_index),
                  pl.BlockSpec((None, t, qk_blk), lambda b, p, qi: (b, 0, p)),
                  pl.BlockSpec((None, n_vb, LANES, vblk), lambda b, p, qi: (b, 0, p, 0)),
                  pl.BlockSpec((None, tile, LANES), q_index),
                  pl.BlockSpec((None, tile, LANES), q_index)],
        out_specs=pl.BlockSpec((None, tile, LANES), q_index),
        out_shape=jax.ShapeDtypeStruct((bsz, t, N_HEADS * V_DIM), BF16),
        scratch_shapes=[pltpu.VMEM((2, 1, tile), F32),
                        pltpu.VMEM((2, V_DIM + BF16_SUBLANES, tile), F32),
                        pltpu.VMEM((tile, tile), F32)],
        compiler_params=pltpu.CompilerParams(
            dimension_semantics=("parallel", "parallel", "arbitrary"),
            vmem_limit_bytes=_vmem_limit(blk_bytes + tmp_bytes)),
        name="attention",
    )(q, k, vt, ga, rg)


def _cache_attn_kernel(q_ref, latc_ref, krc_ref, latn_ref, krn_ref, ga_ref, rg_ref,
                       wkg_ref, wkc_ref, seg_ref, expand_ref, wv_ref, o_ref,
                       qabs_ref, qr_ref, m_ref, acc_ref, *, t, tk, n_new):
    past = latc_ref.shape[0]
    hq = N_HEADS * t
    for h in range(N_HEADS):
        qh = q_ref[:, h * HEAD_PAD:(h + 1) * HEAD_PAD]
        qabs_ref[h * t:(h + 1) * t, :] = _dot(qh, wkg_ref[h]).astype(BF16)
        qr_ref[h * t:(h + 1) * t, :] = qh[:, NOPE_DIM:QK_DIM]
    m_ref[...] = jnp.full(m_ref.shape, -jnp.inf, F32)
    acc_ref[...] = jnp.zeros(acc_ref.shape, F32)
    kv_lora = latc_ref.shape[1]

    def scores(lat, kr, n_valid):
        n = lat.shape[0]
        latb = lat.astype(BF16)
        kvk = _dot(latb, wkc_ref[...])
        ssum = _dot((kvk * kvk).astype(BF16), seg_ref[...])
        r = lax.rsqrt(ssum * (1.0 / NOPE_DIM) + EPS)
        r_hi = r.astype(BF16)
        r_lo = (r - r_hi.astype(F32)).astype(BF16)
        r_cols = _dot(jnp.concatenate([r_hi, r_lo], axis=1), expand_ref[...])
        s = _dot_nt(latb, qabs_ref[...]) * r_cols + _dot_nt(kr.astype(BF16), qr_ref[...])
        if n_valid < n:
            s = jnp.where(lax.broadcasted_iota(jnp.int32, (n, hq), 0) < n_valid, s, -jnp.inf)
        lat_t = jnp.concatenate([lat.T.astype(BF16), jnp.ones((acc_ref.shape[0] - kv_lora, n), BF16)], axis=0)
        return s, lat_t

    def accumulate(s, lat_t):
        m_prev = m_ref[...]
        m_new = jnp.maximum(m_prev, jnp.max(s, axis=0, keepdims=True))
        m_ref[...] = m_new
        p = jnp.exp2(s - m_new).astype(BF16)
        acc_ref[...] = jnp.exp2(m_prev - m_new) * acc_ref[...] + _dot(lat_t, p)

    def tile_scores(j):
        if j < past // tk:
            return scores(latc_ref[j * tk:(j + 1) * tk, :], krc_ref[j * tk:(j + 1) * tk, :], tk)
        return scores(latn_ref[...], krn_ref[...], n_new)

    n_tiles = past // tk + 1
    cur = tile_scores(0)
    for j in range(n_tiles):
        nxt = tile_scores(j + 1) if j + 1 < n_tiles else None
        accumulate(*cur)
        cur = nxt

    ctx = (acc_ref[0:kv_lora, :] / acc_ref[kv_lora:kv_lora + 1, :]).T.astype(BF16)
    first_head = lax.broadcasted_iota(jnp.int32, (t, LANES), 1) < V_DIM
    for pair in range(N_HEADS // 2):
        cols = slice(pair * LANES, (pair + 1) * LANES)
        wv = wv_ref[:, cols]
        a0 = _dot(ctx[(2 * pair) * t:(2 * pair + 1) * t], wv)
        a1 = _dot(ctx[(2 * pair + 1) * t:(2 * pair + 2) * t], wv)
        attn = jnp.where(first_head, a0, a1)
        o_ref[:, cols] = (ga_ref[:, cols].astype(F32) * attn + rg_ref[:, cols].astype(F32)).astype(BF16)


def _cache_attention(q, lat_cache, kr_cache, lat_new, kr_new, ga, rg, wts, *, tk, n_new):
    bsz, t, qk_w = q.shape
    past, kv_lora = lat_cache.shape[1:]
    n_pad = lat_new.shape[1]
    d_model = ga.shape[2]
    hq = N_HEADS * t
    assert past % tk == 0 and LANES == 2 * V_DIM
    w_list = [wts["wkg"], wts["wkc"], wts["seg"], wts["expand"], wts["wv"]]
    w_bytes = sum(w.size * w.dtype.itemsize for w in w_list)
    blk_bytes = 2 * (4 * (past + n_pad) * (kv_lora + LANES) + 2 * t * (qk_w + 3 * d_model))
    tmp_bytes = 4 * tk * (2 * N_HEADS * NOPE_DIM + 4 * hq)

    def batch_spec(r, width):
        return pl.BlockSpec((None, r, width), lambda b: (b, 0, 0))

    kern = functools.partial(_cache_attn_kernel, t=t, tk=tk, n_new=n_new)
    return pl.pallas_call(
        kern,
        grid=(bsz,),
        in_specs=[batch_spec(t, qk_w), batch_spec(past, kv_lora), batch_spec(past, ROPE_DIM),
                  batch_spec(n_pad, kv_lora), batch_spec(n_pad, ROPE_DIM),
                  batch_spec(t, d_model), batch_spec(t, d_model)] + [_const_spec(w.shape) for w in w_list],
        out_specs=batch_spec(t, d_model),
        out_shape=jax.ShapeDtypeStruct((bsz, t, d_model), BF16),
        scratch_shapes=[pltpu.VMEM((hq, kv_lora), BF16), pltpu.VMEM((hq, ROPE_DIM), BF16),
                        pltpu.VMEM((1, hq), F32), pltpu.VMEM((kv_lora + BF16_SUBLANES, hq), F32)],
        compiler_params=pltpu.CompilerParams(
            dimension_semantics=("parallel",),
            vmem_limit_bytes=_vmem_limit(w_bytes + blk_bytes + tmp_bytes)),
        name="cache_attention",
    )(q, lat_cache, kr_cache, lat_new, kr_new, ga, rg, *w_list)


def _mixer_out_kernel(x_ref, mix_ref, hist_ref, w_out_ref, g_ffn_ref, w_up_ref, fw_ref, fb_ref, w_down_ref,
                      y_ref, fst_ref, upc_ref, *, rows, nb, hp, d_model, d_ff, conv_w, col_blk):
    t = pl.program_id(1)

    @pl.when(t == 0)
    def _():
        upc_ref[0:hp, :] = hist_ref[...]

    x1 = x_ref[...] + _dot(mix_ref[...], w_out_ref[...])
    xn = (x1 * _rms_rows(x1, d_model) * g_ffn_ref[...]).astype(BF16)

    def up_proj(c):
        for c0 in (c * col_blk, d_ff + c * col_blk):
            sl = slice(c0, c0 + col_blk)
            upc_ref[hp:hp + rows, sl] = _dot(xn, w_up_ref[:, sl])

    def conv_cols(c0):
        sl = slice(c0, c0 + col_blk)
        out = fb_ref[:, sl] + fw_ref[conv_w - 1:conv_w, sl] * upc_ref[hp:hp + rows, sl]
        for j in range(1, conv_w):
            out = out + fw_ref[conv_w - 1 - j:conv_w - j, sl] * upc_ref[hp - j * nb:hp - j * nb + rows, sl]
        return out

    def gated(c):
        gate = conv_cols(c * col_blk)
        val = conv_cols(d_ff + c * col_blk)
        return (gate * _sigmoid(gate) * val).astype(BF16)

    def down_proj(c, hmid):
        return _dot(hmid, w_down_ref[c * col_blk:(c + 1) * col_blk, :])

    n_chunks = d_ff // col_blk
    y = x1
    up_proj(0)
    prev = None
    for c in range(n_chunks):
        if c + 1 < n_chunks:
            up_proj(c + 1)
        hmid = gated(c)
        if prev is not None:
            y = y + down_proj(c - 1, prev)
        prev = hmid
    y_ref[...] = y + down_proj(n_chunks - 1, prev)

    tail = upc_ref[rows:rows + hp, :]
    fst_ref[...] = tail
    upc_ref[0:hp, :] = tail


def _mixer_out(x, mixed, hist, wts, *, rows, nb):
    groups, tg, d_model = x.shape
    hp = hist.shape[1]
    d_ff = wts["w_down"].shape[0]
    conv_w = wts["fw"].shape[0]
    n_t = tg // rows
    col_blk = MXU_DIM

    def row_spec(width):
        return pl.BlockSpec((None, rows, width), lambda g, t: (g, t, 0))

    def group_spec(r, width):
        return pl.BlockSpec((None, r, width), lambda g, t: (g, 0, 0))

    w_names = ["w_out", "g_ffn", "w_up", "fw", "fb", "w_down"]
    w_list = [wts[n] for n in w_names]
    w_bytes = sum(w.size * w.dtype.itemsize for w in w_list)
    blk_bytes = 2 * rows * d_model * (4 + 2 + 4) + 4 * 4 * hp * 2 * d_ff
    tmp_bytes = 4 * (hp + rows) * 2 * d_ff + 4 * rows * (3 * d_model + 8 * col_blk)
    kern = functools.partial(_mixer_out_kernel, rows=rows, nb=nb, hp=hp, d_model=d_model, d_ff=d_ff,
                             conv_w=conv_w, col_blk=col_blk)
    return pl.pallas_call(
        kern,
        grid=(groups, n_t),
        in_specs=[row_spec(d_model), row_spec(d_model), group_spec(hp, 2 * d_ff)]
                 + [_const_spec(w.shape) for w in w_list],
        out_specs=(row_spec(d_model), group_spec(hp, 2 * d_ff)),
        out_shape=(jax.ShapeDtypeStruct((groups, tg, d_model), F32),
                   jax.ShapeDtypeStruct((groups, hp, 2 * d_ff), F32)),
        scratch_shapes=[pltpu.VMEM((hp + rows, 2 * d_ff), F32)],
        compiler_params=pltpu.CompilerParams(
            dimension_semantics=("parallel", "arbitrary"),
            vmem_limit_bytes=_vmem_limit(w_bytes + blk_bytes + tmp_bytes)),
        name="mixer_out",
    )(x, mixed, hist, *w_list)


def _head_pad(w, widths):
    lead = w.shape[:-1]
    per_head = sum(widths)
    w = w.reshape(lead + (N_HEADS, per_head))
    w = jnp.pad(w, [(0, 0)] * len(lead) + [(0, 0), (0, HEAD_PAD - per_head)])
    return w.reshape(lead + (N_HEADS * HEAD_PAD,))


def _prep_weights(l, g_mix_norm, w_in, g_q_a, w_q_b, g_kv_a, w_kv_b, g_qn, g_qr, g_kn, g_kr,
                  lru_conv_w, lru_conv_b, w_rg, b_rg, w_ig, b_ig, lru_lambda, w_out, g_ffn_norm,
                  w_up, ffn_conv_w, ffn_conv_b, w_down):
    d_model = w_in.shape[1]
    q_lora = w_q_b.shape[1]
    kv_lora = w_kv_b.shape[1]
    row = lambda a: a.reshape(1, -1)
    o_kr = q_lora + kv_lora
    wi = w_in[l].astype(BF16)
    w_kr = jnp.pad(wi[:, o_kr:o_kr + ROPE_DIM], ((0, 0), (NOPE_DIM, HEAD_PAD - QK_DIM)))
    w_in_p = jnp.concatenate([wi[:, :o_kr], w_kr, wi[:, o_kr + ROPE_DIM:]], axis=1)
    scale = QK_DIM ** -0.5 * LOG2_E
    gq = jnp.tile(jnp.pad(jnp.concatenate([g_qn[l], g_qr[l]]) * scale, (0, HEAD_PAD - QK_DIM)), N_HEADS)
    gk = jnp.tile(jnp.pad(g_kn[l], (0, HEAD_PAD - NOPE_DIM)), N_HEADS)
    kv = w_kv_b[l].reshape(kv_lora, N_HEADS, NOPE_DIM + V_DIM)
    wk = _head_pad(kv[:, :, :NOPE_DIM].reshape(kv_lora, N_HEADS * NOPE_DIM), (NOPE_DIM,))
    wv = kv[:, :, NOPE_DIM:].reshape(kv_lora, N_HEADS * V_DIM)
    wkg = jnp.pad(jnp.transpose(kv[:, :, :NOPE_DIM], (1, 2, 0)) * g_kn[l][None, :, None],
                  ((0, 0), (0, HEAD_PAD - NOPE_DIM), (0, 0)))
    seg = jnp.pad(jnp.repeat(jnp.eye(N_HEADS, dtype=F32), NOPE_DIM, axis=0), ((0, 0), (0, LANES - N_HEADS)))
    return {
        "wkg": wkg.astype(BF16), "wkc": kv[:, :, :NOPE_DIM].reshape(kv_lora, -1).astype(BF16),
        "seg": seg.astype(BF16), "wv": wv.astype(BF16),
        "g_mix": row(g_mix_norm[l]), "w_in": w_in_p, "g_qa": row(g_q_a[l]),
        "wq": _head_pad(w_q_b[l], (NOPE_DIM, ROPE_DIM)).astype(BF16), "gq": row(gq),
        "g_kva": row(g_kv_a[l]), "g_kr": row(jnp.pad(g_kr[l], (NOPE_DIM, HEAD_PAD - QK_DIM))),
        "wk": wk.astype(BF16), "gk": row(gk), "wvt": wv.T.astype(BF16),
        "cw": lru_conv_w[l], "cb": row(lru_conv_b[l]),
        "wlru": jnp.concatenate([w_rg[l], w_ig[l]], axis=-1).astype(BF16),
        "brg": row(b_rg[l]), "big": row(b_ig[l]), "lam": row(lru_lambda[l]),
        "w_out": w_out[l].astype(BF16), "g_ffn": row(g_ffn_norm[l]), "w_up": w_up[l].astype(BF16),
        "fw": ffn_conv_w[l], "fb": row(ffn_conv_b[l]), "w_down": w_down[l].astype(BF16),
    }


def _rope_tables(pos):
    inv = np.float32(ROPE_THETA) ** (-np.arange(0, ROPE_DIM, 2, dtype=np.float32) / np.float32(ROPE_DIM))
    ang = pos.astype(np.float32)[:, None] * inv[None, :]
    cos = np.cos(ang.astype(np.float64)).astype(np.float32)
    sin = np.sin(ang.astype(np.float64)).astype(np.float32)
    n = pos.shape[0]
    ones_lo = np.ones((n, NOPE_DIM), np.float32)
    zeros_lo = np.zeros((n, NOPE_DIM), np.float32)
    zeros_half = np.zeros((n, HALF_ROPE), np.float32)
    tail = np.zeros((n, HEAD_PAD - QK_DIM), np.float32)
    c = np.concatenate([ones_lo, cos, cos, tail], axis=1)
    s_lo = np.concatenate([zeros_lo, -sin, zeros_half, tail], axis=1)
    s_hi = np.concatenate([zeros_lo, zeros_half, sin, tail], axis=1)
    return jnp.asarray(c), jnp.asarray(s_lo), jnp.asarray(s_hi)


def _expand_matrix(t):
    one_part = jnp.pad(jnp.repeat(jnp.eye(N_HEADS, dtype=F32), t, axis=1), ((0, LANES - N_HEADS), (0, 0)))
    return jnp.concatenate([one_part, one_part], axis=0).astype(BF16)


def _front_pad_rows(a, hp):
    return jnp.pad(a, ((0, 0), (hp - a.shape[1], 0), (0, 0)))


def _layer_prompt(x, wts, *, rows, out_rows, attn_tile):
    bsz, t, d_model = x.shape
    lru_w = wts["cw"].shape[0]
    ffn_w = wts["fw"].shape[0]
    d_ff2 = wts["w_up"].shape[1]
    hp1 = _round_up(lru_w - 1, SUBLANES)
    hp2 = _round_up(ffn_w - 1, SUBLANES)
    rope = _rope_tables(np.arange(t, dtype=np.int32))
    q, k, vt, lat, kr, ga, rg, h_last, cst = _mixer_in(
        x, jnp.zeros((bsz, hp1, d_model), F32), jnp.zeros((bsz, 1, d_model), F32), rope, wts, rows=rows, nb=1)
    mixed = _attention(q, k, vt, ga, rg, tile=attn_tile)
    y, fst = _mixer_out(x, mixed, jnp.zeros((bsz, hp2, d_ff2), F32), wts, rows=out_rows, nb=1)
    return y, (lat, kr, h_last[:, 0], cst[:, hp1 - (lru_w - 1):], fst[:, hp2 - (ffn_w - 1):])


def _layer_sample(x, past_lat, past_kr, h0, lru_buf, ffn_buf, wts, *, tk):
    bsz, t, d_model = x.shape
    past = past_lat.shape[1]
    lru_w = wts["cw"].shape[0]
    ffn_w = wts["fw"].shape[0]
    hp1 = _round_up((lru_w - 1) * bsz, SUBLANES)
    hp2 = _round_up((ffn_w - 1) * bsz, SUBLANES)
    rows = t * bsz

    def to_tm(a):
        return jnp.swapaxes(a, 0, 1).reshape(1, a.shape[1] * bsz, a.shape[2])

    def from_tm(a):
        return jnp.swapaxes(a.reshape(a.shape[1] // bsz, bsz, a.shape[2]), 0, 1)

    rope = _rope_tables(np.repeat(past + np.arange(t, dtype=np.int32), bsz))
    q, k, vt, lat, kr, ga, rg, h_last, cst = _mixer_in(
        to_tm(x), _front_pad_rows(to_tm(lru_buf), hp1), h0[None], rope, wts, rows=rows, nb=bsz)

    assert (past % CHUNK) + t <= CHUNK, "cache attention assumes all keys visible to all queries"
    n_pad = _round_up(t, LANES)
    pad_rows = lambda a: jnp.pad(from_tm(a), ((0, 0), (0, n_pad - t), (0, 0)))
    cache_wts = dict(wts, expand=_expand_matrix(t))
    mixed = _cache_attention(from_tm(q), past_lat, past_kr, pad_rows(lat), pad_rows(kr),
                             from_tm(ga), from_tm(rg), cache_wts, tk=tk, n_new=t)

    y, fst = _mixer_out(to_tm(x), to_tm(mixed), _front_pad_rows(to_tm(ffn_buf), hp2), wts, rows=rows, nb=bsz)
    states = (from_tm(lat), from_tm(kr), h_last[0],
              from_tm(cst[:, hp1 - (lru_w - 1) * bsz:]), from_tm(fst[:, hp2 - (ffn_w - 1) * bsz:]))
    return from_tm(y), states


def kernel(x_prompt, x_sample, cache_kv_latent, cache_k_rope, state_lru_h, state_lru_conv, state_ffn_conv,
           g_mix_norm, w_in, g_q_a, w_q_b, g_kv_a, w_kv_b, g_qn, g_qr, g_kn, g_kr, lru_conv_w, lru_conv_b,
           w_rg, b_rg, w_ig, b_ig, lru_lambda, w_out, g_ffn_norm, w_up, ffn_conv_w, ffn_conv_b, w_down):
    depth = w_in.shape[0]
    yp, ys = x_prompt, x_sample
    p_states, s_states = [], []
    for l in range(depth):
        wts = _prep_weights(l, g_mix_norm, w_in, g_q_a, w_q_b, g_kv_a, w_kv_b, g_qn, g_qr, g_kn, g_kr,
                            lru_conv_w, lru_conv_b, w_rg, b_rg, w_ig, b_ig, lru_lambda, w_out, g_ffn_norm,
                            w_up, ffn_conv_w, ffn_conv_b, w_down)
        yp, st_p = _layer_prompt(yp, wts, rows=256, out_rows=512, attn_tile=512)
        ys, st_s = _layer_sample(ys, cache_kv_latent[l], cache_k_rope[l], state_lru_h[l],
                                 state_lru_conv[l], state_ffn_conv[l], wts, tk=512)
        p_states.append(st_p)
        s_states.append(st_s)
    p_out = [jnp.stack([st[j] for st in p_states], axis=0) for j in range(5)]
    s_out = [jnp.stack([st[j] for st in s_states], axis=0) for j in range(5)]
    return (yp, ys, *p_out, *s_out)
```

```python
import functools

import jax
import jax.numpy as jnp
import numpy as np
from jax import lax
from jax.experimental import pallas as pl
from jax.experimental.pallas import tpu as pltpu

CHUNK = 64
CHUNK_SHIFT = CHUNK.bit_length() - 1
assert CHUNK == 1 << CHUNK_SHIFT
N_HEADS = 16
NOPE_DIM = 64
ROPE_DIM = 32
V_DIM = 64
QK_DIM = NOPE_DIM + ROPE_DIM
ROPE_THETA = 10000.0
RG_C = 8.0
EPS = 1e-6
LOG2_E = 1.4426950408889634

LANES = 128
SUBLANES = 8
BF16_SUBLANES = 16
MXU_DIM = 256
VMEM_BYTES_V7X = 64 * 1024 * 1024

HEAD_PAD = LANES
HALF_ROPE = ROPE_DIM // 2
F32 = jnp.float32
BF16 = jnp.bfloat16


def _round_up(n, m):
    return (n + m - 1) // m * m


def _vmem_limit(nbytes):
    return int(min(2 * nbytes, VMEM_BYTES_V7X - 8 * 1024 * 1024))


def _const_spec(shape):
    nd = len(shape)
    return pl.BlockSpec(shape, lambda *_: (0,) * nd, pipeline_mode=pl.Buffered(1))


def _dot(a, b):
    return jnp.dot(a, b, preferred_element_type=F32)


def _dot_nt(a, b):
    return lax.dot_general(a, b, (((1,), (1,)), ((), ())), preferred_element_type=F32)


def _sigmoid(x):
    return 0.5 * jnp.tanh(0.5 * x) + 0.5


def _rms_rows(x, n):
    return lax.rsqrt(jnp.sum(x * x, axis=-1, keepdims=True) * (1.0 / n) + EPS)


def _rope_head(x, c, s_lo, s_hi):
    return (x * c + pltpu.roll(x, HALF_ROPE, 1) * s_hi
            + pltpu.roll(x, HEAD_PAD - HALF_ROPE, 1) * s_lo)


def _shift_rows(x, n, fill):
    rows, cols = x.shape
    if n % SUBLANES == 0:
        return jnp.concatenate([jnp.full((n, cols), fill, x.dtype), x[:rows - n]], axis=0)
    rolled = pltpu.roll(x, n, 0)
    row = lax.broadcasted_iota(jnp.int32, x.shape, 0)
    return jnp.where(row >= n, rolled, fill)


def _linear_scan(a, b, nb):
    rows = a.shape[0]
    s = nb
    while s < rows:
        b = a * _shift_rows(b, s, 0.0) + b
        if 2 * s < rows:
            a = a * _shift_rows(a, s, 1.0)
        s *= 2
    return b


def _mixer_in_kernel(x_ref, hist_ref, h0_ref, cos_ref, slo_ref, shi_ref,
                     g_mix_ref, w_in_ref, g_qa_ref, wq_ref, wqx_ref, gq_ref, gqx_ref, segq_ref, expq_ref,
                     g_kva_ref, g_kr_ref,
                     wk_ref, gk_ref, wvt_ref, cw_ref, cb_ref, wlru_ref, brg_ref, big_ref, lam_ref, *rest,
                     rows, nb, hp, d_model, q_lora, kv_lora, conv_w):
    if nb == 1:
        perm_ref, *rest = rest
    q_ref, k_ref, vt_ref, lat_ref, kr_ref, ga_ref, rg_ref, hl_ref, cst_ref, xh_ref, hprev_ref = rest
    t = pl.program_id(1)

    @pl.when(t == 0)
    def _():
        xh_ref[...] = hist_ref[...]
        hprev_ref[...] = h0_ref[...]

    o_kv = q_lora
    o_kr = o_kv + kv_lora
    o_u = o_kr + HEAD_PAD
    o_ga = o_u + d_model
    o_gb = o_ga + d_model
    cos = cos_ref[...]
    s_lo = slo_ref[...]
    s_hi = shi_ref[...]

    x = x_ref[...]
    xn = (x * _rms_rows(x, d_model) * g_mix_ref[...]).astype(BF16)

    xs = _dot(perm_ref[0], xn).astype(BF16) if nb == 1 else xn

    def in_proj(lhs, c0, c1):
        return _dot(lhs, w_in_ref[:, c0:c1])

    z_lat = in_proj(xn, 0, o_u)
    u = in_proj(xs, o_u, o_ga)

    cq = z_lat[:, 0:o_kv]
    ckv = z_lat[:, o_kv:o_kr]
    krb = z_lat[:, o_kr:o_u]
    lat = ckv * _rms_rows(ckv, kv_lora) * g_kva_ref[...]
    lat_ref[...] = lat
    latb = lat.astype(BF16)
    kr = _rope_head(krb * _rms_rows(krb, ROPE_DIM) * g_kr_ref[...], cos, s_lo, s_hi)
    kr_ref[...] = kr[:, NOPE_DIM:NOPE_DIM + ROPE_DIM]
    cqn = (cq * _rms_rows(cq, q_lora) * g_qa_ref[...]).astype(BF16)

    qp = _dot(cqn, wq_ref[...])
    qp2 = _dot(cqn, wqx_ref[...])

    def q_mean_squares():
        return _dot((qp * qp).astype(BF16), segq_ref[...])

    def q_norm_factors(ms):
        r = lax.rsqrt(ms + EPS)
        r_hi = r.astype(BF16)
        r_lo = (r - r_hi.astype(F32)).astype(BF16)
        return _dot(jnp.concatenate([r_hi, r_lo], axis=1), expq_ref[...])

    def q_heads(rr):
        cg = cos * gq_ref[...]
        sg = (s_lo + s_hi) * gqx_ref[...]
        for h in range(N_HEADS):
            sl = slice(h * HEAD_PAD, (h + 1) * HEAD_PAD)
            q_ref[:, sl] = (rr[:, sl] * (qp[:, sl] * cg + qp2[:, sl] * sg)).astype(BF16)

    def k_heads(kp):
        for h in range(N_HEADS):
            sl = slice(h * HEAD_PAD, (h + 1) * HEAD_PAD)
            kh = kp[:, sl]
            k_ref[:, sl] = (kh * _rms_rows(kh, NOPE_DIM) * gk_ref[:, sl] + kr).astype(BF16)

    hgt = SUBLANES if nb == 1 else nb
    steps = rows // hgt
    x_slabs = {g: u[g * hgt:(g + 1) * hgt] for g in range(steps)}
    first_run = lax.broadcasted_iota(jnp.int32, (hgt, d_model), 0) == 0
    for j in range(1, conv_w):
        if nb == 1:
            x_slabs[-j] = jnp.where(first_run, xh_ref[hp - j:hp - j + 1, :], pltpu.roll(x_slabs[steps - j], 1, 0))
        else:
            x_slabs[-j] = xh_ref[hp - j * nb:hp - (j - 1) * nb, :]
    uc_slabs = []
    for g in range(steps):
        acc = cb_ref[...] + cw_ref[conv_w - 1:conv_w, :] * x_slabs[g]
        for j in range(1, conv_w):
            acc = acc + cw_ref[conv_w - 1 - j:conv_w - j, :] * x_slabs[g - j]
        uc_slabs.append(acc)
    u_c = jnp.concatenate(uc_slabs, axis=0)
    if nb == 1:
        row = lax.broadcasted_iota(jnp.int32, (hp, d_model), 0)
        tail = jnp.zeros((hp, d_model), F32)
        for j in range(1, conv_w):
            last = jnp.broadcast_to(x_slabs[steps - j][hgt - 1:hgt], (hp, d_model))
            tail = jnp.where(row == hp - j, last, tail)
    else:
        tail = u[rows - hp:rows]
    cst_ref[...] = tail
    xh_ref[...] = tail

    n_blocks, blk_w, _ = wlru_ref.shape
    q_ms = q_mean_squares()
    gate_proj = [_dot(u_c[:, n * blk_w:(n + 1) * blk_w].astype(BF16), wlru_ref[n])
                 for n in range(n_blocks)]
    q_rr = q_norm_factors(q_ms)

    kp = _dot(latb, wk_ref[...])
    vt_ref[...] = _dot_nt(wvt_ref[...], latb).astype(BF16)
    gate_a = in_proj(xn, o_ga, o_gb)
    gate_b = in_proj(xs, o_gb, o_gb + d_model)

    lam = lam_ref[...]
    softplus_neg = jnp.maximum(-lam, 0.0) + jnp.log1p(jnp.exp(-jnp.abs(lam)))
    a_parts, b_parts = [], []
    for n in range(n_blocks):
        sl = slice(n * blk_w, (n + 1) * blk_w)
        ucn = u_c[:, sl]
        g = gate_proj[n]
        r = _sigmoid(g[:, 0:blk_w] + brg_ref[:, sl])
        i = _sigmoid(g[:, blk_w:2 * blk_w] + big_ref[:, sl])
        log_a = (-RG_C * r) * softplus_neg[:, sl]
        a_n = jnp.exp(log_a)
        a_parts.append(a_n)
        b_parts.append(jnp.sqrt(-jnp.tanh(log_a) * (1.0 + a_n * a_n)) * (i * ucn))
    a = jnp.concatenate(a_parts, axis=1)
    b = jnp.concatenate(b_parts, axis=1)

    a_g = a[0:hgt]
    h_in = a_g * hprev_ref[...]
    if nb == 1:
        h_in = jnp.where(lax.broadcasted_iota(jnp.int32, h_in.shape, 0) == 0, h_in, 0.0)
    h = b[0:hgt] + h_in
    a_run = a_g
    h_slabs, a_slabs = [h], [a_run]
    for g in range(1, steps):
        a_g = a[g * hgt:(g + 1) * hgt]
        h = a_g * h + b[g * hgt:(g + 1) * hgt]
        h_slabs.append(h)
        if nb == 1:
            a_run = a_g * a_run
            a_slabs.append(a_run)
    if nb == 1:
        run_end = _linear_scan(a_run, h, 1)
        run_in = _shift_rows(run_end, 1, 0.0)
        h_slabs = [h_g + a_r * run_in for h_g, a_r in zip(h_slabs, a_slabs)]
        h_last = run_end[hgt - 1:hgt]
    else:
        h_last = h_slabs[-1]
    hprev_ref[...] = h_last
    hl_ref[...] = h_last
    rg = (_sigmoid(gate_b) * jnp.concatenate(h_slabs, axis=0)).astype(BF16)
    if nb == 1:
        rg = _dot(perm_ref[1], rg).astype(BF16)
    rg_ref[...] = rg

    q_heads(q_rr)
    k_heads(kp)
    ga_ref[...] = _sigmoid(gate_a).astype(BF16)


def _slab_permutation(rows):
    steps = rows // SUBLANES
    slab_row = jnp.arange(rows)
    time = (slab_row % SUBLANES) * steps + slab_row // SUBLANES
    gather = (time[:, None] == jnp.arange(rows)[None, :]).astype(BF16)
    return jnp.stack([gather, gather.T])


def _mixer_in(x, hist, h0, rope, wts, *, rows, nb):
    groups, tg, d_model = x.shape
    hp = hist.shape[1]
    q_lora = wts["wq"].shape[0]
    kv_lora = wts["wk"].shape[0]
    conv_w = wts["cw"].shape[0]
    n_t = tg // rows
    qk_w = N_HEADS * HEAD_PAD
    v_w = N_HEADS * V_DIM

    def row_spec(width):
        return pl.BlockSpec((None, rows, width), lambda g, t: (g, t, 0))

    def group_spec(r, width):
        return pl.BlockSpec((None, r, width), lambda g, t: (g, 0, 0))

    tab_spec = pl.BlockSpec((rows, HEAD_PAD), lambda g, t: (t, 0))
    w_names = ["g_mix", "w_in", "g_qa", "wq", "wqx", "gq", "gqx", "segq", "expq", "g_kva", "g_kr", "wk", "gk", "wvt",
               "cw", "cb", "wlru", "brg", "big", "lam"]
    w_list = [wts[n] for n in w_names] + ([_slab_permutation(rows)] if nb == 1 else [])
    in_specs = ([row_spec(d_model), group_spec(hp, d_model), group_spec(nb, d_model),
                 tab_spec, tab_spec, tab_spec] + [_const_spec(w.shape) for w in w_list])
    out_shape = (
        jax.ShapeDtypeStruct((groups, tg, qk_w), BF16),
        jax.ShapeDtypeStruct((groups, tg, qk_w), BF16),
        jax.ShapeDtypeStruct((groups, n_t, v_w, rows), BF16),
        jax.ShapeDtypeStruct((groups, tg, kv_lora), F32),
        jax.ShapeDtypeStruct((groups, tg, ROPE_DIM), F32),
        jax.ShapeDtypeStruct((groups, tg, d_model), BF16),
        jax.ShapeDtypeStruct((groups, tg, d_model), BF16),
        jax.ShapeDtypeStruct((groups, nb, d_model), F32),
        jax.ShapeDtypeStruct((groups, hp, d_model), F32),
    )
    vt_spec = pl.BlockSpec((None, None, v_w, rows), lambda g, t: (g, t, 0, 0))
    out_specs = (row_spec(qk_w), row_spec(qk_w), vt_spec, row_spec(kv_lora), row_spec(ROPE_DIM),
                 row_spec(d_model), row_spec(d_model), group_spec(nb, d_model), group_spec(hp, d_model))
    w_bytes = sum(w.size * w.dtype.itemsize for w in w_list)
    io_bytes = 2 * rows * (4 * d_model + 2 * (2 * qk_w + v_w + 2 * d_model) + 4 * (kv_lora + LANES) + 12 * LANES)
    tmp_bytes = 4 * rows * (wts["w_in"].shape[1] + 2 * qk_w + 6 * d_model) + 4 * (hp + rows) * d_model
    kern = functools.partial(_mixer_in_kernel, rows=rows, nb=nb, hp=hp, d_model=d_model,
                             q_lora=q_lora, kv_lora=kv_lora, conv_w=conv_w)
    return pl.pallas_call(
        kern,
        grid=(groups, n_t),
        in_specs=in_specs,
        out_specs=out_specs,
        out_shape=out_shape,
        scratch_shapes=[pltpu.VMEM((hp, d_model), F32), pltpu.VMEM((nb, d_model), F32)],
        compiler_params=pltpu.CompilerParams(
            dimension_semantics=("parallel", "arbitrary"),
            vmem_limit_bytes=_vmem_limit(w_bytes + io_bytes + tmp_bytes)),
        name="mixer_in",
    )(x, hist, h0, *rope, *w_list)


def _attn_kernel(q_ref, k_ref, vt_ref, ga_ref, rg_ref, o_ref, m_ref, acc_ref, carry_ref, *, tq, tk, cq, vblk):
    qi = pl.program_id(2)
    m_ref[...] = jnp.full(m_ref.shape, -jnp.inf, F32)
    acc_ref[...] = jnp.zeros(acc_ref.shape, F32)
    ones_rows = jnp.ones((acc_ref.shape[1] - V_DIM, vblk), BF16)

    n_sub = tq // tk

    def wide(kj, b):
        return [(kj, hh, b * tk, tk, tk, None) for hh in range(2)]

    def narrow(kj, b):
        return [(kj, hh, b * tk + q0, cq, q0 + cq, q0) for hh in range(2) for q0 in range(0, tk, cq)]

    def scores(kj, hh, q0, qw, n_keys):
        lanes = slice(hh * HEAD_PAD, (hh + 1) * HEAD_PAD)
        k0 = pl.multiple_of(kj * tk, tk)
        return _dot_nt(k_ref[pl.ds(k0, n_keys), lanes], q_ref[q0:q0 + qw, lanes])

    def diag_mask(s, q0):
        k_chunk = lax.broadcasted_iota(jnp.int32, s.shape, 0) >> CHUNK_SHIFT
        q_chunk = (q0 + lax.broadcasted_iota(jnp.int32, s.shape, 1)) >> CHUNK_SHIFT
        return jnp.where(k_chunk <= q_chunk, s, -jnp.inf)

    def softmax(hh, q0, qw, s):
        qs = slice(q0, q0 + qw)
        m_prev = m_ref[hh, :, qs]
        m_new = jnp.maximum(m_prev, jnp.max(s, axis=0, keepdims=True))
        m_ref[hh, :, qs] = m_new
        return jnp.exp2(s - m_new).astype(BF16), jnp.exp2(m_prev - m_new)

    def values(kj, hh, q0, qw, pb, alpha):
        qs = slice(q0, q0 + qw)
        pv = None
        for j in range(pb.shape[0] // vblk):
            vt = vt_ref[kj * (tk // vblk) + j, hh * V_DIM:(hh + 1) * V_DIM, :]
            part = _dot(jnp.concatenate([vt, ones_rows], axis=0), pb[j * vblk:(j + 1) * vblk])
            pv = part if pv is None else pv + part
        acc_ref[hh, :, qs] = alpha * acc_ref[hh, :, qs] + pv

    def run_chains(items, next_tile):
        _, _, _, qw0, n_keys0, _ = items[0]
        s_next = carry_ref[0:n_keys0, 0:qw0]
        prev = None
        for i, (kj, hh, q0, qw, n_keys, mask_q0) in enumerate(items):
            s_cur = s_next
            if i + 1 < len(items):
                s_next = scores(*items[i + 1][:5])
            elif next_tile is not None:
                carry_ref[...] = scores(*wide(next_tile, 0)[0][:5])
            if mask_q0 is not None:
                s_cur = diag_mask(s_cur, mask_q0)
            cur = softmax(hh, q0, qw, s_cur)
            if prev is not None:
                values(*items[i - 1][:4], *prev)
            prev = cur
        values(*items[-1][:4], *prev)

    carry_ref[...] = scores(*wide(0, 0)[0][:5])

    @pl.loop(0, n_sub * qi)
    def _(kj):
        run_chains([c for b in range(n_sub) for c in wide(kj, b)], kj + 1)

    diag_items = []
    for d in range(n_sub):
        kj = n_sub * qi + d
        diag_items += narrow(kj, d) + [c for b in range(d + 1, n_sub) for c in wide(kj, b)]
    run_chains(diag_items, None)

    attn_t = jnp.concatenate([acc_ref[hh, 0:V_DIM, :] / acc_ref[hh, V_DIM:V_DIM + 1, :] for hh in range(2)],
                             axis=0)
    o_ref[...] = (ga_ref[...].astype(F32) * attn_t.T + rg_ref[...].astype(F32)).astype(BF16)


def _attention(q, k, vt, ga, rg, *, tq, tk):
    bsz, t, _ = q.shape
    n_vb, _, vblk = vt.shape[1:]
    assert LANES == 2 * V_DIM and N_HEADS % 2 == 0
    assert k.shape[1] == t == n_vb * vblk and tk % vblk == 0 and t % tq == 0 and tq % tk == 0
    assert tk % CHUNK == 0 and tk % MXU_DIM == 0
    n_hp = N_HEADS // 2
    qk_blk = 2 * HEAD_PAD

    def q_index(b, p, qi):
        return (b, qi, p)

    kern = functools.partial(_attn_kernel, tq=tq, tk=tk, cq=MXU_DIM, vblk=vblk)
    blk_bytes = 2 * 2 * (tq * qk_blk + t * qk_blk + t * LANES + 3 * tq * LANES)
    tmp_bytes = 4 * 8 * tk * tk + 4 * 4 * tq * LANES
    return pl.pallas_call(
        kern,
        grid=(bsz, n_hp, t // tq),
        in_specs=[pl.BlockSpec((None, tq, qk_blk), q_index),
                  pl.BlockSpec((None, t, qk_blk), lambda b, p, qi: (b, 0, p)),
                  pl.BlockSpec((None, n_vb, LANES, vblk), lambda b, p, qi: (b, 0, p, 0)),
                  pl.BlockSpec((None, tq, LANES), q_index),
                  pl.BlockSpec((None, tq, LANES), q_index)],
        out_specs=pl.BlockSpec((None, tq, LANES), q_index),
        out_shape=jax.ShapeDtypeStruct((bsz, t, N_HEADS * V_DIM), BF16),
        scratch_shapes=[pltpu.VMEM((2, 1, tq), F32),
                        pltpu.VMEM((2, V_DIM + BF16_SUBLANES, tq), F32),
                        pltpu.VMEM((tk, tk), F32)],
        compiler_params=pltpu.CompilerParams(
            dimension_semantics=("parallel", "parallel", "arbitrary"),
            vmem_limit_bytes=_vmem_limit(blk_bytes + tmp_bytes)),
        name="attention",
    )(q, k, vt, ga, rg)


def _cache_attn_kernel(q_ref, latc_ref, krc_ref, latn_ref, krn_ref, ga_ref, rg_ref,
                       wkg_ref, wkc_ref, seg_ref, expand_ref, wv_ref, o_ref,
                       qabs_ref, qr_ref, m_ref, acc_ref, *, t, tk, n_new):
    past = latc_ref.shape[0]
    hq = N_HEADS * t
    for h in range(N_HEADS):
        qh = q_ref[:, h * HEAD_PAD:(h + 1) * HEAD_PAD]
        qabs_ref[h * t:(h + 1) * t, :] = _dot(qh, wkg_ref[h]).astype(BF16)
        qr_ref[h * t:(h + 1) * t, :] = qh[:, NOPE_DIM:QK_DIM]
    m_ref[...] = jnp.full(m_ref.shape, -jnp.inf, F32)
    acc_ref[...] = jnp.zeros(acc_ref.shape, F32)
    kv_lora = latc_ref.shape[1]

    def scores(lat, kr, n_valid):
        n = lat.shape[0]
        latb = lat.astype(BF16)
        kvk = _dot(latb, wkc_ref[...])
        ssum = _dot((kvk * kvk).astype(BF16), seg_ref[...])
        r = lax.rsqrt(ssum * (1.0 / NOPE_DIM) + EPS)
        r_hi = r.astype(BF16)
        r_lo = (r - r_hi.astype(F32)).astype(BF16)
        r_cols = _dot(jnp.concatenate([r_hi, r_lo], axis=1), expand_ref[...])
        s = _dot_nt(latb, qabs_ref[...]) * r_cols + _dot_nt(kr.astype(BF16), qr_ref[...])
        if n_valid < n:
            s = jnp.where(lax.broadcasted_iota(jnp.int32, (n, hq), 0) < n_valid, s, -jnp.inf)
        lat_t = jnp.concatenate([lat.T.astype(BF16), jnp.ones((acc_ref.shape[0] - kv_lora, n), BF16)], axis=0)
        return s, lat_t

    def accumulate(s, lat_t):
        m_prev = m_ref[...]
        m_new = jnp.maximum(m_prev, jnp.max(s, axis=0, keepdims=True))
        m_ref[...] = m_new
        p = jnp.exp2(s - m_new).astype(BF16)
        acc_ref[...] = jnp.exp2(m_prev - m_new) * acc_ref[...] + _dot(lat_t, p)

    def tile_scores(j):
        if j < past // tk:
            return scores(latc_ref[j * tk:(j + 1) * tk, :], krc_ref[j * tk:(j + 1) * tk, :], tk)
        return scores(latn_ref[...], krn_ref[...], n_new)

    n_tiles = past // tk + 1
    cur = tile_scores(0)
    for j in range(n_tiles):
        nxt = tile_scores(j + 1) if j + 1 < n_tiles else None
        accumulate(*cur)
        cur = nxt

    ctx = (acc_ref[0:kv_lora, :] / acc_ref[kv_lora:kv_lora + 1, :]).T.astype(BF16)
    first_head = lax.broadcasted_iota(jnp.int32, (t, LANES), 1) < V_DIM
    for pair in range(N_HEADS // 2):
        cols = slice(pair * LANES, (pair + 1) * LANES)
        wv = wv_ref[:, cols]
        a0 = _dot(ctx[(2 * pair) * t:(2 * pair + 1) * t], wv)
        a1 = _dot(ctx[(2 * pair + 1) * t:(2 * pair + 2) * t], wv)
        attn = jnp.where(first_head, a0, a1)
        o_ref[:, cols] = (ga_ref[:, cols].astype(F32) * attn + rg_ref[:, cols].astype(F32)).astype(BF16)


def _cache_attention(q, lat_cache, kr_cache, lat_new, kr_new, ga, rg, wts, *, tk, n_new):
    bsz, t, qk_w = q.shape
    past, kv_lora = lat_cache.shape[1:]
    n_pad = lat_new.shape[1]
    d_model = ga.shape[2]
    hq = N_HEADS * t
    assert past % tk == 0 and LANES == 2 * V_DIM
    w_list = [wts["wkg"], wts["wkc"], wts["seg"], wts["expand"], wts["wv"]]
    w_bytes = sum(w.size * w.dtype.itemsize for w in w_list)
    blk_bytes = 2 * (4 * (past + n_pad) * (kv_lora + LANES) + 2 * t * (qk_w + 3 * d_model))
    tmp_bytes = 4 * tk * (2 * N_HEADS * NOPE_DIM + 4 * hq)

    def batch_spec(r, width):
        return pl.BlockSpec((None, r, width), lambda b: (b, 0, 0))

    kern = functools.partial(_cache_attn_kernel, t=t, tk=tk, n_new=n_new)
    return pl.pallas_call(
        kern,
        grid=(bsz,),
        in_specs=[batch_spec(t, qk_w), batch_spec(past, kv_lora), batch_spec(past, ROPE_DIM),
                  batch_spec(n_pad, kv_lora), batch_spec(n_pad, ROPE_DIM),
                  batch_spec(t, d_model), batch_spec(t, d_model)] + [_const_spec(w.shape) for w in w_list],
        out_specs=batch_spec(t, d_model),
        out_shape=jax.ShapeDtypeStruct((bsz, t, d_model), BF16),
        scratch_shapes=[pltpu.VMEM((hq, kv_lora), BF16), pltpu.VMEM((hq, ROPE_DIM), BF16),
                        pltpu.VMEM((1, hq), F32), pltpu.VMEM((kv_lora + BF16_SUBLANES, hq), F32)],
        compiler_params=pltpu.CompilerParams(
            dimension_semantics=("parallel",),
            vmem_limit_bytes=_vmem_limit(w_bytes + blk_bytes + tmp_bytes)),
        name="cache_attention",
    )(q, lat_cache, kr_cache, lat_new, kr_new, ga, rg, *w_list)


def _mixer_out_kernel(x_ref, mix_ref, hist_ref, w_out_ref, g_ffn_ref, w_up_ref, fw_ref, fb_ref, w_down_ref,
                      y_ref, fst_ref, upc_ref, *, rows, nb, hp, d_model, d_ff, conv_w, col_blk):
    t = pl.program_id(1)

    @pl.when(t == 0)
    def _():
        upc_ref[0:hp, :] = hist_ref[...]

    x1 = x_ref[...] + _dot(mix_ref[...], w_out_ref[...])
    xn = (x1 * _rms_rows(x1, d_model) * g_ffn_ref[...]).astype(BF16)

    def up_proj(c):
        for c0 in (c * col_blk, d_ff + c * col_blk):
            sl = slice(c0, c0 + col_blk)
            upc_ref[hp:hp + rows, sl] = _dot(xn, w_up_ref[:, sl])

    def conv_cols(c0):
        sl = slice(c0, c0 + col_blk)
        out = fb_ref[:, sl] + fw_ref[conv_w - 1:conv_w, sl] * upc_ref[hp:hp + rows, sl]
        for j in range(1, conv_w):
            out = out + fw_ref[conv_w - 1 - j:conv_w - j, sl] * upc_ref[hp - j * nb:hp - j * nb + rows, sl]
        return out

    def gated(c):
        gate = conv_cols(c * col_blk)
        val = conv_cols(d_ff + c * col_blk)
        return (gate * _sigmoid(gate) * val).astype(BF16)

    def down_proj(c, hmid):
        return _dot(hmid, w_down_ref[c * col_blk:(c + 1) * col_blk, :])

    n_chunks = d_ff // col_blk
    y = x1
    up_proj(0)
    prev = None
    for c in range(n_chunks):
        if c + 1 < n_chunks:
            up_proj(c + 1)
        hmid = gated(c)
        if prev is not None:
            y = y + down_proj(c - 1, prev)
        prev = hmid
    y_ref[...] = y + down_proj(n_chunks - 1, prev)

    tail = upc_ref[rows:rows + hp, :]
    fst_ref[...] = tail
    upc_ref[0:hp, :] = tail


def _mixer_out(x, mixed, hist, wts, *, rows, nb):
    groups, tg, d_model = x.shape
    hp = hist.shape[1]
    d_ff = wts["w_down"].shape[0]
    conv_w = wts["fw"].shape[0]
    n_t = tg // rows
    col_blk = MXU_DIM

    def row_spec(width):
        return pl.BlockSpec((None, rows, width), lambda g, t: (g, t, 0))

    def group_spec(r, width):
        return pl.BlockSpec((None, r, width), lambda g, t: (g, 0, 0))

    w_names = ["w_out", "g_ffn", "w_up", "fw", "fb", "w_down"]
    w_list = [wts[n] for n in w_names]
    w_bytes = sum(w.size * w.dtype.itemsize for w in w_list)
    blk_bytes = 2 * rows * d_model * (4 + 2 + 4) + 4 * 4 * hp * 2 * d_ff
    tmp_bytes = 4 * (hp + rows) * 2 * d_ff + 4 * rows * (3 * d_model + 8 * col_blk)
    kern = functools.partial(_mixer_out_kernel, rows=rows, nb=nb, hp=hp, d_model=d_model, d_ff=d_ff,
                             conv_w=conv_w, col_blk=col_blk)
    return pl.pallas_call(
        kern,
        grid=(groups, n_t),
        in_specs=[row_spec(d_model), row_spec(d_model), group_spec(hp, 2 * d_ff)]
                 + [_const_spec(w.shape) for w in w_list],
        out_specs=(row_spec(d_model), group_spec(hp, 2 * d_ff)),
        out_shape=(jax.ShapeDtypeStruct((groups, tg, d_model), F32),
                   jax.ShapeDtypeStruct((groups, hp, 2 * d_ff), F32)),
        scratch_shapes=[pltpu.VMEM((hp + rows, 2 * d_ff), F32)],
        compiler_params=pltpu.CompilerParams(
            dimension_semantics=("parallel", "arbitrary"),
            vmem_limit_bytes=_vmem_limit(w_bytes + blk_bytes + tmp_bytes)),
        name="mixer_out",
    )(x, mixed, hist, *w_list)


def _head_pad(w, widths):
    lead = w.shape[:-1]
    per_head = sum(widths)
    w = w.reshape(lead + (N_HEADS, per_head))
    w = jnp.pad(w, [(0, 0)] * len(lead) + [(0, 0), (0, HEAD_PAD - per_head)])
    return w.reshape(lead + (N_HEADS * HEAD_PAD,))


def _prep_weights(l, g_mix_norm, w_in, g_q_a, w_q_b, g_kv_a, w_kv_b, g_qn, g_qr, g_kn, g_kr,
                  lru_conv_w, lru_conv_b, w_rg, b_rg, w_ig, b_ig, lru_lambda, w_out, g_ffn_norm,
                  w_up, ffn_conv_w, ffn_conv_b, w_down):
    d_model = w_in.shape[1]
    q_lora = w_q_b.shape[1]
    kv_lora = w_kv_b.shape[1]
    row = lambda a: a.reshape(1, -1)
    o_kr = q_lora + kv_lora
    wi = w_in[l].astype(BF16)
    w_kr = jnp.pad(wi[:, o_kr:o_kr + ROPE_DIM], ((0, 0), (NOPE_DIM, HEAD_PAD - QK_DIM)))
    w_in_p = jnp.concatenate([wi[:, :o_kr], w_kr, wi[:, o_kr + ROPE_DIM:]], axis=1)
    scale = QK_DIM ** -0.5 * LOG2_E
    gq = jnp.pad(jnp.concatenate([g_qn[l], g_qr[l]]) * scale, (0, HEAD_PAD - QK_DIM))
    g_rope = g_qr[l] * scale
    gqx = jnp.pad(jnp.concatenate([g_rope[HALF_ROPE:], g_rope[:HALF_ROPE]]), (NOPE_DIM, HEAD_PAD - QK_DIM))
    wq3 = w_q_b[l].reshape(q_lora, N_HEADS, QK_DIM)
    wqx = jnp.concatenate([jnp.zeros((q_lora, N_HEADS, NOPE_DIM), F32), wq3[:, :, NOPE_DIM + HALF_ROPE:],
                           wq3[:, :, NOPE_DIM:NOPE_DIM + HALF_ROPE]], axis=2).reshape(q_lora, N_HEADS * QK_DIM)
    lane = jnp.arange(HEAD_PAD)
    part = jnp.where(lane < NOPE_DIM, 0, jnp.where(lane < QK_DIM, 1, -1))
    seg_head = ((part[:, None] == jnp.arange(2)[None, :]).astype(F32)
                * jnp.array([1.0 / NOPE_DIM, 1.0 / ROPE_DIM], F32)[None, :])
    segq = jnp.pad(jnp.kron(jnp.eye(N_HEADS, dtype=F32), seg_head), ((0, 0), (0, LANES - 2 * N_HEADS)))
    spread = jnp.pad(jnp.kron(jnp.eye(N_HEADS, dtype=F32), (seg_head > 0).astype(F32).T),
                     ((0, LANES - 2 * N_HEADS), (0, 0)))
    expq = jnp.concatenate([spread, spread], axis=0)
    gk = jnp.tile(jnp.pad(g_kn[l], (0, HEAD_PAD - NOPE_DIM)), N_HEADS)
    kv = w_kv_b[l].reshape(kv_lora, N_HEADS, NOPE_DIM + V_DIM)
    wk = _head_pad(kv[:, :, :NOPE_DIM].reshape(kv_lora, N_HEADS * NOPE_DIM), (NOPE_DIM,))
    wv = kv[:, :, NOPE_DIM:].reshape(kv_lora, N_HEADS * V_DIM)
    wkg = jnp.pad(jnp.transpose(kv[:, :, :NOPE_DIM], (1, 2, 0)) * g_kn[l][None, :, None],
                  ((0, 0), (0, HEAD_PAD - NOPE_DIM), (0, 0)))
    seg = jnp.pad(jnp.repeat(jnp.eye(N_HEADS, dtype=F32), NOPE_DIM, axis=0), ((0, 0), (0, LANES - N_HEADS)))
    return {
        "wkg": wkg.astype(BF16), "wkc": kv[:, :, :NOPE_DIM].reshape(kv_lora, -1).astype(BF16),
        "seg": seg.astype(BF16), "wv": wv.astype(BF16),
        "g_mix": row(g_mix_norm[l]), "w_in": w_in_p, "g_qa": row(g_q_a[l]),
        "wq": _head_pad(w_q_b[l], (NOPE_DIM, ROPE_DIM)).astype(BF16), "gq": row(gq), "gqx": row(gqx),
        "wqx": _head_pad(wqx, (NOPE_DIM, ROPE_DIM)).astype(BF16),
        "segq": segq.astype(BF16), "expq": expq.astype(BF16),
        "g_kva": row(g_kv_a[l]), "g_kr": row(jnp.pad(g_kr[l], (NOPE_DIM, HEAD_PAD - QK_DIM))),
        "wk": wk.astype(BF16), "gk": row(gk), "wvt": wv.T.astype(BF16),
        "cw": lru_conv_w[l], "cb": row(lru_conv_b[l]),
        "wlru": jnp.concatenate([w_rg[l], w_ig[l]], axis=-1).astype(BF16),
        "brg": row(b_rg[l]), "big": row(b_ig[l]), "lam": row(lru_lambda[l]),
        "w_out": w_out[l].astype(BF16), "g_ffn": row(g_ffn_norm[l]), "w_up": w_up[l].astype(BF16),
        "fw": ffn_conv_w[l], "fb": row(ffn_conv_b[l]), "w_down": w_down[l].astype(BF16),
    }


def _rope_tables(pos):
    inv = np.float32(ROPE_THETA) ** (-np.arange(0, ROPE_DIM, 2, dtype=np.float32) / np.float32(ROPE_DIM))
    ang = pos.astype(np.float32)[:, None] * inv[None, :]
    cos = np.cos(ang.astype(np.float64)).astype(np.float32)
    sin = np.sin(ang.astype(np.float64)).astype(np.float32)
    n = pos.shape[0]
    ones_lo = np.ones((n, NOPE_DIM), np.float32)
    zeros_lo = np.zeros((n, NOPE_DIM), np.float32)
    zeros_half = np.zeros((n, HALF_ROPE), np.float32)
    tail = np.zeros((n, HEAD_PAD - QK_DIM), np.float32)
    c = np.concatenate([ones_lo, cos, cos, tail], axis=1)
    s_lo = np.concatenate([zeros_lo, -sin, zeros_half, tail], axis=1)
    s_hi = np.concatenate([zeros_lo, zeros_half, sin, tail], axis=1)
    return jnp.asarray(c), jnp.asarray(s_lo), jnp.asarray(s_hi)


def _expand_matrix(t):
    one_part = jnp.pad(jnp.repeat(jnp.eye(N_HEADS, dtype=F32), t, axis=1), ((0, LANES - N_HEADS), (0, 0)))
    return jnp.concatenate([one_part, one_part], axis=0).astype(BF16)


def _front_pad_rows(a, hp):
    return jnp.pad(a, ((0, 0), (hp - a.shape[1], 0), (0, 0)))


def _layer_prompt(x, wts, *, rows, out_rows, attn_tq, attn_tk):
    bsz, t, d_model = x.shape
    lru_w = wts["cw"].shape[0]
    ffn_w = wts["fw"].shape[0]
    d_ff2 = wts["w_up"].shape[1]
    hp1 = _round_up(lru_w - 1, SUBLANES)
    hp2 = _round_up(ffn_w - 1, SUBLANES)
    rope = _rope_tables(np.arange(t, dtype=np.int32))
    q, k, vt, lat, kr, ga, rg, h_last, cst = _mixer_in(
        x, jnp.zeros((bsz, hp1, d_model), F32), jnp.zeros((bsz, 1, d_model), F32), rope, wts, rows=rows, nb=1)
    mixed = _attention(q, k, vt, ga, rg, tq=attn_tq, tk=attn_tk)
    y, fst = _mixer_out(x, mixed, jnp.zeros((bsz, hp2, d_ff2), F32), wts, rows=out_rows, nb=1)
    return y, (lat, kr, h_last[:, 0], cst[:, hp1 - (lru_w - 1):], fst[:, hp2 - (ffn_w - 1):])


def _layer_sample(x, past_lat, past_kr, h0, lru_buf, ffn_buf, wts, *, tk):
    bsz, t, d_model = x.shape
    past = past_lat.shape[1]
    lru_w = wts["cw"].shape[0]
    ffn_w = wts["fw"].shape[0]
    hp1 = _round_up((lru_w - 1) * bsz, SUBLANES)
    hp2 = _round_up((ffn_w - 1) * bsz, SUBLANES)
    rows = t * bsz

    def to_tm(a):
        return jnp.swapaxes(a, 0, 1).reshape(1, a.shape[1] * bsz, a.shape[2])

    def from_tm(a):
        return jnp.swapaxes(a.reshape(a.shape[1] // bsz, bsz, a.shape[2]), 0, 1)

    rope = _rope_tables(np.repeat(past + np.arange(t, dtype=np.int32), bsz))
    q, k, vt, lat, kr, ga, rg, h_last, cst = _mixer_in(
        to_tm(x), _front_pad_rows(to_tm(lru_buf), hp1), h0[None], rope, wts, rows=rows, nb=bsz)

    assert (past % CHUNK) + t <= CHUNK, "cache attention assumes all keys visible to all queries"
    n_pad = _round_up(t, LANES)
    pad_rows = lambda a: jnp.pad(from_tm(a), ((0, 0), (0, n_pad - t), (0, 0)))
    cache_wts = dict(wts, expand=_expand_matrix(t))
    mixed = _cache_attention(from_tm(q), past_lat, past_kr, pad_rows(lat), pad_rows(kr),
                             from_tm(ga), from_tm(rg), cache_wts, tk=tk, n_new=t)

    y, fst = _mixer_out(to_tm(x), to_tm(mixed), _front_pad_rows(to_tm(ffn_buf), hp2), wts, rows=rows, nb=bsz)
    states = (from_tm(lat), from_tm(kr), h_last[0],
              from_tm(cst[:, hp1 - (lru_w - 1) * bsz:]), from_tm(fst[:, hp2 - (ffn_w - 1) * bsz:]))
    return from_tm(y), states


def kernel(x_prompt, x_sample, cache_kv_latent, cache_k_rope, state_lru_h, state_lru_conv, state_ffn_conv,
           g_mix_norm, w_in, g_q_a, w_q_b, g_kv_a, w_kv_b, g_qn, g_qr, g_kn, g_kr, lru_conv_w, lru_conv_b,
           w_rg, b_rg, w_ig, b_ig, lru_lambda, w_out, g_ffn_norm, w_up, ffn_conv_w, ffn_conv_b, w_down):
    depth = w_in.shape[0]
    yp, ys = x_prompt, x_sample
    p_states, s_states = [], []
    for l in range(depth):
        wts = _prep_weights(l, g_mix_norm, w_in, g_q_a, w_q_b, g_kv_a, w_kv_b, g_qn, g_qr, g_kn, g_kr,
                            lru_conv_w, lru_conv_b, w_rg, b_rg, w_ig, b_ig, lru_lambda, w_out, g_ffn_norm,
                            w_up, ffn_conv_w, ffn_conv_b, w_down)
        yp, st_p = _layer_prompt(yp, wts, rows=256, out_rows=512, attn_tq=1024, attn_tk=512)
        ys, st_s = _layer_sample(ys, cache_kv_latent[l], cache_k_rope[l], state_lru_h[l],
                                 state_lru_conv[l], state_ffn_conv[l], wts, tk=512)
        p_states.append(st_p)
        s_states.append(st_s)
    p_out = [jnp.stack([st[j] for st in p_states], axis=0) for j in range(5)]
    s_out = [jnp.stack([st[j] for st in s_states], axis=0) for j in range(5)]
    return (yp, ys, *p_out, *s_out)
```

```python
import functools

import jax
import jax.numpy as jnp
import numpy as np
from jax import lax
from jax.experimental import pallas as pl
from jax.experimental.pallas import tpu as pltpu

CHUNK = 64
CHUNK_SHIFT = CHUNK.bit_length() - 1
assert CHUNK == 1 << CHUNK_SHIFT
N_HEADS = 16
NOPE_DIM = 64
ROPE_DIM = 32
V_DIM = 64
QK_DIM = NOPE_DIM + ROPE_DIM
ROPE_THETA = 10000.0
RG_C = 8.0
EPS = 1e-6
LOG2_E = 1.4426950408889634

LANES = 128
SUBLANES = 8
BF16_SUBLANES = 16
MXU_DIM = 256
VMEM_BYTES_V7X = 64 * 1024 * 1024

HEAD_PAD = LANES
SLAB_SUB_ROWS = MXU_DIM
HALF_ROPE = ROPE_DIM // 2
F32 = jnp.float32
BF16 = jnp.bfloat16


def _round_up(n, m):
    return (n + m - 1) // m * m


def _vmem_limit(nbytes):
    return int(min(2 * nbytes, VMEM_BYTES_V7X - 8 * 1024 * 1024))


def _const_spec(shape):
    nd = len(shape)
    return pl.BlockSpec(shape, lambda *_: (0,) * nd, pipeline_mode=pl.Buffered(1))


def _dot(a, b):
    return jnp.dot(a, b, preferred_element_type=F32)


def _dot_nt(a, b):
    return lax.dot_general(a, b, (((1,), (1,)), ((), ())), preferred_element_type=F32)


def _sigmoid(x):
    return 0.5 * jnp.tanh(0.5 * x) + 0.5


def _rms_rows(x, n):
    return lax.rsqrt(jnp.sum(x * x, axis=-1, keepdims=True) * (1.0 / n) + EPS)


def _rope_head(x, c, s_lo, s_hi):
    return (x * c + pltpu.roll(x, HALF_ROPE, 1) * s_hi
            + pltpu.roll(x, HEAD_PAD - HALF_ROPE, 1) * s_lo)


def _shift_rows(x, n, fill):
    rows, cols = x.shape
    if n % SUBLANES == 0:
        return jnp.concatenate([jnp.full((n, cols), fill, x.dtype), x[:rows - n]], axis=0)
    rolled = pltpu.roll(x, n, 0)
    row = lax.broadcasted_iota(jnp.int32, x.shape, 0)
    return jnp.where(row >= n, rolled, fill)


def _linear_scan(a, b, nb):
    rows = a.shape[0]
    s = nb
    while s < rows:
        b = a * _shift_rows(b, s, 0.0) + b
        if 2 * s < rows:
            a = a * _shift_rows(a, s, 1.0)
        s *= 2
    return b


def _mixer_in_kernel(*refs, rows, sub, **static):
    hist_ref, h0_ref = refs[1], refs[2]
    xh_ref, hprev_ref = refs[-2], refs[-1]

    @pl.when(pl.program_id(1) == 0)
    def _():
        xh_ref[...] = hist_ref[...]
        hprev_ref[...] = h0_ref[...]

    waiting = [_mixer_in_stages(*refs, r0=r0, sub=sub, **static) for r0 in range(0, rows, sub)]
    running = []
    while waiting or running:
        if waiting:
            running.append(waiting.pop(0))
        for stages in list(running):
            if next(stages, "done") == "done":
                running.remove(stages)


def _mixer_in_stages(x_ref, hist_ref, h0_ref, cos_ref, slo_ref, shi_ref,
                     g_mix_ref, w_in_ref, g_qa_ref, wq_ref, wqx_ref, gq_ref, gqx_ref,
                     g_kva_ref, g_kr_ref,
                     wk_ref, gk_ref, wvt_ref, cw_ref, cb_ref, wlru_ref, brg_ref, big_ref, lam_ref, *rest,
                     r0, sub, nb, hp, d_model, q_lora, kv_lora, conv_w):
    if nb == 1:
        perm_ref, *rest = rest
    q_ref, k_ref, vt_ref, lat_ref, kr_ref, ga_ref, rg_ref, hl_ref, cst_ref, xh_ref, hprev_ref = rest
    rs = slice(r0, r0 + sub)
    rows = sub

    o_kv = q_lora
    o_kr = o_kv + kv_lora
    o_u = o_kr + HEAD_PAD
    o_ga = o_u + d_model
    o_gb = o_ga + d_model
    cos = cos_ref[rs, :]
    s_lo = slo_ref[rs, :]
    s_hi = shi_ref[rs, :]

    x = x_ref[rs, :]
    xn = (x * _rms_rows(x, d_model) * g_mix_ref[...]).astype(BF16)
    yield

    xs = _dot(perm_ref[0], xn).astype(BF16) if nb == 1 else xn

    def in_proj(lhs, c0, c1):
        return _dot(lhs, w_in_ref[:, c0:c1])

    z_lat = in_proj(xn, 0, o_u)
    u = in_proj(xs, o_u, o_ga)
    yield

    cq = z_lat[:, 0:o_kv]
    ckv = z_lat[:, o_kv:o_kr]
    krb = z_lat[:, o_kr:o_u]
    lat = ckv * _rms_rows(ckv, kv_lora) * g_kva_ref[...]
    lat_ref[rs, :] = lat
    latb = lat.astype(BF16)
    kr = _rope_head(krb * _rms_rows(krb, ROPE_DIM) * g_kr_ref[...], cos, s_lo, s_hi)
    kr_ref[rs, :] = kr[:, NOPE_DIM:NOPE_DIM + ROPE_DIM]
    cqn = (cq * _rms_rows(cq, q_lora) * g_qa_ref[...]).astype(BF16)
    yield

    qp = _dot(cqn, wq_ref[...])
    qp2 = _dot(cqn, wqx_ref[...])
    yield

    def q_heads():
        cg = cos * gq_ref[...]
        sg = (s_lo + s_hi) * gqx_ref[...]
        is_nope = lax.broadcasted_iota(jnp.int32, (rows, HEAD_PAD), 1) < NOPE_DIM
        for h in range(N_HEADS):
            sl = slice(h * HEAD_PAD, (h + 1) * HEAD_PAD)
            qh = qp[:, sl]
            sq = qh * qh
            r_n = lax.rsqrt(jnp.sum(jnp.where(is_nope, sq, 0.0), axis=-1, keepdims=True)
                            * (1.0 / NOPE_DIM) + EPS)
            r_r = lax.rsqrt(jnp.sum(jnp.where(is_nope, 0.0, sq), axis=-1, keepdims=True)
                            * (1.0 / ROPE_DIM) + EPS)
            q_ref[rs, sl] = (jnp.where(is_nope, r_n, r_r) * (qh * cg + qp2[:, sl] * sg)).astype(BF16)

    def k_heads(kp):
        for h in range(N_HEADS):
            sl = slice(h * HEAD_PAD, (h + 1) * HEAD_PAD)
            kh = kp[:, sl]
            k_ref[rs, sl] = (kh * _rms_rows(kh, NOPE_DIM) * gk_ref[:, sl] + kr).astype(BF16)

    hgt = SUBLANES if nb == 1 else nb
    steps = rows // hgt
    x_slabs = {g: u[g * hgt:(g + 1) * hgt] for g in range(steps)}
    first_run = lax.broadcasted_iota(jnp.int32, (hgt, d_model), 0) == 0
    for j in range(1, conv_w):
        if nb == 1:
            x_slabs[-j] = jnp.where(first_run, xh_ref[hp - j:hp - j + 1, :], pltpu.roll(x_slabs[steps - j], 1, 0))
        else:
            x_slabs[-j] = xh_ref[hp - j * nb:hp - (j - 1) * nb, :]
    uc_slabs = []
    for g in range(steps):
        acc = cb_ref[...] + cw_ref[conv_w - 1:conv_w, :] * x_slabs[g]
        for j in range(1, conv_w):
            acc = acc + cw_ref[conv_w - 1 - j:conv_w - j, :] * x_slabs[g - j]
        uc_slabs.append(acc)
    u_c = jnp.concatenate(uc_slabs, axis=0)
    if nb == 1:
        row = lax.broadcasted_iota(jnp.int32, (hp, d_model), 0)
        tail = jnp.zeros((hp, d_model), F32)
        for j in range(1, conv_w):
            last = jnp.broadcast_to(x_slabs[steps - j][hgt - 1:hgt], (hp, d_model))
            tail = jnp.where(row == hp - j, last, tail)
    else:
        tail = u[rows - hp:rows]
    cst_ref[...] = tail
    xh_ref[...] = tail
    yield

    n_blocks, blk_w, _ = wlru_ref.shape
    gate_proj = [_dot(u_c[:, n * blk_w:(n + 1) * blk_w].astype(BF16), wlru_ref[n])
                 for n in range(n_blocks)]

    kp = _dot(latb, wk_ref[...])
    vt_ref[:, rs] = _dot_nt(wvt_ref[...], latb).astype(BF16)
    gate_a = in_proj(xn, o_ga, o_gb)
    gate_b = in_proj(xs, o_gb, o_gb + d_model)
    yield

    lam = lam_ref[...]
    softplus_neg = jnp.maximum(-lam, 0.0) + jnp.log1p(jnp.exp(-jnp.abs(lam)))
    a_parts, b_parts = [], []
    for n in range(n_blocks):
        sl = slice(n * blk_w, (n + 1) * blk_w)
        ucn = u_c[:, sl]
        g = gate_proj[n]
        r = _sigmoid(g[:, 0:blk_w] + brg_ref[:, sl])
        i = _sigmoid(g[:, blk_w:2 * blk_w] + big_ref[:, sl])
        log_a = (-RG_C * r) * softplus_neg[:, sl]
        a_n = jnp.exp(log_a)
        a_parts.append(a_n)
        b_parts.append(jnp.sqrt(-jnp.tanh(log_a) * (1.0 + a_n * a_n)) * (i * ucn))
    a = jnp.concatenate(a_parts, axis=1)
    b = jnp.concatenate(b_parts, axis=1)

    a_g = a[0:hgt]
    h_in = a_g * hprev_ref[...]
    if nb == 1:
        h_in = jnp.where(lax.broadcasted_iota(jnp.int32, h_in.shape, 0) == 0, h_in, 0.0)
    h = b[0:hgt] + h_in
    a_run = a_g
    h_slabs, a_slabs = [h], [a_run]
    for g in range(1, steps):
        a_g = a[g * hgt:(g + 1) * hgt]
        h = a_g * h + b[g * hgt:(g + 1) * hgt]
        h_slabs.append(h)
        if nb == 1:
            a_run = a_g * a_run
            a_slabs.append(a_run)
    if nb == 1:
        run_end = _linear_scan(a_run, h, 1)
        run_in = _shift_rows(run_end, 1, 0.0)
        h_slabs = [h_g + a_r * run_in for h_g, a_r in zip(h_slabs, a_slabs)]
        h_last = run_end[hgt - 1:hgt]
    else:
        h_last = h_slabs[-1]
    hprev_ref[...] = h_last
    hl_ref[...] = h_last
    rg = (_sigmoid(gate_b) * jnp.concatenate(h_slabs, axis=0)).astype(BF16)
    if nb == 1:
        rg = _dot(perm_ref[1], rg).astype(BF16)
    rg_ref[rs, :] = rg
    yield

    q_heads()
    k_heads(kp)
    ga_ref[rs, :] = _sigmoid(gate_a).astype(BF16)


def _slab_permutation(rows):
    steps = rows // SUBLANES
    slab_row = jnp.arange(rows)
    time = (slab_row % SUBLANES) * steps + slab_row // SUBLANES
    gather = (time[:, None] == jnp.arange(rows)[None, :]).astype(BF16)
    return jnp.stack([gather, gather.T])


def _mixer_in(x, hist, h0, rope, wts, *, rows, nb):
    groups, tg, d_model = x.shape
    hp = hist.shape[1]
    q_lora = wts["wq"].shape[0]
    kv_lora = wts["wk"].shape[0]
    conv_w = wts["cw"].shape[0]
    n_t = tg // rows
    qk_w = N_HEADS * HEAD_PAD
    v_w = N_HEADS * V_DIM

    def row_spec(width):
        return pl.BlockSpec((None, rows, width), lambda g, t: (g, t, 0))

    def group_spec(r, width):
        return pl.BlockSpec((None, r, width), lambda g, t: (g, 0, 0))

    tab_spec = pl.BlockSpec((rows, HEAD_PAD), lambda g, t: (t, 0))
    w_names = ["g_mix", "w_in", "g_qa", "wq", "wqx", "gq", "gqx", "g_kva", "g_kr", "wk", "gk", "wvt",
               "cw", "cb", "wlru", "brg", "big", "lam"]
    sub = min(rows, SLAB_SUB_ROWS) if nb == 1 else rows
    assert rows % sub == 0
    w_list = [wts[n] for n in w_names] + ([_slab_permutation(sub)] if nb == 1 else [])
    in_specs = ([row_spec(d_model), group_spec(hp, d_model), group_spec(nb, d_model),
                 tab_spec, tab_spec, tab_spec] + [_const_spec(w.shape) for w in w_list])
    out_shape = (
        jax.ShapeDtypeStruct((groups, tg, qk_w), BF16),
        jax.ShapeDtypeStruct((groups, tg, qk_w), BF16),
        jax.ShapeDtypeStruct((groups, n_t, v_w, rows), BF16),
        jax.ShapeDtypeStruct((groups, tg, kv_lora), F32),
        jax.ShapeDtypeStruct((groups, tg, ROPE_DIM), F32),
        jax.ShapeDtypeStruct((groups, tg, d_model), BF16),
        jax.ShapeDtypeStruct((groups, tg, d_model), BF16),
        jax.ShapeDtypeStruct((groups, nb, d_model), F32),
        jax.ShapeDtypeStruct((groups, hp, d_model), F32),
    )
    vt_spec = pl.BlockSpec((None, None, v_w, rows), lambda g, t: (g, t, 0, 0))
    out_specs = (row_spec(qk_w), row_spec(qk_w), vt_spec, row_spec(kv_lora), row_spec(ROPE_DIM),
                 row_spec(d_model), row_spec(d_model), group_spec(nb, d_model), group_spec(hp, d_model))
    w_bytes = sum(w.size * w.dtype.itemsize for w in w_list)
    io_bytes = 2 * rows * (4 * d_model + 2 * (2 * qk_w + v_w + 2 * d_model) + 4 * (kv_lora + LANES) + 12 * LANES)
    tmp_bytes = 4 * rows * (wts["w_in"].shape[1] + 2 * qk_w + 6 * d_model) + 4 * (hp + rows) * d_model
    kern = functools.partial(_mixer_in_kernel, rows=rows, sub=sub, nb=nb, hp=hp, d_model=d_model,
                             q_lora=q_lora, kv_lora=kv_lora, conv_w=conv_w)
    return pl.pallas_call(
        kern,
        grid=(groups, n_t),
        in_specs=in_specs,
        out_specs=out_specs,
        out_shape=out_shape,
        scratch_shapes=[pltpu.VMEM((hp, d_model), F32), pltpu.VMEM((nb, d_model), F32)],
        compiler_params=pltpu.CompilerParams(
            dimension_semantics=("parallel", "arbitrary"),
            vmem_limit_bytes=_vmem_limit(w_bytes + io_bytes + tmp_bytes)),
        name="mixer_in",
    )(x, hist, h0, *rope, *w_list)


def _attn_kernel(q_ref, k_ref, vt_ref, ga_ref, rg_ref, o_ref, m_ref, acc_ref, carry_ref, *, tq, tk, cq, vblk):
    qi = pl.program_id(2)
    m_ref[...] = jnp.full(m_ref.shape, -jnp.inf, F32)
    acc_ref[...] = jnp.zeros(acc_ref.shape, F32)

    n_sub = tq // tk

    def wide(kj, b):
        return [(kj, hh, b * tk, tk, tk, None) for hh in range(2)]

    def narrow(kj, b):
        return [(kj, hh, b * tk + q0, cq, q0 + cq, q0) for hh in range(2) for q0 in range(0, tk, cq)]

    def scores(kj, hh, q0, qw, n_keys):
        lanes = slice(hh * HEAD_PAD, (hh + 1) * HEAD_PAD)
        k0 = pl.multiple_of(kj * tk, tk)
        return _dot_nt(k_ref[pl.ds(k0, n_keys), lanes], q_ref[q0:q0 + qw, lanes])

    def diag_mask(s, q0):
        k_chunk = lax.broadcasted_iota(jnp.int32, s.shape, 0) >> CHUNK_SHIFT
        q_chunk = (q0 + lax.broadcasted_iota(jnp.int32, s.shape, 1)) >> CHUNK_SHIFT
        return jnp.where(k_chunk <= q_chunk, s, -jnp.inf)

    def softmax(hh, q0, qw, s):
        qs = slice(q0, q0 + qw)
        m_prev = m_ref[hh, :, qs]
        m_new = jnp.maximum(m_prev, jnp.max(s, axis=0, keepdims=True))
        m_ref[hh, :, qs] = m_new
        return jnp.exp2(s - m_new).astype(BF16), jnp.exp2(m_prev - m_new)

    def values(kj, hh, q0, qw, pb, alpha):
        qs = slice(q0, q0 + qw)
        pv = None
        n_keys = pb.shape[0]
        piece = min(vblk, n_keys)
        for j0 in range(0, n_keys, piece):
            vt = vt_ref[kj * (tk // vblk) + j0 // vblk, hh * V_DIM:(hh + 1) * V_DIM, pl.ds(j0 % vblk, piece)]
            ones_rows = jnp.ones((acc_ref.shape[1] - V_DIM, piece), BF16)
            part = _dot(jnp.concatenate([vt, ones_rows], axis=0), pb[j0:j0 + piece])
            pv = part if pv is None else pv + part
        acc_ref[hh, :, qs] = alpha * acc_ref[hh, :, qs] + pv

    def run_chains(items, next_tile):
        _, _, _, qw0, n_keys0, _ = items[0]
        s_next = carry_ref[0:n_keys0, 0:qw0]
        prev = None
        for i, (kj, hh, q0, qw, n_keys, mask_q0) in enumerate(items):
            s_cur = s_next
            if i + 1 < len(items):
                s_next = scores(*items[i + 1][:5])
            elif next_tile is not None:
                carry_ref[...] = scores(*wide(next_tile, 0)[0][:5])
            if mask_q0 is not None:
                s_cur = diag_mask(s_cur, mask_q0)
            cur = softmax(hh, q0, qw, s_cur)
            if prev is not None:
                values(*items[i - 1][:4], *prev)
            prev = cur
        values(*items[-1][:4], *prev)

    carry_ref[...] = scores(*wide(0, 0)[0][:5])

    @pl.loop(0, qi)
    def _(j):
        run_chains([c for d in range(n_sub) for b in range(n_sub) for c in wide(n_sub * j + d, b)],
                   n_sub * (j + 1))

    diag_items = []
    for d in range(n_sub):
        kj = n_sub * qi + d
        diag_items += narrow(kj, d) + [c for b in range(d + 1, n_sub) for c in wide(kj, b)]
    run_chains(diag_items, None)

    attn_t = jnp.concatenate([acc_ref[hh, 0:V_DIM, :] / acc_ref[hh, V_DIM:V_DIM + 1, :] for hh in range(2)],
                             axis=0)
    o_ref[...] = (ga_ref[...].astype(F32) * attn_t.T + rg_ref[...].astype(F32)).astype(BF16)


def _attention(q, k, vt, ga, rg, *, tq, tk):
    bsz, t, _ = q.shape
    n_vb, _, vblk = vt.shape[1:]
    assert LANES == 2 * V_DIM and N_HEADS % 2 == 0
    assert k.shape[1] == t == n_vb * vblk and tk % vblk == 0 and t % tq == 0 and tq % tk == 0
    assert tk % CHUNK == 0 and tk % MXU_DIM == 0
    n_hp = N_HEADS // 2
    qk_blk = 2 * HEAD_PAD

    def q_index(b, p, qi):
        return (b, qi, p)

    kern = functools.partial(_attn_kernel, tq=tq, tk=tk, cq=MXU_DIM, vblk=vblk)
    blk_bytes = 2 * 2 * (tq * qk_blk + t * qk_blk + t * LANES + 3 * tq * LANES)
    tmp_bytes = 4 * 8 * tk * tk + 4 * 4 * tq * LANES
    return pl.pallas_call(
        kern,
        grid=(bsz, n_hp, t // tq),
        in_specs=[pl.BlockSpec((None, tq, qk_blk), q_index),
                  pl.BlockSpec((None, t, qk_blk), lambda b, p, qi: (b, 0, p)),
                  pl.BlockSpec((None, n_vb, LANES, vblk), lambda b, p, qi: (b, 0, p, 0)),
                  pl.BlockSpec((None, tq, LANES), q_index),
                  pl.BlockSpec((None, tq, LANES), q_index)],
        out_specs=pl.BlockSpec((None, tq, LANES), q_index),
        out_shape=jax.ShapeDtypeStruct((bsz, t, N_HEADS * V_DIM), BF16),
        scratch_shapes=[pltpu.VMEM((2, 1, tq), F32),
                        pltpu.VMEM((2, V_DIM + BF16_SUBLANES, tq), F32),
                        pltpu.VMEM((tk, tk), F32)],
        compiler_params=pltpu.CompilerParams(
            dimension_semantics=("parallel", "parallel", "arbitrary"),
            vmem_limit_bytes=_vmem_limit(blk_bytes + tmp_bytes)),
        name="attention",
    )(q, k, vt, ga, rg)


def _cache_attn_kernel(q_ref, latc_ref, krc_ref, latn_ref, krn_ref, ga_ref, rg_ref,
                       wkg_ref, wkc_ref, seg_ref, expand_ref, wv_ref, o_ref,
                       qabs_ref, qr_ref, m_ref, acc_ref, *, t, tk, n_new):
    past = latc_ref.shape[0]
    hq = N_HEADS * t
    for h in range(N_HEADS):
        qh = q_ref[:, h * HEAD_PAD:(h + 1) * HEAD_PAD]
        qabs_ref[h * t:(h + 1) * t, :] = _dot(qh, wkg_ref[h]).astype(BF16)
        qr_ref[h * t:(h + 1) * t, :] = qh[:, NOPE_DIM:QK_DIM]
    m_ref[...] = jnp.full(m_ref.shape, -jnp.inf, F32)
    acc_ref[...] = jnp.zeros(acc_ref.shape, F32)
    kv_lora = latc_ref.shape[1]

    def scores(lat, kr, n_valid):
        n = lat.shape[0]
        latb = lat.astype(BF16)
        kvk = _dot(latb, wkc_ref[...])
        ssum = _dot((kvk * kvk).astype(BF16), seg_ref[...])
        r = lax.rsqrt(ssum * (1.0 / NOPE_DIM) + EPS)
        r_hi = r.astype(BF16)
        r_lo = (r - r_hi.astype(F32)).astype(BF16)
        r_cols = _dot(jnp.concatenate([r_hi, r_lo], axis=1), expand_ref[...])
        s = _dot_nt(latb, qabs_ref[...]) * r_cols + _dot_nt(kr.astype(BF16), qr_ref[...])
        if n_valid < n:
            s = jnp.where(lax.broadcasted_iota(jnp.int32, (n, hq), 0) < n_valid, s, -jnp.inf)
        lat_t = jnp.concatenate([lat.T.astype(BF16), jnp.ones((acc_ref.shape[0] - kv_lora, n), BF16)], axis=0)
        return s, lat_t

    def accumulate(s, lat_t):
        m_prev = m_ref[...]
        m_new = jnp.maximum(m_prev, jnp.max(s, axis=0, keepdims=True))
        m_ref[...] = m_new
        p = jnp.exp2(s - m_new).astype(BF16)
        acc_ref[...] = jnp.exp2(m_prev - m_new) * acc_ref[...] + _dot(lat_t, p)

    def tile_scores(j):
        if j < past // tk:
            return scores(latc_ref[j * tk:(j + 1) * tk, :], krc_ref[j * tk:(j + 1) * tk, :], tk)
        return scores(latn_ref[...], krn_ref[...], n_new)

    n_tiles = past // tk + 1
    cur = tile_scores(0)
    for j in range(n_tiles):
        nxt = tile_scores(j + 1) if j + 1 < n_tiles else None
        accumulate(*cur)
        cur = nxt

    ctx = (acc_ref[0:kv_lora, :] / acc_ref[kv_lora:kv_lora + 1, :]).T.astype(BF16)
    first_head = lax.broadcasted_iota(jnp.int32, (t, LANES), 1) < V_DIM
    for pair in range(N_HEADS // 2):
        cols = slice(pair * LANES, (pair + 1) * LANES)
        wv = wv_ref[:, cols]
        a0 = _dot(ctx[(2 * pair) * t:(2 * pair + 1) * t], wv)
        a1 = _dot(ctx[(2 * pair + 1) * t:(2 * pair + 2) * t], wv)
        attn = jnp.where(first_head, a0, a1)
        o_ref[:, cols] = (ga_ref[:, cols].astype(F32) * attn + rg_ref[:, cols].astype(F32)).astype(BF16)


def _cache_attention(q, lat_cache, kr_cache, lat_new, kr_new, ga, rg, wts, *, tk, n_new):
    bsz, t, qk_w = q.shape
    past, kv_lora = lat_cache.shape[1:]
    n_pad = lat_new.shape[1]
    d_model = ga.shape[2]
    hq = N_HEADS * t
    assert past % tk == 0 and LANES == 2 * V_DIM
    w_list = [wts["wkg"], wts["wkc"], wts["seg"], wts["expand"], wts["wv"]]
    w_bytes = sum(w.size * w.dtype.itemsize for w in w_list)
    blk_bytes = 2 * (4 * (past + n_pad) * (kv_lora + LANES) + 2 * t * (qk_w + 3 * d_model))
    tmp_bytes = 4 * tk * (2 * N_HEADS * NOPE_DIM + 4 * hq)

    def batch_spec(r, width):
        return pl.BlockSpec((None, r, width), lambda b: (b, 0, 0))

    kern = functools.partial(_cache_attn_kernel, t=t, tk=tk, n_new=n_new)
    return pl.pallas_call(
        kern,
        grid=(bsz,),
        in_specs=[batch_spec(t, qk_w), batch_spec(past, kv_lora), batch_spec(past, ROPE_DIM),
                  batch_spec(n_pad, kv_lora), batch_spec(n_pad, ROPE_DIM),
                  batch_spec(t, d_model), batch_spec(t, d_model)] + [_const_spec(w.shape) for w in w_list],
        out_specs=batch_spec(t, d_model),
        out_shape=jax.ShapeDtypeStruct((bsz, t, d_model), BF16),
        scratch_shapes=[pltpu.VMEM((hq, kv_lora), BF16), pltpu.VMEM((hq, ROPE_DIM), BF16),
                        pltpu.VMEM((1, hq), F32), pltpu.VMEM((kv_lora + BF16_SUBLANES, hq), F32)],
        compiler_params=pltpu.CompilerParams(
            dimension_semantics=("parallel",),
            vmem_limit_bytes=_vmem_limit(w_bytes + blk_bytes + tmp_bytes)),
        name="cache_attention",
    )(q, lat_cache, kr_cache, lat_new, kr_new, ga, rg, *w_list)


def _mixer_out_kernel(x_ref, mix_ref, hist_ref, w_out_ref, g_ffn_ref, w_up_ref, fw_ref, fb_ref, w_down_ref,
                      y_ref, fst_ref, upc_ref, *, rows, nb, hp, d_model, d_ff, conv_w, col_blk):
    t = pl.program_id(1)

    @pl.when(t == 0)
    def _():
        upc_ref[0:hp, :] = hist_ref[...]

    x1 = x_ref[...] + _dot(mix_ref[...], w_out_ref[...])
    xn = (x1 * _rms_rows(x1, d_model) * g_ffn_ref[...]).astype(BF16)

    def up_proj(c):
        for c0 in (c * col_blk, d_ff + c * col_blk):
            sl = slice(c0, c0 + col_blk)
            upc_ref[hp:hp + rows, sl] = _dot(xn, w_up_ref[:, sl])

    def conv_cols(c0):
        sl = slice(c0, c0 + col_blk)
        out = fb_ref[:, sl] + fw_ref[conv_w - 1:conv_w, sl] * upc_ref[hp:hp + rows, sl]
        for j in range(1, conv_w):
            out = out + fw_ref[conv_w - 1 - j:conv_w - j, sl] * upc_ref[hp - j * nb:hp - j * nb + rows, sl]
        return out

    def gated(c):
        gate = conv_cols(c * col_blk)
        val = conv_cols(d_ff + c * col_blk)
        return (gate * _sigmoid(gate) * val).astype(BF16)

    def down_proj(c, hmid):
        return _dot(hmid, w_down_ref[c * col_blk:(c + 1) * col_blk, :])

    n_chunks = d_ff // col_blk
    y = x1
    up_proj(0)
    prev = None
    for c in range(n_chunks):
        if c + 1 < n_chunks:
            up_proj(c + 1)
        hmid = gated(c)
        if prev is not None:
            y = y + down_proj(c - 1, prev)
        prev = hmid
    y_ref[...] = y + down_proj(n_chunks - 1, prev)

    tail = upc_ref[rows:rows + hp, :]
    fst_ref[...] = tail
    upc_ref[0:hp, :] = tail


def _mixer_out(x, mixed, hist, wts, *, rows, nb):
    groups, tg, d_model = x.shape
    hp = hist.shape[1]
    d_ff = wts["w_down"].shape[0]
    conv_w = wts["fw"].shape[0]
    n_t = tg // rows
    col_blk = MXU_DIM

    def row_spec(width):
        return pl.BlockSpec((None, rows, width), lambda g, t: (g, t, 0))

    def group_spec(r, width):
        return pl.BlockSpec((None, r, width), lambda g, t: (g, 0, 0))

    w_names = ["w_out", "g_ffn", "w_up", "fw", "fb", "w_down"]
    w_list = [wts[n] for n in w_names]
    w_bytes = sum(w.size * w.dtype.itemsize for w in w_list)
    blk_bytes = 2 * rows * d_model * (4 + 2 + 4) + 4 * 4 * hp * 2 * d_ff
    tmp_bytes = 4 * (hp + rows) * 2 * d_ff + 4 * rows * (3 * d_model + 8 * col_blk)
    kern = functools.partial(_mixer_out_kernel, rows=rows, nb=nb, hp=hp, d_model=d_model, d_ff=d_ff,
                             conv_w=conv_w, col_blk=col_blk)
    return pl.pallas_call(
        kern,
        grid=(groups, n_t),
        in_specs=[row_spec(d_model), row_spec(d_model), group_spec(hp, 2 * d_ff)]
                 + [_const_spec(w.shape) for w in w_list],
        out_specs=(row_spec(d_model), group_spec(hp, 2 * d_ff)),
        out_shape=(jax.ShapeDtypeStruct((groups, tg, d_model), F32),
                   jax.ShapeDtypeStruct((groups, hp, 2 * d_ff), F32)),
        scratch_shapes=[pltpu.VMEM((hp + rows, 2 * d_ff), F32)],
        compiler_params=pltpu.CompilerParams(
            dimension_semantics=("parallel", "arbitrary"),
            vmem_limit_bytes=_vmem_limit(w_bytes + blk_bytes + tmp_bytes)),
        name="mixer_out",
    )(x, mixed, hist, *w_list)


def _head_pad(w, widths):
    lead = w.shape[:-1]
    per_head = sum(widths)
    w = w.reshape(lead + (N_HEADS, per_head))
    w = jnp.pad(w, [(0, 0)] * len(lead) + [(0, 0), (0, HEAD_PAD - per_head)])
    return w.reshape(lead + (N_HEADS * HEAD_PAD,))


def _prep_weights(l, g_mix_norm, w_in, g_q_a, w_q_b, g_kv_a, w_kv_b, g_qn, g_qr, g_kn, g_kr,
                  lru_conv_w, lru_conv_b, w_rg, b_rg, w_ig, b_ig, lru_lambda, w_out, g_ffn_norm,
                  w_up, ffn_conv_w, ffn_conv_b, w_down):
    d_model = w_in.shape[1]
    q_lora = w_q_b.shape[1]
    kv_lora = w_kv_b.shape[1]
    row = lambda a: a.reshape(1, -1)
    o_kr = q_lora + kv_lora
    wi = w_in[l].astype(BF16)
    w_kr = jnp.pad(wi[:, o_kr:o_kr + ROPE_DIM], ((0, 0), (NOPE_DIM, HEAD_PAD - QK_DIM)))
    w_in_p = jnp.concatenate([wi[:, :o_kr], w_kr, wi[:, o_kr + ROPE_DIM:]], axis=1)
    scale = QK_DIM ** -0.5 * LOG2_E
    gq = jnp.pad(jnp.concatenate([g_qn[l], g_qr[l]]) * scale, (0, HEAD_PAD - QK_DIM))
    g_rope = g_qr[l] * scale
    gqx = jnp.pad(jnp.concatenate([g_rope[HALF_ROPE:], g_rope[:HALF_ROPE]]), (NOPE_DIM, HEAD_PAD - QK_DIM))
    wq3 = w_q_b[l].reshape(q_lora, N_HEADS, QK_DIM)
    wqx = jnp.concatenate([jnp.zeros((q_lora, N_HEADS, NOPE_DIM), F32), wq3[:, :, NOPE_DIM + HALF_ROPE:],
                           wq3[:, :, NOPE_DIM:NOPE_DIM + HALF_ROPE]], axis=2).reshape(q_lora, N_HEADS * QK_DIM)
    gk = jnp.tile(jnp.pad(g_kn[l], (0, HEAD_PAD - NOPE_DIM)), N_HEADS)
    kv = w_kv_b[l].reshape(kv_lora, N_HEADS, NOPE_DIM + V_DIM)
    wk = _head_pad(kv[:, :, :NOPE_DIM].reshape(kv_lora, N_HEADS * NOPE_DIM), (NOPE_DIM,))
    wv = kv[:, :, NOPE_DIM:].reshape(kv_lora, N_HEADS * V_DIM)
    wkg = jnp.pad(jnp.transpose(kv[:, :, :NOPE_DIM], (1, 2, 0)) * g_kn[l][None, :, None],
                  ((0, 0), (0, HEAD_PAD - NOPE_DIM), (0, 0)))
    seg = jnp.pad(jnp.repeat(jnp.eye(N_HEADS, dtype=F32), NOPE_DIM, axis=0), ((0, 0), (0, LANES - N_HEADS)))
    return {
        "wkg": wkg.astype(BF16), "wkc": kv[:, :, :NOPE_DIM].reshape(kv_lora, -1).astype(BF16),
        "seg": seg.astype(BF16), "wv": wv.astype(BF16),
        "g_mix": row(g_mix_norm[l]), "w_in": w_in_p, "g_qa": row(g_q_a[l]),
        "wq": _head_pad(w_q_b[l], (NOPE_DIM, ROPE_DIM)).astype(BF16), "gq": row(gq), "gqx": row(gqx),
        "wqx": _head_pad(wqx, (NOPE_DIM, ROPE_DIM)).astype(BF16),
        "g_kva": row(g_kv_a[l]), "g_kr": row(jnp.pad(g_kr[l], (NOPE_DIM, HEAD_PAD - QK_DIM))),
        "wk": wk.astype(BF16), "gk": row(gk), "wvt": wv.T.astype(BF16),
        "cw": lru_conv_w[l], "cb": row(lru_conv_b[l]),
        "wlru": jnp.concatenate([w_rg[l], w_ig[l]], axis=-1).astype(BF16),
        "brg": row(b_rg[l]), "big": row(b_ig[l]), "lam": row(lru_lambda[l]),
        "w_out": w_out[l].astype(BF16), "g_ffn": row(g_ffn_norm[l]), "w_up": w_up[l].astype(BF16),
        "fw": ffn_conv_w[l], "fb": row(ffn_conv_b[l]), "w_down": w_down[l].astype(BF16),
    }


def _rope_tables(pos):
    inv = np.float32(ROPE_THETA) ** (-np.arange(0, ROPE_DIM, 2, dtype=np.float32) / np.float32(ROPE_DIM))
    ang = pos.astype(np.float32)[:, None] * inv[None, :]
    cos = np.cos(ang.astype(np.float64)).astype(np.float32)
    sin = np.sin(ang.astype(np.float64)).astype(np.float32)
    n = pos.shape[0]
    ones_lo = np.ones((n, NOPE_DIM), np.float32)
    zeros_lo = np.zeros((n, NOPE_DIM), np.float32)
    zeros_half = np.zeros((n, HALF_ROPE), np.float32)
    tail = np.zeros((n, HEAD_PAD - QK_DIM), np.float32)
    c = np.concatenate([ones_lo, cos, cos, tail], axis=1)
    s_lo = np.concatenate([zeros_lo, -sin, zeros_half, tail], axis=1)
    s_hi = np.concatenate([zeros_lo, zeros_half, sin, tail], axis=1)
    return jnp.asarray(c), jnp.asarray(s_lo), jnp.asarray(s_hi)


def _expand_matrix(t):
    one_part = jnp.pad(jnp.repeat(jnp.eye(N_HEADS, dtype=F32), t, axis=1), ((0, LANES - N_HEADS), (0, 0)))
    return jnp.concatenate([one_part, one_part], axis=0).astype(BF16)


def _front_pad_rows(a, hp):
    return jnp.pad(a, ((0, 0), (hp - a.shape[1], 0), (0, 0)))


def _layer_prompt(x, wts, *, rows, out_rows, attn_tq, attn_tk):
    bsz, t, d_model = x.shape
    lru_w = wts["cw"].shape[0]
    ffn_w = wts["fw"].shape[0]
    d_ff2 = wts["w_up"].shape[1]
    hp1 = _round_up(lru_w - 1, SUBLANES)
    hp2 = _round_up(ffn_w - 1, SUBLANES)
    rope = _rope_tables(np.arange(t, dtype=np.int32))
    q, k, vt, lat, kr, ga, rg, h_last, cst = _mixer_in(
        x, jnp.zeros((bsz, hp1, d_model), F32), jnp.zeros((bsz, 1, d_model), F32), rope, wts, rows=rows, nb=1)
    mixed = _attention(q, k, vt, ga, rg, tq=attn_tq, tk=attn_tk)
    y, fst = _mixer_out(x, mixed, jnp.zeros((bsz, hp2, d_ff2), F32), wts, rows=out_rows, nb=1)
    return y, (lat, kr, h_last[:, 0], cst[:, hp1 - (lru_w - 1):], fst[:, hp2 - (ffn_w - 1):])


def _layer_sample(x, past_lat, past_kr, h0, lru_buf, ffn_buf, wts, *, tk):
    bsz, t, d_model = x.shape
    past = past_lat.shape[1]
    lru_w = wts["cw"].shape[0]
    ffn_w = wts["fw"].shape[0]
    hp1 = _round_up((lru_w - 1) * bsz, SUBLANES)
    hp2 = _round_up((ffn_w - 1) * bsz, SUBLANES)
    rows = t * bsz

    def to_tm(a):
        return jnp.swapaxes(a, 0, 1).reshape(1, a.shape[1] * bsz, a.shape[2])

    def from_tm(a):
        return jnp.swapaxes(a.reshape(a.shape[1] // bsz, bsz, a.shape[2]), 0, 1)

    rope = _rope_tables(np.repeat(past + np.arange(t, dtype=np.int32), bsz))
    q, k, vt, lat, kr, ga, rg, h_last, cst = _mixer_in(
        to_tm(x), _front_pad_rows(to_tm(lru_buf), hp1), h0[None], rope, wts, rows=rows, nb=bsz)

    assert (past % CHUNK) + t <= CHUNK, "cache attention assumes all keys visible to all queries"
    n_pad = _round_up(t, LANES)
    pad_rows = lambda a: jnp.pad(from_tm(a), ((0, 0), (0, n_pad - t), (0, 0)))
    cache_wts = dict(wts, expand=_expand_matrix(t))
    mixed = _cache_attention(from_tm(q), past_lat, past_kr, pad_rows(lat), pad_rows(kr),
                             from_tm(ga), from_tm(rg), cache_wts, tk=tk, n_new=t)

    y, fst = _mixer_out(to_tm(x), to_tm(mixed), _front_pad_rows(to_tm(ffn_buf), hp2), wts, rows=rows, nb=bsz)
    states = (from_tm(lat), from_tm(kr), h_last[0],
              from_tm(cst[:, hp1 - (lru_w - 1) * bsz:]), from_tm(fst[:, hp2 - (ffn_w - 1) * bsz:]))
    return from_tm(y), states


def kernel(x_prompt, x_sample, cache_kv_latent, cache_k_rope, state_lru_h, state_lru_conv, state_ffn_conv,
           g_mix_norm, w_in, g_q_a, w_q_b, g_kv_a, w_kv_b, g_qn, g_qr, g_kn, g_kr, lru_conv_w, lru_conv_b,
           w_rg, b_rg, w_ig, b_ig, lru_lambda, w_out, g_ffn_norm, w_up, ffn_conv_w, ffn_conv_b, w_down):
    depth = w_in.shape[0]
    yp, ys = x_prompt, x_sample
    p_states, s_states = [], []
    for l in range(depth):
        wts = _prep_weights(l, g_mix_norm, w_in, g_q_a, w_q_b, g_kv_a, w_kv_b, g_qn, g_qr, g_kn, g_kr,
                            lru_conv_w, lru_conv_b, w_rg, b_rg, w_ig, b_ig, lru_lambda, w_out, g_ffn_norm,
                            w_up, ffn_conv_w, ffn_conv_b, w_down)
        yp, st_p = _layer_prompt(yp, wts, rows=512, out_rows=512, attn_tq=1024, attn_tk=512)
        ys, st_s = _layer_sample(ys, cache_kv_latent[l], cache_k_rope[l], state_lru_h[l],
                                 state_lru_conv[l], state_ffn_conv[l], wts, tk=512)
        p_states.append(st_p)
        s_states.append(st_s)
    p_out = [jnp.stack([st[j] for st in p_states], axis=0) for j in range(5)]
    s_out = [jnp.stack([st[j] for st in s_states], axis=0) for j in range(5)]
    return (yp, ys, *p_out, *s_out)
```

```python
import functools

import jax
import jax.numpy as jnp
import numpy as np
from jax import lax
from jax.experimental import pallas as pl
from jax.experimental.pallas import tpu as pltpu

CHUNK = 64
CHUNK_SHIFT = CHUNK.bit_length() - 1
assert CHUNK == 1 << CHUNK_SHIFT
N_HEADS = 16
NOPE_DIM = 64
ROPE_DIM = 32
V_DIM = 64
QK_DIM = NOPE_DIM + ROPE_DIM
ROPE_THETA = 10000.0
RG_C = 8.0
EPS = 1e-6
LOG2_E = 1.4426950408889634

LANES = 128
SUBLANES = 8
BF16_SUBLANES = 16
MXU_DIM = 256
VMEM_BYTES_V7X = 64 * 1024 * 1024

HEAD_PAD = LANES
SLAB_SUB_ROWS = MXU_DIM
HALF_ROPE = ROPE_DIM // 2
F32 = jnp.float32
BF16 = jnp.bfloat16


def _round_up(n, m):
    return (n + m - 1) // m * m


def _vmem_limit(nbytes):
    return int(min(2 * nbytes, VMEM_BYTES_V7X - 8 * 1024 * 1024))


def _const_spec(shape):
    nd = len(shape)
    return pl.BlockSpec(shape, lambda *_: (0,) * nd, pipeline_mode=pl.Buffered(1))


def _dot(a, b):
    return jnp.dot(a, b, preferred_element_type=F32)


def _dot_nt(a, b):
    return lax.dot_general(a, b, (((1,), (1,)), ((), ())), preferred_element_type=F32)


def _sigmoid(x):
    return 0.5 * jnp.tanh(0.5 * x) + 0.5


def _rms_rows(x, n):
    return lax.rsqrt(jnp.sum(x * x, axis=-1, keepdims=True) * (1.0 / n) + EPS)


def _rope_head(x, c, s_lo, s_hi):
    return (x * c + pltpu.roll(x, HALF_ROPE, 1) * s_hi
            + pltpu.roll(x, HEAD_PAD - HALF_ROPE, 1) * s_lo)


def _shift_rows(x, n, fill):
    rows, cols = x.shape
    if n % SUBLANES == 0:
        return jnp.concatenate([jnp.full((n, cols), fill, x.dtype), x[:rows - n]], axis=0)
    rolled = pltpu.roll(x, n, 0)
    row = lax.broadcasted_iota(jnp.int32, x.shape, 0)
    return jnp.where(row >= n, rolled, fill)


def _linear_scan(a, b, nb):
    rows = a.shape[0]
    s = nb
    while s < rows:
        b = a * _shift_rows(b, s, 0.0) + b
        if 2 * s < rows:
            a = a * _shift_rows(a, s, 1.0)
        s *= 2
    return b


def _mixer_in_kernel(*refs, rows, sub, **static):
    hist_ref, h0_ref = refs[1], refs[2]
    xh_ref, hprev_ref = refs[-2], refs[-1]

    @pl.when(pl.program_id(1) == 0)
    def _():
        xh_ref[...] = hist_ref[...]
        hprev_ref[...] = h0_ref[...]

    waiting = [_mixer_in_stages(*refs, r0=r0, sub=sub, **static) for r0 in range(0, rows, sub)]
    running = []
    while waiting or running:
        if waiting:
            running.append(waiting.pop(0))
        for stages in list(running):
            if next(stages, "done") == "done":
                running.remove(stages)


def _mixer_in_stages(x_ref, hist_ref, h0_ref, cos_ref, slo_ref, shi_ref,
                     g_mix_ref, w_in_ref, g_qa_ref, wq_ref, wqx_ref, gq_ref, gqx_ref,
                     g_kva_ref, g_kr_ref,
                     wk_ref, gk_ref, wvt_ref, cw_ref, cb_ref, wlru_ref, brg_ref, big_ref, lam_ref, *rest,
                     r0, sub, nb, hp, d_model, q_lora, kv_lora, conv_w):
    if nb == 1:
        perm_ref, *rest = rest
    q_ref, k_ref, vt_ref, lat_ref, kr_ref, ga_ref, rg_ref, hl_ref, cst_ref, xh_ref, hprev_ref = rest
    rs = slice(r0, r0 + sub)
    rows = sub

    o_kv = q_lora
    o_kr = o_kv + kv_lora
    o_u = o_kr + HEAD_PAD
    o_ga = o_u + d_model
    o_gb = o_ga + d_model
    cos = cos_ref[rs, :]
    s_lo = slo_ref[rs, :]
    s_hi = shi_ref[rs, :]

    x = x_ref[rs, :]
    xn = (x * _rms_rows(x, d_model) * g_mix_ref[...]).astype(BF16)
    yield

    xs = _dot(perm_ref[0], xn).astype(BF16) if nb == 1 else xn

    def in_proj(lhs, c0, c1):
        return _dot(lhs, w_in_ref[:, c0:c1])

    z_lat = in_proj(xn, 0, o_u)
    u = in_proj(xs, o_u, o_ga)
    yield

    cq = z_lat[:, 0:o_kv]
    ckv = z_lat[:, o_kv:o_kr]
    krb = z_lat[:, o_kr:o_u]
    lat = ckv * _rms_rows(ckv, kv_lora) * g_kva_ref[...]
    lat_ref[rs, :] = lat
    latb = lat.astype(BF16)
    kr = _rope_head(krb * _rms_rows(krb, ROPE_DIM) * g_kr_ref[...], cos, s_lo, s_hi)
    kr_ref[rs, :] = kr[:, NOPE_DIM:NOPE_DIM + ROPE_DIM]
    cqn = (cq * _rms_rows(cq, q_lora) * g_qa_ref[...]).astype(BF16)
    yield

    qp = _dot(cqn, wq_ref[...])
    qp2 = _dot(cqn, wqx_ref[...])
    yield

    def q_heads():
        cg = cos * gq_ref[...]
        sg = (s_lo + s_hi) * gqx_ref[...]
        is_nope = lax.broadcasted_iota(jnp.int32, (rows, HEAD_PAD), 1) < NOPE_DIM
        for h in range(N_HEADS):
            sl = slice(h * HEAD_PAD, (h + 1) * HEAD_PAD)
            qh = qp[:, sl]
            sq = qh * qh
            r_n = lax.rsqrt(jnp.sum(jnp.where(is_nope, sq, 0.0), axis=-1, keepdims=True)
                            * (1.0 / NOPE_DIM) + EPS)
            r_r = lax.rsqrt(jnp.sum(jnp.where(is_nope, 0.0, sq), axis=-1, keepdims=True)
                            * (1.0 / ROPE_DIM) + EPS)
            q_ref[rs, sl] = (jnp.where(is_nope, r_n, r_r) * (qh * cg + qp2[:, sl] * sg)).astype(BF16)

    def k_heads(kp):
        for h in range(N_HEADS):
            sl = slice(h * HEAD_PAD, (h + 1) * HEAD_PAD)
            kh = kp[:, sl]
            k_ref[rs, sl] = (kh * _rms_rows(kh, NOPE_DIM) * gk_ref[:, sl] + kr).astype(BF16)

    hgt = SUBLANES if nb == 1 else nb
    steps = rows // hgt
    x_slabs = {g: u[g * hgt:(g + 1) * hgt] for g in range(steps)}
    first_run = lax.broadcasted_iota(jnp.int32, (hgt, d_model), 0) == 0
    for j in range(1, conv_w):
        if nb == 1:
            x_slabs[-j] = jnp.where(first_run, xh_ref[hp - j:hp - j + 1, :], pltpu.roll(x_slabs[steps - j], 1, 0))
        else:
            x_slabs[-j] = xh_ref[hp - j * nb:hp - (j - 1) * nb, :]
    uc_slabs = []
    for g in range(steps):
        acc = cb_ref[...] + cw_ref[conv_w - 1:conv_w, :] * x_slabs[g]
        for j in range(1, conv_w):
            acc = acc + cw_ref[conv_w - 1 - j:conv_w - j, :] * x_slabs[g - j]
        uc_slabs.append(acc)
    u_c = jnp.concatenate(uc_slabs, axis=0)
    if nb == 1:
        row = lax.broadcasted_iota(jnp.int32, (hp, d_model), 0)
        tail = jnp.zeros((hp, d_model), F32)
        for j in range(1, conv_w):
            last = jnp.broadcast_to(x_slabs[steps - j][hgt - 1:hgt], (hp, d_model))
            tail = jnp.where(row == hp - j, last, tail)
    else:
        tail = u[rows - hp:rows]
    cst_ref[...] = tail
    xh_ref[...] = tail
    yield

    n_blocks, blk_w, _ = wlru_ref.shape
    gate_proj = [_dot(u_c[:, n * blk_w:(n + 1) * blk_w].astype(BF16), wlru_ref[n])
                 for n in range(n_blocks)]

    kp = _dot(latb, wk_ref[...])
    vt_ref[:, rs] = _dot_nt(wvt_ref[...], latb).astype(BF16)
    gate_a = in_proj(xn, o_ga, o_gb)
    gate_b = in_proj(xs, o_gb, o_gb + d_model)
    yield

    lam = lam_ref[...]
    softplus_neg = jnp.maximum(-lam, 0.0) + jnp.log1p(jnp.exp(-jnp.abs(lam)))
    a_parts, b_parts = [], []
    for n in range(n_blocks):
        sl = slice(n * blk_w, (n + 1) * blk_w)
        ucn = u_c[:, sl]
        g = gate_proj[n]
        r = _sigmoid(g[:, 0:blk_w] + brg_ref[:, sl])
        i = _sigmoid(g[:, blk_w:2 * blk_w] + big_ref[:, sl])
        log_a = (-RG_C * r) * softplus_neg[:, sl]
        a_n = jnp.exp(log_a)
        a_parts.append(a_n)
        b_parts.append(jnp.sqrt(-jnp.tanh(log_a) * (1.0 + a_n * a_n)) * (i * ucn))
    a = jnp.concatenate(a_parts, axis=1)
    b = jnp.concatenate(b_parts, axis=1)

    a_g = a[0:hgt]
    h_in = a_g * hprev_ref[...]
    if nb == 1:
        h_in = jnp.where(lax.broadcasted_iota(jnp.int32, h_in.shape, 0) == 0, h_in, 0.0)
    h = b[0:hgt] + h_in
    a_run = a_g
    h_slabs, a_slabs = [h], [a_run]
    for g in range(1, steps):
        a_g = a[g * hgt:(g + 1) * hgt]
        h = a_g * h + b[g * hgt:(g + 1) * hgt]
        h_slabs.append(h)
        if nb == 1:
            a_run = a_g * a_run
            a_slabs.append(a_run)
    if nb == 1:
        run_end = _linear_scan(a_run, h, 1)
        run_in = _shift_rows(run_end, 1, 0.0)
        h_slabs = [h_g + a_r * run_in for h_g, a_r in zip(h_slabs, a_slabs)]
        h_last = run_end[hgt - 1:hgt]
    else:
        h_last = h_slabs[-1]
    hprev_ref[...] = h_last
    hl_ref[...] = h_last
    rg = (_sigmoid(gate_b) * jnp.concatenate(h_slabs, axis=0)).astype(BF16)
    if nb == 1:
        rg = _dot(perm_ref[1], rg).astype(BF16)
    rg_ref[rs, :] = rg
    yield

    q_heads()
    k_heads(kp)
    ga_ref[rs, :] = _sigmoid(gate_a).astype(BF16)


def _slab_permutation(rows):
    steps = rows // SUBLANES
    slab_row = jnp.arange(rows)
    time = (slab_row % SUBLANES) * steps + slab_row // SUBLANES
    gather = (time[:, None] == jnp.arange(rows)[None, :]).astype(BF16)
    return jnp.stack([gather, gather.T])


def _mixer_in(x, hist, h0, rope, wts, *, rows, nb):
    groups, tg, d_model = x.shape
    hp = hist.shape[1]
    q_lora = wts["wq"].shape[0]
    kv_lora = wts["wk"].shape[0]
    conv_w = wts["cw"].shape[0]
    n_t = tg // rows
    qk_w = N_HEADS * HEAD_PAD
    v_w = N_HEADS * V_DIM

    def row_spec(width):
        return pl.BlockSpec((None, rows, width), lambda g, t: (g, t, 0))

    def group_spec(r, width):
        return pl.BlockSpec((None, r, width), lambda g, t: (g, 0, 0))

    tab_spec = pl.BlockSpec((rows, HEAD_PAD), lambda g, t: (t, 0))
    w_names = ["g_mix", "w_in", "g_qa", "wq", "wqx", "gq", "gqx", "g_kva", "g_kr", "wk", "gk", "wvt",
               "cw", "cb", "wlru", "brg", "big", "lam"]
    sub = min(rows, SLAB_SUB_ROWS) if nb == 1 else rows
    assert rows % sub == 0
    w_list = [wts[n] for n in w_names] + ([_slab_permutation(sub)] if nb == 1 else [])
    in_specs = ([row_spec(d_model), group_spec(hp, d_model), group_spec(nb, d_model),
                 tab_spec, tab_spec, tab_spec] + [_const_spec(w.shape) for w in w_list])
    out_shape = (
        jax.ShapeDtypeStruct((groups, tg, qk_w), BF16),
        jax.ShapeDtypeStruct((groups, tg, qk_w), BF16),
        jax.ShapeDtypeStruct((groups, n_t, v_w, rows), BF16),
        jax.ShapeDtypeStruct((groups, tg, kv_lora), F32),
        jax.ShapeDtypeStruct((groups, tg, ROPE_DIM), F32),
        jax.ShapeDtypeStruct((groups, tg, d_model), BF16),
        jax.ShapeDtypeStruct((groups, tg, d_model), BF16),
        jax.ShapeDtypeStruct((groups, nb, d_model), F32),
        jax.ShapeDtypeStruct((groups, hp, d_model), F32),
    )
    vt_spec = pl.BlockSpec((None, None, v_w, rows), lambda g, t: (g, t, 0, 0))
    out_specs = (row_spec(qk_w), row_spec(qk_w), vt_spec, row_spec(kv_lora), row_spec(ROPE_DIM),
                 row_spec(d_model), row_spec(d_model), group_spec(nb, d_model), group_spec(hp, d_model))
    w_bytes = sum(w.size * w.dtype.itemsize for w in w_list)
    io_bytes = 2 * rows * (4 * d_model + 2 * (2 * qk_w + v_w + 2 * d_model) + 4 * (kv_lora + LANES) + 12 * LANES)
    tmp_bytes = 4 * rows * (wts["w_in"].shape[1] + 2 * qk_w + 6 * d_model) + 4 * (hp + rows) * d_model
    kern = functools.partial(_mixer_in_kernel, rows=rows, sub=sub, nb=nb, hp=hp, d_model=d_model,
                             q_lora=q_lora, kv_lora=kv_lora, conv_w=conv_w)
    return pl.pallas_call(
        kern,
        grid=(groups, n_t),
        in_specs=in_specs,
        out_specs=out_specs,
        out_shape=out_shape,
        scratch_shapes=[pltpu.VMEM((hp, d_model), F32), pltpu.VMEM((nb, d_model), F32)],
        compiler_params=pltpu.CompilerParams(
            dimension_semantics=("parallel", "arbitrary"),
            vmem_limit_bytes=_vmem_limit(w_bytes + io_bytes + tmp_bytes)),
        name="mixer_in",
    )(x, hist, h0, *rope, *w_list)


def _attn_kernel(q_ref, k_ref, vt_ref, ga_ref, rg_ref, o_ref, m_ref, acc_ref, carry_ref, *, tq, tk, cq, vblk):
    qi = pl.program_id(2)
    m_ref[...] = jnp.full(m_ref.shape, -jnp.inf, F32)
    acc_ref[...] = jnp.zeros(acc_ref.shape, F32)

    n_sub = tq // tk

    def wide(kj, b):
        return [(kj, hh, b * tk, tk, tk, None) for hh in range(2)]

    def narrow(kj, b):
        return [(kj, hh, b * tk + q0, cq, q0 + cq, q0) for hh in range(2) for q0 in range(0, tk, cq)]

    def scores(kj, hh, q0, qw, n_keys):
        lanes = slice(hh * HEAD_PAD, (hh + 1) * HEAD_PAD)
        k0 = pl.multiple_of(kj * tk, tk)
        return _dot_nt(k_ref[pl.ds(k0, n_keys), lanes], q_ref[q0:q0 + qw, lanes])

    def diag_mask(s, q0):
        k_chunk = lax.broadcasted_iota(jnp.int32, s.shape, 0) >> CHUNK_SHIFT
        q_chunk = (q0 + lax.broadcasted_iota(jnp.int32, s.shape, 1)) >> CHUNK_SHIFT
        return jnp.where(k_chunk <= q_chunk, s, -jnp.inf)

    def softmax(hh, q0, qw, s):
        qs = slice(q0, q0 + qw)
        m_prev = m_ref[hh, :, qs]
        m_new = jnp.maximum(m_prev, jnp.max(s, axis=0, keepdims=True))
        m_ref[hh, :, qs] = m_new
        return jnp.exp2(s - m_new).astype(BF16), jnp.exp2(m_prev - m_new)

    def values(kj, hh, q0, qw, pb, alpha):
        qs = slice(q0, q0 + qw)
        pv = None
        n_keys = pb.shape[0]
        piece = min(vblk, n_keys)
        for j0 in range(0, n_keys, piece):
            vt = vt_ref[kj * (tk // vblk) + j0 // vblk, hh * V_DIM:(hh + 1) * V_DIM, pl.ds(j0 % vblk, piece)]
            ones_rows = jnp.ones((acc_ref.shape[1] - V_DIM, piece), BF16)
            part = _dot(jnp.concatenate([vt, ones_rows], axis=0), pb[j0:j0 + piece])
            pv = part if pv is None else pv + part
        acc_ref[hh, :, qs] = alpha * acc_ref[hh, :, qs] + pv

    def run_chains(items, next_tile):
        _, _, _, qw0, n_keys0, _ = items[0]
        s_next = carry_ref[0:n_keys0, 0:qw0]
        prev = None
        for i, (kj, hh, q0, qw, n_keys, mask_q0) in enumerate(items):
            s_cur = s_next
            if i + 1 < len(items):
                s_next = scores(*items[i + 1][:5])
            elif next_tile is not None:
                carry_ref[...] = scores(*wide(next_tile, 0)[0][:5])
            if mask_q0 is not None:
                s_cur = diag_mask(s_cur, mask_q0)
            cur = softmax(hh, q0, qw, s_cur)
            if prev is not None:
                values(*items[i - 1][:4], *prev)
            prev = cur
        values(*items[-1][:4], *prev)

    carry_ref[...] = scores(*wide(0, 0)[0][:5])

    @pl.loop(0, qi)
    def _(j):
        run_chains([c for d in range(n_sub) for b in range(n_sub) for c in wide(n_sub * j + d, b)],
                   n_sub * (j + 1))

    diag_items = []
    for d in range(n_sub):
        kj = n_sub * qi + d
        diag_items += narrow(kj, d) + [c for b in range(d + 1, n_sub) for c in wide(kj, b)]
    run_chains(diag_items, None)

    attn_t = jnp.concatenate([acc_ref[hh, 0:V_DIM, :] / acc_ref[hh, V_DIM:V_DIM + 1, :] for hh in range(2)],
                             axis=0)
    o_ref[...] = (ga_ref[...].astype(F32) * attn_t.T + rg_ref[...].astype(F32)).astype(BF16)


def _attention(q, k, vt, ga, rg, *, tq, tk):
    bsz, t, _ = q.shape
    n_vb, _, vblk = vt.shape[1:]
    assert LANES == 2 * V_DIM and N_HEADS % 2 == 0
    assert k.shape[1] == t == n_vb * vblk and tk % vblk == 0 and t % tq == 0 and tq % tk == 0
    assert tk % CHUNK == 0 and tk % MXU_DIM == 0
    n_hp = N_HEADS // 2
    qk_blk = 2 * HEAD_PAD

    def q_index(b, p, qi):
        return (b, qi, p)

    kern = functools.partial(_attn_kernel, tq=tq, tk=tk, cq=MXU_DIM, vblk=vblk)
    blk_bytes = 2 * 2 * (tq * qk_blk + t * qk_blk + t * LANES + 3 * tq * LANES)
    tmp_bytes = 4 * 8 * tk * tk + 4 * 4 * tq * LANES
    return pl.pallas_call(
        kern,
        grid=(bsz, n_hp, t // tq),
        in_specs=[pl.BlockSpec((None, tq, qk_blk), q_index),
                  pl.BlockSpec((None, t, qk_blk), lambda b, p, qi: (b, 0, p)),
                  pl.BlockSpec((None, n_vb, LANES, vblk), lambda b, p, qi: (b, 0, p, 0)),
                  pl.BlockSpec((None, tq, LANES), q_index),
                  pl.BlockSpec((None, tq, LANES), q_index)],
        out_specs=pl.BlockSpec((None, tq, LANES), q_index),
        out_shape=jax.ShapeDtypeStruct((bsz, t, N_HEADS * V_DIM), BF16),
        scratch_shapes=[pltpu.VMEM((2, 1, tq), F32),
                        pltpu.VMEM((2, V_DIM + BF16_SUBLANES, tq), F32),
                        pltpu.VMEM((tk, tk), F32)],
        compiler_params=pltpu.CompilerParams(
            dimension_semantics=("parallel", "parallel", "arbitrary"),
            vmem_limit_bytes=_vmem_limit(blk_bytes + tmp_bytes)),
        name="attention",
    )(q, k, vt, ga, rg)


def _cache_attn_kernel(q_ref, latc_ref, krc_ref, latn_ref, krn_ref, ga_ref, rg_ref,
                       wkg_ref, wkc_ref, seg_ref, expand_ref, wv_ref, o_ref,
                       qabs_ref, qr_ref, m_ref, acc_ref, *, t, tk, n_new):
    past = latc_ref.shape[0]
    hq = N_HEADS * t
    for h in range(N_HEADS):
        qh = q_ref[:, h * HEAD_PAD:(h + 1) * HEAD_PAD]
        qabs_ref[h * t:(h + 1) * t, :] = _dot(qh, wkg_ref[h]).astype(BF16)
        qr_ref[h * t:(h + 1) * t, :] = qh[:, NOPE_DIM:QK_DIM]
    m_ref[...] = jnp.full(m_ref.shape, -jnp.inf, F32)
    acc_ref[...] = jnp.zeros(acc_ref.shape, F32)
    kv_lora = latc_ref.shape[1]

    def scores(lat, kr, n_valid):
        n = lat.shape[0]
        latb = lat.astype(BF16)
        kvk = _dot(latb, wkc_ref[...])
        ssum = _dot((kvk * kvk).astype(BF16), seg_ref[...])
        r = lax.rsqrt(ssum * (1.0 / NOPE_DIM) + EPS)
        r_hi = r.astype(BF16)
        r_lo = (r - r_hi.astype(F32)).astype(BF16)
        r_cols = _dot(jnp.concatenate([r_hi, r_lo], axis=1), expand_ref[...])
        s = _dot_nt(latb, qabs_ref[...]) * r_cols + _dot_nt(kr.astype(BF16), qr_ref[...])
        if n_valid < n:
            s = jnp.where(lax.broadcasted_iota(jnp.int32, (n, hq), 0) < n_valid, s, -jnp.inf)
        lat_t = jnp.concatenate([lat.T.astype(BF16), jnp.ones((acc_ref.shape[0] - kv_lora, n), BF16)], axis=0)
        return s, lat_t

    def accumulate(s, lat_t):
        m_prev = m_ref[...]
        m_new = jnp.maximum(m_prev, jnp.max(s, axis=0, keepdims=True))
        m_ref[...] = m_new
        p = jnp.exp2(s - m_new).astype(BF16)
        acc_ref[...] = jnp.exp2(m_prev - m_new) * acc_ref[...] + _dot(lat_t, p)

    def tile_scores(j):
        if j < past // tk:
            return scores(latc_ref[j * tk:(j + 1) * tk, :], krc_ref[j * tk:(j + 1) * tk, :], tk)
        return scores(latn_ref[...], krn_ref[...], n_new)

    n_tiles = past // tk + 1
    cur = tile_scores(0)
    for j in range(n_tiles):
        nxt = tile_scores(j + 1) if j + 1 < n_tiles else None
        accumulate(*cur)
        cur = nxt

    ctx = (acc_ref[0:kv_lora, :] / acc_ref[kv_lora:kv_lora + 1, :]).T.astype(BF16)
    first_head = lax.broadcasted_iota(jnp.int32, (t, LANES), 1) < V_DIM
    for pair in range(N_HEADS // 2):
        cols = slice(pair * LANES, (pair + 1) * LANES)
        wv = wv_ref[:, cols]
        a0 = _dot(ctx[(2 * pair) * t:(2 * pair + 1) * t], wv)
        a1 = _dot(ctx[(2 * pair + 1) * t:(2 * pair + 2) * t], wv)
        attn = jnp.where(first_head, a0, a1)
        o_ref[:, cols] = (ga_ref[:, cols].astype(F32) * attn + rg_ref[:, cols].astype(F32)).astype(BF16)


def _cache_attention(q, lat_cache, kr_cache, lat_new, kr_new, ga, rg, wts, *, tk, n_new):
    bsz, t, qk_w = q.shape
    past, kv_lora = lat_cache.shape[1:]
    n_pad = lat_new.shape[1]
    d_model = ga.shape[2]
    hq = N_HEADS * t
    assert past % tk == 0 and LANES == 2 * V_DIM
    w_list = [wts["wkg"], wts["wkc"], wts["seg"], wts["expand"], wts["wv"]]
    w_bytes = sum(w.size * w.dtype.itemsize for w in w_list)
    blk_bytes = 2 * (4 * (past + n_pad) * (kv_lora + LANES) + 2 * t * (qk_w + 3 * d_model))
    tmp_bytes = 4 * tk * (2 * N_HEADS * NOPE_DIM + 4 * hq)

    def batch_spec(r, width):
        return pl.BlockSpec((None, r, width), lambda b: (b, 0, 0))

    kern = functools.partial(_cache_attn_kernel, t=t, tk=tk, n_new=n_new)
    return pl.pallas_call(
        kern,
        grid=(bsz,),
        in_specs=[batch_spec(t, qk_w), batch_spec(past, kv_lora), batch_spec(past, ROPE_DIM),
                  batch_spec(n_pad, kv_lora), batch_spec(n_pad, ROPE_DIM),
                  batch_spec(t, d_model), batch_spec(t, d_model)] + [_const_spec(w.shape) for w in w_list],
        out_specs=batch_spec(t, d_model),
        out_shape=jax.ShapeDtypeStruct((bsz, t, d_model), BF16),
        scratch_shapes=[pltpu.VMEM((hq, kv_lora), BF16), pltpu.VMEM((hq, ROPE_DIM), BF16),
                        pltpu.VMEM((1, hq), F32), pltpu.VMEM((kv_lora + BF16_SUBLANES, hq), F32)],
        compiler_params=pltpu.CompilerParams(
            dimension_semantics=("parallel",),
            vmem_limit_bytes=_vmem_limit(w_bytes + blk_bytes + tmp_bytes)),
        name="cache_attention",
    )(q, lat_cache, kr_cache, lat_new, kr_new, ga, rg, *w_list)


def _mixer_out_kernel(x_ref, mix_ref, hist_ref, w_out_ref, g_ffn_ref, w_up_ref, fw_ref, fb_ref, w_down_ref,
                      y_ref, fst_ref, upc_ref, *, rows, nb, hp, d_model, d_ff, conv_w, col_blk):
    t = pl.program_id(1)

    @pl.when(t == 0)
    def _():
        upc_ref[0:hp, :] = hist_ref[...]

    x1 = x_ref[...] + _dot(mix_ref[...], w_out_ref[...])
    xn = (x1 * _rms_rows(x1, d_model) * g_ffn_ref[...]).astype(BF16)

    def up_proj(c):
        for c0 in (c * col_blk, d_ff + c * col_blk):
            sl = slice(c0, c0 + col_blk)
            upc_ref[hp:hp + rows, sl] = _dot(xn, w_up_ref[:, sl])

    def conv_cols(c0):
        sl = slice(c0, c0 + col_blk)
        out = fb_ref[:, sl] + fw_ref[conv_w - 1:conv_w, sl] * upc_ref[hp:hp + rows, sl]
        for j in range(1, conv_w):
            out = out + fw_ref[conv_w - 1 - j:conv_w - j, sl] * upc_ref[hp - j * nb:hp - j * nb + rows, sl]
        return out

    def gated(c):
        gate = conv_cols(c * col_blk)
        val = conv_cols(d_ff + c * col_blk)
        return (gate * _sigmoid(gate) * val).astype(BF16)

    def down_proj(c, hmid):
        return _dot(hmid, w_down_ref[c * col_blk:(c + 1) * col_blk, :])

    n_chunks = d_ff // col_blk
    y = x1
    up_proj(0)
    prev = None
    for c in range(n_chunks):
        if c + 1 < n_chunks:
            up_proj(c + 1)
        hmid = gated(c)
        if prev is not None:
            y = y + down_proj(c - 1, prev)
        prev = hmid
    y_ref[...] = y + down_proj(n_chunks - 1, prev)

    tail = upc_ref[rows:rows + hp, :]
    fst_ref[...] = tail
    upc_ref[0:hp, :] = tail


def _mixer_out(x, mixed, hist, wts, *, rows, nb):
    groups, tg, d_model = x.shape
    hp = hist.shape[1]
    d_ff = wts["w_down"].shape[0]
    conv_w = wts["fw"].shape[0]
    n_t = tg // rows
    col_blk = MXU_DIM

    def row_spec(width):
        return pl.BlockSpec((None, rows, width), lambda g, t: (g, t, 0))

    def group_spec(r, width):
        return pl.BlockSpec((None, r, width), lambda g, t: (g, 0, 0))

    w_names = ["w_out", "g_ffn", "w_up", "fw", "fb", "w_down"]
    w_list = [wts[n] for n in w_names]
    w_bytes = sum(w.size * w.dtype.itemsize for w in w_list)
    blk_bytes = 2 * rows * d_model * (4 + 2 + 4) + 4 * 4 * hp * 2 * d_ff
    tmp_bytes = 4 * (hp + rows) * 2 * d_ff + 4 * rows * (3 * d_model + 8 * col_blk)
    kern = functools.partial(_mixer_out_kernel, rows=rows, nb=nb, hp=hp, d_model=d_model, d_ff=d_ff,
                             conv_w=conv_w, col_blk=col_blk)
    return pl.pallas_call(
        kern,
        grid=(groups, n_t),
        in_specs=[row_spec(d_model), row_spec(d_model), group_spec(hp, 2 * d_ff)]
                 + [_const_spec(w.shape) for w in w_list],
        out_specs=(row_spec(d_model), group_spec(hp, 2 * d_ff)),
        out_shape=(jax.ShapeDtypeStruct((groups, tg, d_model), F32),
                   jax.ShapeDtypeStruct((groups, hp, 2 * d_ff), F32)),
        scratch_shapes=[pltpu.VMEM((hp + rows, 2 * d_ff), F32)],
        compiler_params=pltpu.CompilerParams(
            dimension_semantics=("parallel", "arbitrary"),
            vmem_limit_bytes=_vmem_limit(w_bytes + blk_bytes + tmp_bytes)),
        name="mixer_out",
    )(x, mixed, hist, *w_list)


def _head_pad(w, widths):
    lead = w.shape[:-1]
    per_head = sum(widths)
    w = w.reshape(lead + (N_HEADS, per_head))
    w = jnp.pad(w, [(0, 0)] * len(lead) + [(0, 0), (0, HEAD_PAD - per_head)])
    return w.reshape(lead + (N_HEADS * HEAD_PAD,))


def _prep_weights(l, g_mix_norm, w_in, g_q_a, w_q_b, g_kv_a, w_kv_b, g_qn, g_qr, g_kn, g_kr,
                  lru_conv_w, lru_conv_b, w_rg, b_rg, w_ig, b_ig, lru_lambda, w_out, g_ffn_norm,
                  w_up, ffn_conv_w, ffn_conv_b, w_down):
    d_model = w_in.shape[1]
    q_lora = w_q_b.shape[1]
    kv_lora = w_kv_b.shape[1]
    row = lambda a: a.reshape(1, -1)
    o_kr = q_lora + kv_lora
    wi = w_in[l].astype(BF16)
    w_kr = jnp.pad(wi[:, o_kr:o_kr + ROPE_DIM], ((0, 0), (NOPE_DIM, HEAD_PAD - QK_DIM)))
    w_in_p = jnp.concatenate([wi[:, :o_kr], w_kr, wi[:, o_kr + ROPE_DIM:]], axis=1)
    scale = QK_DIM ** -0.5 * LOG2_E
    gq = jnp.pad(jnp.concatenate([g_qn[l], g_qr[l]]) * scale, (0, HEAD_PAD - QK_DIM))
    g_rope = g_qr[l] * scale
    gqx = jnp.pad(jnp.concatenate([g_rope[HALF_ROPE:], g_rope[:HALF_ROPE]]), (NOPE_DIM, HEAD_PAD - QK_DIM))
    wq3 = w_q_b[l].reshape(q_lora, N_HEADS, QK_DIM)
    wqx = jnp.concatenate([jnp.zeros((q_lora, N_HEADS, NOPE_DIM), F32), wq3[:, :, NOPE_DIM + HALF_ROPE:],
                           wq3[:, :, NOPE_DIM:NOPE_DIM + HALF_ROPE]], axis=2).reshape(q_lora, N_HEADS * QK_DIM)
    gk = jnp.tile(jnp.pad(g_kn[l], (0, HEAD_PAD - NOPE_DIM)), N_HEADS)
    kv = w_kv_b[l].reshape(kv_lora, N_HEADS, NOPE_DIM + V_DIM)
    wk = _head_pad(kv[:, :, :NOPE_DIM].reshape(kv_lora, N_HEADS * NOPE_DIM), (NOPE_DIM,))
    wv = kv[:, :, NOPE_DIM:].reshape(kv_lora, N_HEADS * V_DIM)
    wkg = jnp.pad(jnp.transpose(kv[:, :, :NOPE_DIM], (1, 2, 0)) * g_kn[l][None, :, None],
                  ((0, 0), (0, HEAD_PAD - NOPE_DIM), (0, 0)))
    seg = jnp.pad(jnp.repeat(jnp.eye(N_HEADS, dtype=F32), NOPE_DIM, axis=0), ((0, 0), (0, LANES - N_HEADS)))
    return {
        "wkg": wkg.astype(BF16), "wkc": kv[:, :, :NOPE_DIM].reshape(kv_lora, -1).astype(BF16),
        "seg": seg.astype(BF16), "wv": wv.astype(BF16),
        "g_mix": row(g_mix_norm[l]), "w_in": w_in_p, "g_qa": row(g_q_a[l]),
        "wq": _head_pad(w_q_b[l], (NOPE_DIM, ROPE_DIM)).astype(BF16), "gq": row(gq), "gqx": row(gqx),
        "wqx": _head_pad(wqx, (NOPE_DIM, ROPE_DIM)).astype(BF16),
        "g_kva": row(g_kv_a[l]), "g_kr": row(jnp.pad(g_kr[l], (NOPE_DIM, HEAD_PAD - QK_DIM))),
        "wk": wk.astype(BF16), "gk": row(gk), "wvt": wv.T.astype(BF16),
        "cw": lru_conv_w[l], "cb": row(lru_conv_b[l]),
        "wlru": jnp.concatenate([w_rg[l], w_ig[l]], axis=-1).astype(BF16),
        "brg": row(b_rg[l]), "big": row(b_ig[l]), "lam": row(lru_lambda[l]),
        "w_out": w_out[l].astype(BF16), "g_ffn": row(g_ffn_norm[l]), "w_up": w_up[l].astype(BF16),
        "fw": ffn_conv_w[l], "fb": row(ffn_conv_b[l]), "w_down": w_down[l].astype(BF16),
    }


def _rope_tables(pos):
    inv = np.float32(ROPE_THETA) ** (-np.arange(0, ROPE_DIM, 2, dtype=np.float32) / np.float32(ROPE_DIM))
    ang = pos.astype(np.float32)[:, None] * inv[None, :]
    cos = np.cos(ang.astype(np.float64)).astype(np.float32)
    sin = np.sin(ang.astype(np.float64)).astype(np.float32)
    n = pos.shape[0]
    ones_lo = np.ones((n, NOPE_DIM), np.float32)
    zeros_lo = np.zeros((n, NOPE_DIM), np.float32)
    zeros_half = np.zeros((n, HALF_ROPE), np.float32)
    tail = np.zeros((n, HEAD_PAD - QK_DIM), np.float32)
    c = np.concatenate([ones_lo, cos, cos, tail], axis=1)
    s_lo = np.concatenate([zeros_lo, -sin, zeros_half, tail], axis=1)
    s_hi = np.concatenate([zeros_lo, zeros_half, sin, tail], axis=1)
    return jnp.asarray(c), jnp.asarray(s_lo), jnp.asarray(s_hi)


def _expand_matrix(t):
    one_part = jnp.pad(jnp.repeat(jnp.eye(N_HEADS, dtype=F32), t, axis=1), ((0, LANES - N_HEADS), (0, 0)))
    return jnp.concatenate([one_part, one_part], axis=0).astype(BF16)


def _front_pad_rows(a, hp):
    return jnp.pad(a, ((0, 0), (hp - a.shape[1], 0), (0, 0)))


def _layer_prompt(x, wts, *, rows, out_rows, attn_tq, attn_tk):
    bsz, t, d_model = x.shape
    lru_w = wts["cw"].shape[0]
    ffn_w = wts["fw"].shape[0]
    d_ff2 = wts["w_up"].shape[1]
    hp1 = _round_up(lru_w - 1, SUBLANES)
    hp2 = _round_up(ffn_w - 1, SUBLANES)
    rope = _rope_tables(np.arange(t, dtype=np.int32))
    q, k, vt, lat, kr, ga, rg, h_last, cst = _mixer_in(
        x, jnp.zeros((bsz, hp1, d_model), F32), jnp.zeros((bsz, 1, d_model), F32), rope, wts, rows=rows, nb=1)
    mixed = _attention(q, k, vt, ga, rg, tq=attn_tq, tk=attn_tk)
    y, fst = _mixer_out(x, mixed, jnp.zeros((bsz, hp2, d_ff2), F32), wts, rows=out_rows, nb=1)
    return y, (lat, kr, h_last[:, 0], cst[:, hp1 - (lru_w - 1):], fst[:, hp2 - (ffn_w - 1):])


def _layer_sample(x, past_lat, past_kr, h0, lru_buf, ffn_buf, wts, *, tk):
    bsz, t, d_model = x.shape
    past = past_lat.shape[1]
    lru_w = wts["cw"].shape[0]
    ffn_w = wts["fw"].shape[0]
    hp1 = _round_up((lru_w - 1) * bsz, SUBLANES)
    hp2 = _round_up((ffn_w - 1) * bsz, SUBLANES)
    rows = t * bsz

    def to_tm(a):
        return jnp.swapaxes(a, 0, 1).reshape(1, a.shape[1] * bsz, a.shape[2])

    def from_tm(a):
        return jnp.swapaxes(a.reshape(a.shape[1] // bsz, bsz, a.shape[2]), 0, 1)

    rope = _rope_tables(np.repeat(past + np.arange(t, dtype=np.int32), bsz))
    q, k, vt, lat, kr, ga, rg, h_last, cst = _mixer_in(
        to_tm(x), _front_pad_rows(to_tm(lru_buf), hp1), h0[None], rope, wts, rows=rows, nb=bsz)

    assert (past % CHUNK) + t <= CHUNK, "cache attention assumes all keys visible to all queries"
    n_pad = _round_up(t, LANES)
    pad_rows = lambda a: jnp.pad(from_tm(a), ((0, 0), (0, n_pad - t), (0, 0)))
    cache_wts = dict(wts, expand=_expand_matrix(t))
    mixed = _cache_attention(from_tm(q), past_lat, past_kr, pad_rows(lat), pad_rows(kr),
                             from_tm(ga), from_tm(rg), cache_wts, tk=tk, n_new=t)

    y, fst = _mixer_out(to_tm(x), to_tm(mixed), _front_pad_rows(to_tm(ffn_buf), hp2), wts, rows=rows, nb=bsz)
    states = (from_tm(lat), from_tm(kr), h_last[0],
              from_tm(cst[:, hp1 - (lru_w - 1) * bsz:]), from_tm(fst[:, hp2 - (ffn_w - 1) * bsz:]))
    return from_tm(y), states


def kernel(x_prompt, x_sample, cache_kv_latent, cache_k_rope, state_lru_h, state_lru_conv, state_ffn_conv,
           g_mix_norm, w_in, g_q_a, w_q_b, g_kv_a, w_kv_b, g_qn, g_qr, g_kn, g_kr, lru_conv_w, lru_conv_b,
           w_rg, b_rg, w_ig, b_ig, lru_lambda, w_out, g_ffn_norm, w_up, ffn_conv_w, ffn_conv_b, w_down):
    depth = w_in.shape[0]
    yp, ys = x_prompt, x_sample
    p_states, s_states = [], []
    for l in range(depth):
        wts = _prep_weights(l, g_mix_norm, w_in, g_q_a, w_q_b, g_kv_a, w_kv_b, g_qn, g_qr, g_kn, g_kr,
                            lru_conv_w, lru_conv_b, w_rg, b_rg, w_ig, b_ig, lru_lambda, w_out, g_ffn_norm,
                            w_up, ffn_conv_w, ffn_conv_b, w_down)
        yp, st_p = _layer_prompt(yp, wts, rows=512, out_rows=512, attn_tq=2048, attn_tk=512)
        ys, st_s = _layer_sample(ys, cache_kv_latent[l], cache_k_rope[l], state_lru_h[l],
                                 state_lru_conv[l], state_ffn_conv[l], wts, tk=1024)
        p_states.append(st_p)
        s_states.append(st_s)
    p_out = [jnp.stack([st[j] for st in p_states], axis=0) for j in range(5)]
    s_out = [jnp.stack([st[j] for st in s_states], axis=0) for j in range(5)]
    return (yp, ys, *p_out, *s_out)
```

```python
import functools

import jax
import jax.numpy as jnp
import numpy as np
from jax import lax
from jax.experimental import pallas as pl
from jax.experimental.pallas import tpu as pltpu

CHUNK = 64
CHUNK_SHIFT = CHUNK.bit_length() - 1
assert CHUNK == 1 << CHUNK_SHIFT
N_HEADS = 16
NOPE_DIM = 64
ROPE_DIM = 32
V_DIM = 64
QK_DIM = NOPE_DIM + ROPE_DIM
ROPE_THETA = 10000.0
RG_C = 8.0
EPS = 1e-6
LOG2_E = 1.4426950408889634

LANES = 128
SUBLANES = 8
BF16_SUBLANES = 16
MXU_DIM = 256
VMEM_BYTES_V7X = 64 * 1024 * 1024

HEAD_PAD = LANES
SLAB_SUB_ROWS = MXU_DIM
VALUE_SLAB_ROWS = 2 * MXU_DIM
HALF_ROPE = ROPE_DIM // 2
F32 = jnp.float32
BF16 = jnp.bfloat16


def _round_up(n, m):
    return (n + m - 1) // m * m


def _vmem_limit(nbytes):
    return int(min(2 * nbytes, VMEM_BYTES_V7X - 8 * 1024 * 1024))


def _const_spec(shape):
    nd = len(shape)
    return pl.BlockSpec(shape, lambda *_: (0,) * nd, pipeline_mode=pl.Buffered(1))


def _dot(a, b):
    return jnp.dot(a, b, preferred_element_type=F32)


def _dot_nt(a, b):
    return lax.dot_general(a, b, (((1,), (1,)), ((), ())), preferred_element_type=F32)


def _sigmoid(x):
    return 0.5 * jnp.tanh(0.5 * x) + 0.5


def _rms_rows(x, n):
    return lax.rsqrt(jnp.sum(x * x, axis=-1, keepdims=True) * (1.0 / n) + EPS)


def _rope_head(x, c, s_lo, s_hi):
    return (x * c + pltpu.roll(x, HALF_ROPE, 1) * s_hi
            + pltpu.roll(x, HEAD_PAD - HALF_ROPE, 1) * s_lo)


def _shift_rows(x, n, fill):
    rows, cols = x.shape
    if n % SUBLANES == 0:
        return jnp.concatenate([jnp.full((n, cols), fill, x.dtype), x[:rows - n]], axis=0)
    rolled = pltpu.roll(x, n, 0)
    row = lax.broadcasted_iota(jnp.int32, x.shape, 0)
    return jnp.where(row >= n, rolled, fill)


def _linear_scan(a, b, nb):
    rows = a.shape[0]
    s = nb
    while s < rows:
        b = a * _shift_rows(b, s, 0.0) + b
        if 2 * s < rows:
            a = a * _shift_rows(a, s, 1.0)
        s *= 2
    return b


def _mixer_in_kernel(*refs, rows, sub, **static):
    hist_ref, h0_ref = refs[1], refs[2]
    xh_ref, hprev_ref = refs[-2], refs[-1]

    @pl.when(pl.program_id(1) == 0)
    def _():
        xh_ref[...] = hist_ref[...]
        hprev_ref[...] = h0_ref[...]

    waiting = [_mixer_in_stages(*refs, r0=r0, sub=sub, **static) for r0 in range(0, rows, sub)]
    running = []
    while waiting or running:
        if waiting:
            running.append(waiting.pop(0))
        for stages in list(running):
            if next(stages, "done") == "done":
                running.remove(stages)


def _mixer_in_stages(x_ref, hist_ref, h0_ref, cos_ref, slo_ref, shi_ref,
                     g_mix_ref, w_in_ref, g_qa_ref, wq_ref, wqx_ref, gq_ref, gqx_ref,
                     g_kva_ref, g_kr_ref,
                     wk_ref, gk_ref, wvt_ref, cw_ref, cb_ref, wlru_ref, brg_ref, big_ref, lam_ref, *rest,
                     r0, sub, nb, hp, d_model, q_lora, kv_lora, conv_w):
    if nb == 1:
        perm_ref, *rest = rest
    q_ref, k_ref, vt_ref, lat_ref, kr_ref, ga_ref, rg_ref, hl_ref, cst_ref, xh_ref, hprev_ref = rest
    rs = slice(r0, r0 + sub)
    rows = sub

    o_kv = q_lora
    o_kr = o_kv + kv_lora
    o_u = o_kr + HEAD_PAD
    o_ga = o_u + d_model
    o_gb = o_ga + d_model
    cos = cos_ref[rs, :]
    s_lo = slo_ref[rs, :]
    s_hi = shi_ref[rs, :]

    x = x_ref[rs, :]
    xn = (x * _rms_rows(x, d_model) * g_mix_ref[...]).astype(BF16)
    yield

    xs = _dot(perm_ref[0], xn).astype(BF16) if nb == 1 else xn

    def in_proj(lhs, c0, c1):
        return _dot(lhs, w_in_ref[:, c0:c1])

    z_lat = in_proj(xn, 0, o_u)
    u = in_proj(xs, o_u, o_ga)
    yield

    cq = z_lat[:, 0:o_kv]
    ckv = z_lat[:, o_kv:o_kr]
    krb = z_lat[:, o_kr:o_u]
    lat = ckv * _rms_rows(ckv, kv_lora) * g_kva_ref[...]
    lat_ref[rs, :] = lat
    latb = lat.astype(BF16)
    kr = _rope_head(krb * _rms_rows(krb, ROPE_DIM) * g_kr_ref[...], cos, s_lo, s_hi)
    kr_ref[rs, :] = kr[:, NOPE_DIM:NOPE_DIM + ROPE_DIM]
    cqn = (cq * _rms_rows(cq, q_lora) * g_qa_ref[...]).astype(BF16)

    def q_heads():
        cg = cos * gq_ref[...]
        sg = (s_lo + s_hi) * gqx_ref[...]
        is_nope = lax.broadcasted_iota(jnp.int32, (rows, HEAD_PAD), 1) < NOPE_DIM
        for h in range(N_HEADS):
            sl = slice(h * HEAD_PAD, (h + 1) * HEAD_PAD)
            qh = qp[:, sl]
            sq = qh * qh
            r_n = lax.rsqrt(jnp.sum(jnp.where(is_nope, sq, 0.0), axis=-1, keepdims=True)
                            * (1.0 / NOPE_DIM) + EPS)
            r_r = lax.rsqrt(jnp.sum(jnp.where(is_nope, 0.0, sq), axis=-1, keepdims=True)
                            * (1.0 / ROPE_DIM) + EPS)
            q_ref[rs, sl] = (jnp.where(is_nope, r_n, r_r) * (qh * cg + qp2[:, sl] * sg)).astype(BF16)

    def k_heads(kp):
        for h in range(N_HEADS):
            sl = slice(h * HEAD_PAD, (h + 1) * HEAD_PAD)
            kh = kp[:, sl]
            k_ref[rs, sl] = (kh * _rms_rows(kh, NOPE_DIM) * gk_ref[:, sl] + kr).astype(BF16)

    hgt = SUBLANES if nb == 1 else nb
    steps = rows // hgt
    x_slabs = {g: u[g * hgt:(g + 1) * hgt] for g in range(steps)}
    first_run = lax.broadcasted_iota(jnp.int32, (hgt, d_model), 0) == 0
    for j in range(1, conv_w):
        if nb == 1:
            x_slabs[-j] = jnp.where(first_run, xh_ref[hp - j:hp - j + 1, :], pltpu.roll(x_slabs[steps - j], 1, 0))
        else:
            x_slabs[-j] = xh_ref[hp - j * nb:hp - (j - 1) * nb, :]
    uc_slabs = []
    for g in range(steps):
        acc = cb_ref[...] + cw_ref[conv_w - 1:conv_w, :] * x_slabs[g]
        for j in range(1, conv_w):
            acc = acc + cw_ref[conv_w - 1 - j:conv_w - j, :] * x_slabs[g - j]
        uc_slabs.append(acc)
    u_c = jnp.concatenate(uc_slabs, axis=0)
    if nb == 1:
        row = lax.broadcasted_iota(jnp.int32, (hp, d_model), 0)
        tail = jnp.zeros((hp, d_model), F32)
        for j in range(1, conv_w):
            last = jnp.broadcast_to(x_slabs[steps - j][hgt - 1:hgt], (hp, d_model))
            tail = jnp.where(row == hp - j, last, tail)
    else:
        tail = u[rows - hp:rows]
    cst_ref[...] = tail
    xh_ref[...] = tail
    yield

    qp = _dot(cqn, wq_ref[...])
    qp2 = _dot(cqn, wqx_ref[...])
    n_blocks, blk_w, _ = wlru_ref.shape
    gate_proj = [_dot(u_c[:, n * blk_w:(n + 1) * blk_w].astype(BF16), wlru_ref[n])
                 for n in range(n_blocks)]
    yield

    q_heads()
    yield

    kp = _dot(latb, wk_ref[...])
    vslab = vt_ref.shape[-1]
    vt_ref[r0 // vslab, :, r0 % vslab:r0 % vslab + sub] = _dot_nt(wvt_ref[...], latb).astype(BF16)
    gate_a = in_proj(xn, o_ga, o_gb)
    gate_b = in_proj(xs, o_gb, o_gb + d_model)
    yield

    lam = lam_ref[...]
    softplus_neg = jnp.maximum(-lam, 0.0) + jnp.log1p(jnp.exp(-jnp.abs(lam)))
    a_parts, b_parts = [], []
    for n in range(n_blocks):
        sl = slice(n * blk_w, (n + 1) * blk_w)
        ucn = u_c[:, sl]
        g = gate_proj[n]
        r = _sigmoid(g[:, 0:blk_w] + brg_ref[:, sl])
        i = _sigmoid(g[:, blk_w:2 * blk_w] + big_ref[:, sl])
        log_a = (-RG_C * r) * softplus_neg[:, sl]
        a_n = jnp.exp(log_a)
        a_parts.append(a_n)
        b_parts.append(jnp.sqrt(-jnp.tanh(log_a) * (1.0 + a_n * a_n)) * (i * ucn))
    a = jnp.concatenate(a_parts, axis=1)
    b = jnp.concatenate(b_parts, axis=1)

    a_g = a[0:hgt]
    h_in = a_g * hprev_ref[...]
    if nb == 1:
        h_in = jnp.where(lax.broadcasted_iota(jnp.int32, h_in.shape, 0) == 0, h_in, 0.0)
    h = b[0:hgt] + h_in
    a_run = a_g
    h_slabs, a_slabs = [h], [a_run]
    for g in range(1, steps):
        a_g = a[g * hgt:(g + 1) * hgt]
        h = a_g * h + b[g * hgt:(g + 1) * hgt]
        h_slabs.append(h)
        if nb == 1:
            a_run = a_g * a_run
            a_slabs.append(a_run)
    if nb == 1:
        run_end = _linear_scan(a_run, h, 1)
        run_in = _shift_rows(run_end, 1, 0.0)
        h_slabs = [h_g + a_r * run_in for h_g, a_r in zip(h_slabs, a_slabs)]
        h_last = run_end[hgt - 1:hgt]
    else:
        h_last = h_slabs[-1]
    hprev_ref[...] = h_last
    hl_ref[...] = h_last
    rg = (_sigmoid(gate_b) * jnp.concatenate(h_slabs, axis=0)).astype(BF16)
    if nb == 1:
        rg = _dot(perm_ref[1], rg).astype(BF16)
    rg_ref[rs, :] = rg
    yield

    k_heads(kp)
    ga_ref[rs, :] = _sigmoid(gate_a).astype(BF16)


def _slab_permutation(rows):
    steps = rows // SUBLANES
    slab_row = jnp.arange(rows)
    time = (slab_row % SUBLANES) * steps + slab_row // SUBLANES
    gather = (time[:, None] == jnp.arange(rows)[None, :]).astype(BF16)
    return jnp.stack([gather, gather.T])


def _mixer_in(x, hist, h0, rope, wts, *, rows, nb):
    groups, tg, d_model = x.shape
    hp = hist.shape[1]
    q_lora = wts["wq"].shape[0]
    kv_lora = wts["wk"].shape[0]
    conv_w = wts["cw"].shape[0]
    n_t = tg // rows
    qk_w = N_HEADS * HEAD_PAD
    v_w = N_HEADS * V_DIM

    def row_spec(width):
        return pl.BlockSpec((None, rows, width), lambda g, t: (g, t, 0))

    def group_spec(r, width):
        return pl.BlockSpec((None, r, width), lambda g, t: (g, 0, 0))

    tab_spec = pl.BlockSpec((rows, HEAD_PAD), lambda g, t: (t, 0))
    w_names = ["g_mix", "w_in", "g_qa", "wq", "wqx", "gq", "gqx", "g_kva", "g_kr", "wk", "gk", "wvt",
               "cw", "cb", "wlru", "brg", "big", "lam"]
    sub = min(rows, SLAB_SUB_ROWS) if nb == 1 else rows
    vslab = min(rows, VALUE_SLAB_ROWS)
    assert rows % sub == 0 and rows % vslab == 0 and vslab % sub == 0
    w_list = [wts[n] for n in w_names] + ([_slab_permutation(sub)] if nb == 1 else [])
    in_specs = ([row_spec(d_model), group_spec(hp, d_model), group_spec(nb, d_model),
                 tab_spec, tab_spec, tab_spec] + [_const_spec(w.shape) for w in w_list])
    out_shape = (
        jax.ShapeDtypeStruct((groups, tg, qk_w), BF16),
        jax.ShapeDtypeStruct((groups, tg, qk_w), BF16),
        jax.ShapeDtypeStruct((groups, tg // vslab, v_w, vslab), BF16),
        jax.ShapeDtypeStruct((groups, tg, kv_lora), F32),
        jax.ShapeDtypeStruct((groups, tg, ROPE_DIM), F32),
        jax.ShapeDtypeStruct((groups, tg, d_model), BF16),
        jax.ShapeDtypeStruct((groups, tg, d_model), BF16),
        jax.ShapeDtypeStruct((groups, nb, d_model), F32),
        jax.ShapeDtypeStruct((groups, hp, d_model), F32),
    )
    vt_spec = pl.BlockSpec((None, rows // vslab, v_w, vslab), lambda g, t: (g, t, 0, 0))
    out_specs = (row_spec(qk_w), row_spec(qk_w), vt_spec, row_spec(kv_lora), row_spec(ROPE_DIM),
                 row_spec(d_model), row_spec(d_model), group_spec(nb, d_model), group_spec(hp, d_model))
    w_bytes = sum(w.size * w.dtype.itemsize for w in w_list)
    io_bytes = 2 * rows * (4 * d_model + 2 * (2 * qk_w + v_w + 2 * d_model) + 4 * (kv_lora + LANES) + 12 * LANES)
    tmp_bytes = 4 * rows * (wts["w_in"].shape[1] + 2 * qk_w + 6 * d_model) + 4 * (hp + rows) * d_model
    kern = functools.partial(_mixer_in_kernel, rows=rows, sub=sub, nb=nb, hp=hp, d_model=d_model,
                             q_lora=q_lora, kv_lora=kv_lora, conv_w=conv_w)
    return pl.pallas_call(
        kern,
        grid=(groups, n_t),
        in_specs=in_specs,
        out_specs=out_specs,
        out_shape=out_shape,
        scratch_shapes=[pltpu.VMEM((hp, d_model), F32), pltpu.VMEM((nb, d_model), F32)],
        compiler_params=pltpu.CompilerParams(
            dimension_semantics=("parallel", "arbitrary"),
            vmem_limit_bytes=_vmem_limit(w_bytes + io_bytes + tmp_bytes)),
        name="mixer_in",
    )(x, hist, h0, *rope, *w_list)


def _attn_kernel(q_ref, k_ref, vt_ref, ga_ref, rg_ref, o_ref, m_ref, acc_ref, carry_ref, *, tq, tk, cq, vblk):
    qi = pl.program_id(2)
    m_ref[...] = jnp.full(m_ref.shape, -jnp.inf, F32)
    acc_ref[...] = jnp.zeros(acc_ref.shape, F32)

    n_sub = tq // tk

    def wide(kj, b):
        return [(kj, hh, b * tk, tk, tk, None) for hh in range(2)]

    def narrow(kj, b):
        return [(kj, hh, b * tk + q0, cq, q0 + cq, q0) for hh in range(2) for q0 in range(0, tk, cq)]

    def scores(kj, hh, q0, qw, n_keys):
        lanes = slice(hh * HEAD_PAD, (hh + 1) * HEAD_PAD)
        k0 = pl.multiple_of(kj * tk, tk)
        return _dot_nt(k_ref[pl.ds(k0, n_keys), lanes], q_ref[q0:q0 + qw, lanes])

    def diag_mask(s, q0):
        k_chunk = lax.broadcasted_iota(jnp.int32, s.shape, 0) >> CHUNK_SHIFT
        q_chunk = (q0 + lax.broadcasted_iota(jnp.int32, s.shape, 1)) >> CHUNK_SHIFT
        return jnp.where(k_chunk <= q_chunk, s, -jnp.inf)

    def softmax(hh, q0, qw, s):
        qs = slice(q0, q0 + qw)
        m_prev = m_ref[hh, :, qs]
        m_new = jnp.maximum(m_prev, jnp.max(s, axis=0, keepdims=True))
        m_ref[hh, :, qs] = m_new
        return jnp.exp2(s - m_new).astype(BF16), jnp.exp2(m_prev - m_new)

    def values(kj, hh, q0, qw, pb, alpha):
        qs = slice(q0, q0 + qw)
        pv = None
        n_keys = pb.shape[0]
        piece = min(vblk, n_keys)
        for j0 in range(0, n_keys, piece):
            vt = vt_ref[kj * (tk // vblk) + j0 // vblk, hh * V_DIM:(hh + 1) * V_DIM, pl.ds(j0 % vblk, piece)]
            ones_rows = jnp.ones((acc_ref.shape[1] - V_DIM, piece), BF16)
            part = _dot(jnp.concatenate([vt, ones_rows], axis=0), pb[j0:j0 + piece])
            pv = part if pv is None else pv + part
        acc_ref[hh, :, qs] = alpha * acc_ref[hh, :, qs] + pv

    def run_chains(items, next_tile):
        _, _, _, qw0, n_keys0, _ = items[0]
        s_next = carry_ref[0:n_keys0, 0:qw0]
        prev = None
        for i, (kj, hh, q0, qw, n_keys, mask_q0) in enumerate(items):
            s_cur = s_next
            if i + 1 < len(items):
                s_next = scores(*items[i + 1][:5])
            elif next_tile is not None:
                carry_ref[...] = scores(*wide(next_tile, 0)[0][:5])
            if mask_q0 is not None:
                s_cur = diag_mask(s_cur, mask_q0)
            cur = softmax(hh, q0, qw, s_cur)
            if prev is not None:
                values(*items[i - 1][:4], *prev)
            prev = cur
        values(*items[-1][:4], *prev)

    carry_ref[...] = scores(*wide(0, 0)[0][:5])

    @pl.loop(0, qi)
    def _(j):
        run_chains([c for d in range(n_sub) for b in range(n_sub) for c in wide(n_sub * j + d, b)],
                   n_sub * (j + 1))

    diag_items = []
    for d in range(n_sub):
        kj = n_sub * qi + d
        diag_items += narrow(kj, d) + [c for b in range(d + 1, n_sub) for c in wide(kj, b)]
    run_chains(diag_items, None)

    attn_t = jnp.concatenate([acc_ref[hh, 0:V_DIM, :] / acc_ref[hh, V_DIM:V_DIM + 1, :] for hh in range(2)],
                             axis=0)
    o_ref[...] = (ga_ref[...].astype(F32) * attn_t.T + rg_ref[...].astype(F32)).astype(BF16)


def _attention(q, k, vt, ga, rg, *, tq, tk):
    bsz, t, _ = q.shape
    n_vb, _, vblk = vt.shape[1:]
    assert LANES == 2 * V_DIM and N_HEADS % 2 == 0
    assert k.shape[1] == t == n_vb * vblk and tk % vblk == 0 and t % tq == 0 and tq % tk == 0
    assert tk % CHUNK == 0 and tk % MXU_DIM == 0
    n_hp = N_HEADS // 2
    qk_blk = 2 * HEAD_PAD

    def q_index(b, p, qi):
        return (b, qi, p)

    kern = functools.partial(_attn_kernel, tq=tq, tk=tk, cq=MXU_DIM, vblk=vblk)
    blk_bytes = 2 * 2 * (tq * qk_blk + t * qk_blk + t * LANES + 3 * tq * LANES)
    tmp_bytes = 4 * 8 * tk * tk + 4 * 4 * tq * LANES
    return pl.pallas_call(
        kern,
        grid=(bsz, n_hp, t // tq),
        in_specs=[pl.BlockSpec((None, tq, qk_blk), q_index),
                  pl.BlockSpec((None, t, qk_blk), lambda b, p, qi: (b, 0, p)),
                  pl.BlockSpec((None, n_vb, LANES, vblk), lambda b, p, qi: (b, 0, p, 0)),
                  pl.BlockSpec((None, tq, LANES), q_index),
                  pl.BlockSpec((None, tq, LANES), q_index)],
        out_specs=pl.BlockSpec((None, tq, LANES), q_index),
        out_shape=jax.ShapeDtypeStruct((bsz, t, N_HEADS * V_DIM), BF16),
        scratch_shapes=[pltpu.VMEM((2, 1, tq), F32),
                        pltpu.VMEM((2, V_DIM + BF16_SUBLANES, tq), F32),
                        pltpu.VMEM((tk, tk), F32)],
        compiler_params=pltpu.CompilerParams(
            dimension_semantics=("parallel", "parallel", "arbitrary"),
            vmem_limit_bytes=_vmem_limit(blk_bytes + tmp_bytes)),
        name="attention",
    )(q, k, vt, ga, rg)


def _cache_attn_kernel(q_ref, latc_ref, krc_ref, latn_ref, krn_ref, ga_ref, rg_ref,
                       wkg_ref, wkc_ref, seg_ref, expand_ref, wv_ref, o_ref,
                       qabs_ref, qr_ref, m_ref, acc_ref, *, t, tk, n_new):
    past = latc_ref.shape[0]
    hq = N_HEADS * t
    for h in range(N_HEADS):
        qh = q_ref[:, h * HEAD_PAD:(h + 1) * HEAD_PAD]
        qabs_ref[h * t:(h + 1) * t, :] = _dot(qh, wkg_ref[h]).astype(BF16)
        qr_ref[h * t:(h + 1) * t, :] = qh[:, NOPE_DIM:QK_DIM]
    m_ref[...] = jnp.full(m_ref.shape, -jnp.inf, F32)
    acc_ref[...] = jnp.zeros(acc_ref.shape, F32)
    kv_lora = latc_ref.shape[1]

    def scores(lat, kr, n_valid):
        n = lat.shape[0]
        latb = lat.astype(BF16)
        kvk = _dot(latb, wkc_ref[...])
        ssum = _dot((kvk * kvk).astype(BF16), seg_ref[...])
        r = lax.rsqrt(ssum * (1.0 / NOPE_DIM) + EPS)
        r_hi = r.astype(BF16)
        r_lo = (r - r_hi.astype(F32)).astype(BF16)
        r_cols = _dot(jnp.concatenate([r_hi, r_lo], axis=1), expand_ref[...])
        s = _dot_nt(latb, qabs_ref[...]) * r_cols + _dot_nt(kr.astype(BF16), qr_ref[...])
        if n_valid < n:
            s = jnp.where(lax.broadcasted_iota(jnp.int32, (n, hq), 0) < n_valid, s, -jnp.inf)
        lat_t = jnp.concatenate([lat.T.astype(BF16), jnp.ones((acc_ref.shape[0] - kv_lora, n), BF16)], axis=0)
        return s, lat_t

    def accumulate(s, lat_t):
        m_prev = m_ref[...]
        m_new = jnp.maximum(m_prev, jnp.max(s, axis=0, keepdims=True))
        m_ref[...] = m_new
        p = jnp.exp2(s - m_new).astype(BF16)
        acc_ref[...] = jnp.exp2(m_prev - m_new) * acc_ref[...] + _dot(lat_t, p)

    def tile_scores(j):
        if j < past // tk:
            return scores(latc_ref[j * tk:(j + 1) * tk, :], krc_ref[j * tk:(j + 1) * tk, :], tk)
        return scores(latn_ref[...], krn_ref[...], n_new)

    n_tiles = past // tk + 1
    cur = tile_scores(0)
    for j in range(n_tiles):
        nxt = tile_scores(j + 1) if j + 1 < n_tiles else None
        accumulate(*cur)
        cur = nxt

    ctx = (acc_ref[0:kv_lora, :] / acc_ref[kv_lora:kv_lora + 1, :]).T.astype(BF16)
    first_head = lax.broadcasted_iota(jnp.int32, (t, LANES), 1) < V_DIM
    for pair in range(N_HEADS // 2):
        cols = slice(pair * LANES, (pair + 1) * LANES)
        wv = wv_ref[:, cols]
        a0 = _dot(ctx[(2 * pair) * t:(2 * pair + 1) * t], wv)
        a1 = _dot(ctx[(2 * pair + 1) * t:(2 * pair + 2) * t], wv)
        attn = jnp.where(first_head, a0, a1)
        o_ref[:, cols] = (ga_ref[:, cols].astype(F32) * attn + rg_ref[:, cols].astype(F32)).astype(BF16)


def _cache_attention(q, lat_cache, kr_cache, lat_new, kr_new, ga, rg, wts, *, tk, n_new):
    bsz, t, qk_w = q.shape
    past, kv_lora = lat_cache.shape[1:]
    n_pad = lat_new.shape[1]
    d_model = ga.shape[2]
    hq = N_HEADS * t
    assert past % tk == 0 and LANES == 2 * V_DIM
    w_list = [wts["wkg"], wts["wkc"], wts["seg"], wts["expand"], wts["wv"]]
    w_bytes = sum(w.size * w.dtype.itemsize for w in w_list)
    blk_bytes = 2 * (4 * (past + n_pad) * (kv_lora + LANES) + 2 * t * (qk_w + 3 * d_model))
    tmp_bytes = 4 * tk * (2 * N_HEADS * NOPE_DIM + 4 * hq)

    def batch_spec(r, width):
        return pl.BlockSpec((None, r, width), lambda b: (b, 0, 0))

    kern = functools.partial(_cache_attn_kernel, t=t, tk=tk, n_new=n_new)
    return pl.pallas_call(
        kern,
        grid=(bsz,),
        in_specs=[batch_spec(t, qk_w), batch_spec(past, kv_lora), batch_spec(past, ROPE_DIM),
                  batch_spec(n_pad, kv_lora), batch_spec(n_pad, ROPE_DIM),
                  batch_spec(t, d_model), batch_spec(t, d_model)] + [_const_spec(w.shape) for w in w_list],
        out_specs=batch_spec(t, d_model),
        out_shape=jax.ShapeDtypeStruct((bsz, t, d_model), BF16),
        scratch_shapes=[pltpu.VMEM((hq, kv_lora), BF16), pltpu.VMEM((hq, ROPE_DIM), BF16),
                        pltpu.VMEM((1, hq), F32), pltpu.VMEM((kv_lora + BF16_SUBLANES, hq), F32)],
        compiler_params=pltpu.CompilerParams(
            dimension_semantics=("parallel",),
            vmem_limit_bytes=_vmem_limit(w_bytes + blk_bytes + tmp_bytes)),
        name="cache_attention",
    )(q, lat_cache, kr_cache, lat_new, kr_new, ga, rg, *w_list)


def _mixer_out_kernel(x_ref, mix_ref, hist_ref, w_out_ref, g_ffn_ref, w_up_ref, fw_ref, fb_ref, w_down_ref,
                      y_ref, fst_ref, upc_ref, *, rows, nb, hp, d_model, d_ff, conv_w, col_blk):
    t = pl.program_id(1)

    @pl.when(t == 0)
    def _():
        upc_ref[0:hp, :] = hist_ref[...]

    x1 = x_ref[...] + _dot(mix_ref[...], w_out_ref[...])
    xn = (x1 * _rms_rows(x1, d_model) * g_ffn_ref[...]).astype(BF16)

    def up_proj(c):
        for c0 in (c * col_blk, d_ff + c * col_blk):
            sl = slice(c0, c0 + col_blk)
            upc_ref[hp:hp + rows, sl] = _dot(xn, w_up_ref[:, sl])

    def conv_cols(c0):
        sl = slice(c0, c0 + col_blk)
        out = fb_ref[:, sl] + fw_ref[conv_w - 1:conv_w, sl] * upc_ref[hp:hp + rows, sl]
        for j in range(1, conv_w):
            out = out + fw_ref[conv_w - 1 - j:conv_w - j, sl] * upc_ref[hp - j * nb:hp - j * nb + rows, sl]
        return out

    def gated(c):
        gate = conv_cols(c * col_blk)
        val = conv_cols(d_ff + c * col_blk)
        return (gate * _sigmoid(gate) * val).astype(BF16)

    def down_proj(c, hmid):
        return _dot(hmid, w_down_ref[c * col_blk:(c + 1) * col_blk, :])

    n_chunks = d_ff // col_blk
    y = x1
    up_proj(0)
    prev = None
    for c in range(n_chunks):
        if c + 1 < n_chunks:
            up_proj(c + 1)
        hmid = gated(c)
        if prev is not None:
            y = y + down_proj(c - 1, prev)
        prev = hmid
    y_ref[...] = y + down_proj(n_chunks - 1, prev)

    tail = upc_ref[rows:rows + hp, :]
    fst_ref[...] = tail
    upc_ref[0:hp, :] = tail


def _mixer_out(x, mixed, hist, wts, *, rows, nb):
    groups, tg, d_model = x.shape
    hp = hist.shape[1]
    d_ff = wts["w_down"].shape[0]
    conv_w = wts["fw"].shape[0]
    n_t = tg // rows
    col_blk = MXU_DIM

    def row_spec(width):
        return pl.BlockSpec((None, rows, width), lambda g, t: (g, t, 0))

    def group_spec(r, width):
        return pl.BlockSpec((None, r, width), lambda g, t: (g, 0, 0))

    w_names = ["w_out", "g_ffn", "w_up", "fw", "fb", "w_down"]
    w_list = [wts[n] for n in w_names]
    w_bytes = sum(w.size * w.dtype.itemsize for w in w_list)
    blk_bytes = 2 * rows * d_model * (4 + 2 + 4) + 4 * 4 * hp * 2 * d_ff
    tmp_bytes = 4 * (hp + rows) * 2 * d_ff + 4 * rows * (3 * d_model + 8 * col_blk)
    kern = functools.partial(_mixer_out_kernel, rows=rows, nb=nb, hp=hp, d_model=d_model, d_ff=d_ff,
                             conv_w=conv_w, col_blk=col_blk)
    return pl.pallas_call(
        kern,
        grid=(groups, n_t),
        in_specs=[row_spec(d_model), row_spec(d_model), group_spec(hp, 2 * d_ff)]
                 + [_const_spec(w.shape) for w in w_list],
        out_specs=(row_spec(d_model), group_spec(hp, 2 * d_ff)),
        out_shape=(jax.ShapeDtypeStruct((groups, tg, d_model), F32),
                   jax.ShapeDtypeStruct((groups, hp, 2 * d_ff), F32)),
        scratch_shapes=[pltpu.VMEM((hp + rows, 2 * d_ff), F32)],
        compiler_params=pltpu.CompilerParams(
            dimension_semantics=("parallel", "arbitrary"),
            vmem_limit_bytes=_vmem_limit(w_bytes + blk_bytes + tmp_bytes)),
        name="mixer_out",
    )(x, mixed, hist, *w_list)


def _head_pad(w, widths):
    lead = w.shape[:-1]
    per_head = sum(widths)
    w = w.reshape(lead + (N_HEADS, per_head))
    w = jnp.pad(w, [(0, 0)] * len(lead) + [(0, 0), (0, HEAD_PAD - per_head)])
    return w.reshape(lead + (N_HEADS * HEAD_PAD,))


def _prep_weights(l, g_mix_norm, w_in, g_q_a, w_q_b, g_kv_a, w_kv_b, g_qn, g_qr, g_kn, g_kr,
                  lru_conv_w, lru_conv_b, w_rg, b_rg, w_ig, b_ig, lru_lambda, w_out, g_ffn_norm,
                  w_up, ffn_conv_w, ffn_conv_b, w_down):
    d_model = w_in.shape[1]
    q_lora = w_q_b.shape[1]
    kv_lora = w_kv_b.shape[1]
    row = lambda a: a.reshape(1, -1)
    o_kr = q_lora + kv_lora
    wi = w_in[l].astype(BF16)
    w_kr = jnp.pad(wi[:, o_kr:o_kr + ROPE_DIM], ((0, 0), (NOPE_DIM, HEAD_PAD - QK_DIM)))
    w_in_p = jnp.concatenate([wi[:, :o_kr], w_kr, wi[:, o_kr + ROPE_DIM:]], axis=1)
    scale = QK_DIM ** -0.5 * LOG2_E
    gq = jnp.pad(jnp.concatenate([g_qn[l], g_qr[l]]) * scale, (0, HEAD_PAD - QK_DIM))
    g_rope = g_qr[l] * scale
    gqx = jnp.pad(jnp.concatenate([g_rope[HALF_ROPE:], g_rope[:HALF_ROPE]]), (NOPE_DIM, HEAD_PAD - QK_DIM))
    wq3 = w_q_b[l].reshape(q_lora, N_HEADS, QK_DIM)
    wqx = jnp.concatenate([jnp.zeros((q_lora, N_HEADS, NOPE_DIM), F32), wq3[:, :, NOPE_DIM + HALF_ROPE:],
                           wq3[:, :, NOPE_DIM:NOPE_DIM + HALF_ROPE]], axis=2).reshape(q_lora, N_HEADS * QK_DIM)
    gk = jnp.tile(jnp.pad(g_kn[l], (0, HEAD_PAD - NOPE_DIM)), N_HEADS)
    kv = w_kv_b[l].reshape(kv_lora, N_HEADS, NOPE_DIM + V_DIM)
    wk = _head_pad(kv[:, :, :NOPE_DIM].reshape(kv_lora, N_HEADS * NOPE_DIM), (NOPE_DIM,))
    wv = kv[:, :, NOPE_DIM:].reshape(kv_lora, N_HEADS * V_DIM)
    wkg = jnp.pad(jnp.transpose(kv[:, :, :NOPE_DIM], (1, 2, 0)) * g_kn[l][None, :, None],
                  ((0, 0), (0, HEAD_PAD - NOPE_DIM), (0, 0)))
    seg = jnp.pad(jnp.repeat(jnp.eye(N_HEADS, dtype=F32), NOPE_DIM, axis=0), ((0, 0), (0, LANES - N_HEADS)))
    return {
        "wkg": wkg.astype(BF16), "wkc": kv[:, :, :NOPE_DIM].reshape(kv_lora, -1).astype(BF16),
        "seg": seg.astype(BF16), "wv": wv.astype(BF16),
        "g_mix": row(g_mix_norm[l]), "w_in": w_in_p, "g_qa": row(g_q_a[l]),
        "wq": _head_pad(w_q_b[l], (NOPE_DIM, ROPE_DIM)).astype(BF16), "gq": row(gq), "gqx": row(gqx),
        "wqx": _head_pad(wqx, (NOPE_DIM, ROPE_DIM)).astype(BF16),
        "g_kva": row(g_kv_a[l]), "g_kr": row(jnp.pad(g_kr[l], (NOPE_DIM, HEAD_PAD - QK_DIM))),
        "wk": wk.astype(BF16), "gk": row(gk), "wvt": wv.T.astype(BF16),
        "cw": lru_conv_w[l], "cb": row(lru_conv_b[l]),
        "wlru": jnp.concatenate([w_rg[l], w_ig[l]], axis=-1).astype(BF16),
        "brg": row(b_rg[l]), "big": row(b_ig[l]), "lam": row(lru_lambda[l]),
        "w_out": w_out[l].astype(BF16), "g_ffn": row(g_ffn_norm[l]), "w_up": w_up[l].astype(BF16),
        "fw": ffn_conv_w[l], "fb": row(ffn_conv_b[l]), "w_down": w_down[l].astype(BF16),
    }


def _rope_tables(pos):
    inv = np.float32(ROPE_THETA) ** (-np.arange(0, ROPE_DIM, 2, dtype=np.float32) / np.float32(ROPE_DIM))
    ang = pos.astype(np.float32)[:, None] * inv[None, :]
    cos = np.cos(ang.astype(np.float64)).astype(np.float32)
    sin = np.sin(ang.astype(np.float64)).astype(np.float32)
    n = pos.shape[0]
    ones_lo = np.ones((n, NOPE_DIM), np.float32)
    zeros_lo = np.zeros((n, NOPE_DIM), np.float32)
    zeros_half = np.zeros((n, HALF_ROPE), np.float32)
    tail = np.zeros((n, HEAD_PAD - QK_DIM), np.float32)
    c = np.concatenate([ones_lo, cos, cos, tail], axis=1)
    s_lo = np.concatenate([zeros_lo, -sin, zeros_half, tail], axis=1)
    s_hi = np.concatenate([zeros_lo, zeros_half, sin, tail], axis=1)
    return jnp.asarray(c), jnp.asarray(s_lo), jnp.asarray(s_hi)


def _expand_matrix(t):
    one_part = jnp.pad(jnp.repeat(jnp.eye(N_HEADS, dtype=F32), t, axis=1), ((0, LANES - N_HEADS), (0, 0)))
    return jnp.concatenate([one_part, one_part], axis=0).astype(BF16)


def _front_pad_rows(a, hp):
    return jnp.pad(a, ((0, 0), (hp - a.shape[1], 0), (0, 0)))


def _layer_prompt(x, wts, *, rows, out_rows, attn_tq, attn_tk):
    bsz, t, d_model = x.shape
    lru_w = wts["cw"].shape[0]
    ffn_w = wts["fw"].shape[0]
    d_ff2 = wts["w_up"].shape[1]
    hp1 = _round_up(lru_w - 1, SUBLANES)
    hp2 = _round_up(ffn_w - 1, SUBLANES)
    rope = _rope_tables(np.arange(t, dtype=np.int32))
    q, k, vt, lat, kr, ga, rg, h_last, cst = _mixer_in(
        x, jnp.zeros((bsz, hp1, d_model), F32), jnp.zeros((bsz, 1, d_model), F32), rope, wts, rows=rows, nb=1)
    mixed = _attention(q, k, vt, ga, rg, tq=attn_tq, tk=attn_tk)
    y, fst = _mixer_out(x, mixed, jnp.zeros((bsz, hp2, d_ff2), F32), wts, rows=out_rows, nb=1)
    return y, (lat, kr, h_last[:, 0], cst[:, hp1 - (lru_w - 1):], fst[:, hp2 - (ffn_w - 1):])


def _layer_sample(x, past_lat, past_kr, h0, lru_buf, ffn_buf, wts, *, tk):
    bsz, t, d_model = x.shape
    past = past_lat.shape[1]
    lru_w = wts["cw"].shape[0]
    ffn_w = wts["fw"].shape[0]
    hp1 = _round_up((lru_w - 1) * bsz, SUBLANES)
    hp2 = _round_up((ffn_w - 1) * bsz, SUBLANES)
    rows = t * bsz

    def to_tm(a):
        return jnp.swapaxes(a, 0, 1).reshape(1, a.shape[1] * bsz, a.shape[2])

    def from_tm(a):
        return jnp.swapaxes(a.reshape(a.shape[1] // bsz, bsz, a.shape[2]), 0, 1)

    rope = _rope_tables(np.repeat(past + np.arange(t, dtype=np.int32), bsz))
    q, k, vt, lat, kr, ga, rg, h_last, cst = _mixer_in(
        to_tm(x), _front_pad_rows(to_tm(lru_buf), hp1), h0[None], rope, wts, rows=rows, nb=bsz)

    assert (past % CHUNK) + t <= CHUNK, "cache attention assumes all keys visible to all queries"
    n_pad = _round_up(t, LANES)
    pad_rows = lambda a: jnp.pad(from_tm(a), ((0, 0), (0, n_pad - t), (0, 0)))
    cache_wts = dict(wts, expand=_expand_matrix(t))
    mixed = _cache_attention(from_tm(q), past_lat, past_kr, pad_rows(lat), pad_rows(kr),
                             from_tm(ga), from_tm(rg), cache_wts, tk=tk, n_new=t)

    y, fst = _mixer_out(to_tm(x), to_tm(mixed), _front_pad_rows(to_tm(ffn_buf), hp2), wts, rows=rows, nb=bsz)
    states = (from_tm(lat), from_tm(kr), h_last[0],
              from_tm(cst[:, hp1 - (lru_w - 1) * bsz:]), from_tm(fst[:, hp2 - (ffn_w - 1) * bsz:]))
    return from_tm(y), states


def kernel(x_prompt, x_sample, cache_kv_latent, cache_k_rope, state_lru_h, state_lru_conv, state_ffn_conv,
           g_mix_norm, w_in, g_q_a, w_q_b, g_kv_a, w_kv_b, g_qn, g_qr, g_kn, g_kr, lru_conv_w, lru_conv_b,
           w_rg, b_rg, w_ig, b_ig, lru_lambda, w_out, g_ffn_norm, w_up, ffn_conv_w, ffn_conv_b, w_down):
    depth = w_in.shape[0]
    yp, ys = x_prompt, x_sample
    p_states, s_states = [], []
    for l in range(depth):
        wts = _prep_weights(l, g_mix_norm, w_in, g_q_a, w_q_b, g_kv_a, w_kv_b, g_qn, g_qr, g_kn, g_kr,
                            lru_conv_w, lru_conv_b, w_rg, b_rg, w_ig, b_ig, lru_lambda, w_out, g_ffn_norm,
                            w_up, ffn_conv_w, ffn_conv_b, w_down)
        yp, st_p = _layer_prompt(yp, wts, rows=512, out_rows=512, attn_tq=2048, attn_tk=512)
        ys, st_s = _layer_sample(ys, cache_kv_latent[l], cache_k_rope[l], state_lru_h[l],
                                 state_lru_conv[l], state_ffn_conv[l], wts, tk=1024)
        p_states.append(st_p)
        s_states.append(st_s)
    p_out = [jnp.stack([st[j] for st in p_states], axis=0) for j in range(5)]
    s_out = [jnp.stack([st[j] for st in s_states], axis=0) for j in range(5)]
    return (yp, ys, *p_out, *s_out)
```

```python
import functools

import jax
import jax.numpy as jnp
import numpy as np
from jax import lax
from jax.experimental import pallas as pl
from jax.experimental.pallas import tpu as pltpu

CHUNK = 64
CHUNK_SHIFT = CHUNK.bit_length() - 1
assert CHUNK == 1 << CHUNK_SHIFT
N_HEADS = 16
NOPE_DIM = 64
ROPE_DIM = 32
V_DIM = 64
QK_DIM = NOPE_DIM + ROPE_DIM
ROPE_THETA = 10000.0
RG_C = 8.0
EPS = 1e-6
LOG2_E = 1.4426950408889634

LANES = 128
SUBLANES = 8
BF16_SUBLANES = 16
MXU_DIM = 256
VMEM_BYTES_V7X = 64 * 1024 * 1024

HEAD_PAD = LANES
SLAB_SUB_ROWS = MXU_DIM
VALUE_SLAB_ROWS = 2 * MXU_DIM
HALF_ROPE = ROPE_DIM // 2
F32 = jnp.float32
BF16 = jnp.bfloat16


def _round_up(n, m):
    return (n + m - 1) // m * m


def _vmem_limit(nbytes):
    return int(min(2 * nbytes, VMEM_BYTES_V7X - 8 * 1024 * 1024))


def _const_spec(shape):
    nd = len(shape)
    return pl.BlockSpec(shape, lambda *_: (0,) * nd, pipeline_mode=pl.Buffered(1))


def _dot(a, b):
    return jnp.dot(a, b, preferred_element_type=F32)


def _dot_nt(a, b):
    return lax.dot_general(a, b, (((1,), (1,)), ((), ())), preferred_element_type=F32)


def _sigmoid(x):
    return 0.5 * jnp.tanh(0.5 * x) + 0.5


def _rms_rows(x, n):
    return lax.rsqrt(jnp.sum(x * x, axis=-1, keepdims=True) * (1.0 / n) + EPS)


def _rope_head(x, c, s_lo, s_hi):
    return (x * c + pltpu.roll(x, HALF_ROPE, 1) * s_hi
            + pltpu.roll(x, HEAD_PAD - HALF_ROPE, 1) * s_lo)


def _shift_rows(x, n, fill):
    rows, cols = x.shape
    if n % SUBLANES == 0:
        return jnp.concatenate([jnp.full((n, cols), fill, x.dtype), x[:rows - n]], axis=0)
    rolled = pltpu.roll(x, n, 0)
    row = lax.broadcasted_iota(jnp.int32, x.shape, 0)
    return jnp.where(row >= n, rolled, fill)


def _linear_scan(a, b, nb):
    rows = a.shape[0]
    s = nb
    while s < rows:
        b = a * _shift_rows(b, s, 0.0) + b
        if 2 * s < rows:
            a = a * _shift_rows(a, s, 1.0)
        s *= 2
    return b


def _mixer_in_kernel(*refs, rows, sub, **static):
    hist_ref, h0_ref = refs[1], refs[2]
    xh_ref, hprev_ref = refs[-2], refs[-1]

    @pl.when(pl.program_id(1) == 0)
    def _():
        xh_ref[...] = hist_ref[...]
        hprev_ref[...] = h0_ref[...]

    waiting = [_mixer_in_stages(*refs, r0=r0, sub=sub, **static) for r0 in range(0, rows, sub)]
    running = []
    while waiting or running:
        if waiting:
            running.append(waiting.pop(0))
        for stages in list(running):
            if next(stages, "done") == "done":
                running.remove(stages)


def _mixer_in_stages(x_ref, hist_ref, h0_ref, cos_ref, slo_ref, shi_ref,
                     g_mix_ref, w_in_ref, g_qa_ref, wq_ref, wqx_ref, gq_ref, gqx_ref,
                     g_kva_ref, g_kr_ref,
                     wk_ref, gk_ref, wvt_ref, cw_ref, cb_ref, wlru_ref, brg_ref, big_ref, lam_ref, *rest,
                     r0, sub, nb, hp, d_model, q_lora, kv_lora, conv_w):
    if nb == 1:
        perm_ref, *rest = rest
    q_ref, k_ref, vt_ref, lat_ref, kr_ref, ga_ref, rg_ref, hl_ref, cst_ref, xh_ref, hprev_ref = rest
    rs = slice(r0, r0 + sub)
    rows = sub

    o_kv = q_lora
    o_kr = o_kv + kv_lora
    o_u = o_kr + HEAD_PAD
    o_ga = o_u + d_model
    o_gb = o_ga + d_model
    cos = cos_ref[rs, :]
    s_lo = slo_ref[rs, :]
    s_hi = shi_ref[rs, :]

    x = x_ref[rs, :]
    xn = (x * _rms_rows(x, d_model) * g_mix_ref[...]).astype(BF16)
    yield

    xs = _dot(perm_ref[0], xn).astype(BF16) if nb == 1 else xn

    def in_proj(lhs, c0, c1):
        return _dot(lhs, w_in_ref[:, c0:c1])

    z_lat = in_proj(xn, 0, o_u)
    u = in_proj(xs, o_u, o_ga)
    yield

    cq = z_lat[:, 0:o_kv]
    ckv = z_lat[:, o_kv:o_kr]
    krb = z_lat[:, o_kr:o_u]
    lat = ckv * _rms_rows(ckv, kv_lora) * g_kva_ref[...]
    lat_ref[rs, :] = lat
    latb = lat.astype(BF16)
    kr = _rope_head(krb * _rms_rows(krb, ROPE_DIM) * g_kr_ref[...], cos, s_lo, s_hi)
    kr_ref[rs, :] = kr[:, NOPE_DIM:NOPE_DIM + ROPE_DIM]
    cqn = (cq * _rms_rows(cq, q_lora) * g_qa_ref[...]).astype(BF16)

    def q_heads():
        is_nope = lax.broadcasted_iota(jnp.int32, (rows, HEAD_PAD), 1) < NOPE_DIM
        root_n = jnp.where(is_nope, NOPE_DIM ** 0.5, ROPE_DIM ** 0.5)
        cg = cos * gq_ref[...] * root_n
        sg = (s_lo + s_hi) * gqx_ref[...] * root_n
        for h in range(N_HEADS):
            sl = slice(h * HEAD_PAD, (h + 1) * HEAD_PAD)
            qh = qp[:, sl]
            sq = qh * qh
            r_n = lax.rsqrt(jnp.sum(jnp.where(is_nope, sq, 0.0), axis=-1, keepdims=True) + NOPE_DIM * EPS)
            r_r = lax.rsqrt(jnp.sum(jnp.where(is_nope, 0.0, sq), axis=-1, keepdims=True) + ROPE_DIM * EPS)
            q_ref[rs, sl] = (jnp.where(is_nope, r_n, r_r) * (qh * cg + qp2[:, sl] * sg)).astype(BF16)

    def k_heads(kp):
        gk = gk_ref[:, 0:HEAD_PAD] * NOPE_DIM ** 0.5
        for h in range(N_HEADS):
            sl = slice(h * HEAD_PAD, (h + 1) * HEAD_PAD)
            kh = kp[:, sl]
            r_k = lax.rsqrt(jnp.sum(kh * kh, axis=-1, keepdims=True) + NOPE_DIM * EPS)
            k_ref[rs, sl] = (kh * r_k * gk + kr).astype(BF16)

    hgt = SUBLANES if nb == 1 else nb
    steps = rows // hgt
    x_slabs = {g: u[g * hgt:(g + 1) * hgt] for g in range(steps)}
    first_run = lax.broadcasted_iota(jnp.int32, (hgt, d_model), 0) == 0
    for j in range(1, conv_w):
        if nb == 1:
            x_slabs[-j] = jnp.where(first_run, xh_ref[hp - j:hp - j + 1, :], pltpu.roll(x_slabs[steps - j], 1, 0))
        else:
            x_slabs[-j] = xh_ref[hp - j * nb:hp - (j - 1) * nb, :]
    uc_slabs = []
    for g in range(steps):
        acc = cb_ref[...] + cw_ref[conv_w - 1:conv_w, :] * x_slabs[g]
        for j in range(1, conv_w):
            acc = acc + cw_ref[conv_w - 1 - j:conv_w - j, :] * x_slabs[g - j]
        uc_slabs.append(acc)
    u_c = jnp.concatenate(uc_slabs, axis=0)
    if nb == 1:
        row = lax.broadcasted_iota(jnp.int32, (hp, d_model), 0)
        tail = jnp.zeros((hp, d_model), F32)
        for j in range(1, conv_w):
            last = jnp.broadcast_to(x_slabs[steps - j][hgt - 1:hgt], (hp, d_model))
            tail = jnp.where(row == hp - j, last, tail)
    else:
        tail = u[rows - hp:rows]
    cst_ref[...] = tail
    xh_ref[...] = tail
    yield

    qp = _dot(cqn, wq_ref[...])
    qp2 = _dot(cqn, wqx_ref[...])
    n_blocks, blk_w, _ = wlru_ref.shape
    gate_proj = [_dot(u_c[:, n * blk_w:(n + 1) * blk_w].astype(BF16), wlru_ref[n])
                 for n in range(n_blocks)]
    yield

    q_heads()
    yield

    kp = _dot(latb, wk_ref[...])
    vslab = vt_ref.shape[-1]
    vt_ref[r0 // vslab, :, r0 % vslab:r0 % vslab + sub] = _dot_nt(wvt_ref[...], latb).astype(BF16)
    gate_a = in_proj(xn, o_ga, o_gb)
    gate_b = in_proj(xs, o_gb, o_gb + d_model)
    yield

    lam = lam_ref[...]
    softplus_neg = jnp.maximum(-lam, 0.0) + jnp.log1p(jnp.exp(-jnp.abs(lam)))
    a_parts, b_parts = [], []
    for n in range(n_blocks):
        sl = slice(n * blk_w, (n + 1) * blk_w)
        ucn = u_c[:, sl]
        g = gate_proj[n]
        r = _sigmoid(g[:, 0:blk_w] + brg_ref[:, sl])
        i = _sigmoid(g[:, blk_w:2 * blk_w] + big_ref[:, sl])
        log_a = (-RG_C * r) * softplus_neg[:, sl]
        a_n = jnp.exp(log_a)
        a_parts.append(a_n)
        b_parts.append(jnp.sqrt(-jnp.tanh(log_a) * (1.0 + a_n * a_n)) * (i * ucn))
    a = jnp.concatenate(a_parts, axis=1)
    b = jnp.concatenate(b_parts, axis=1)

    a_g = a[0:hgt]
    h_in = a_g * hprev_ref[...]
    if nb == 1:
        h_in = jnp.where(lax.broadcasted_iota(jnp.int32, h_in.shape, 0) == 0, h_in, 0.0)
    h = b[0:hgt] + h_in
    a_run = a_g
    h_slabs, a_slabs = [h], [a_run]
    for g in range(1, steps):
        a_g = a[g * hgt:(g + 1) * hgt]
        h = a_g * h + b[g * hgt:(g + 1) * hgt]
        h_slabs.append(h)
        if nb == 1:
            a_run = a_g * a_run
            a_slabs.append(a_run)
    if nb == 1:
        run_end = _linear_scan(a_run, h, 1)
        run_in = _shift_rows(run_end, 1, 0.0)
        h_slabs = [h_g + a_r * run_in for h_g, a_r in zip(h_slabs, a_slabs)]
        h_last = run_end[hgt - 1:hgt]
    else:
        h_last = h_slabs[-1]
    hprev_ref[...] = h_last
    hl_ref[...] = h_last
    rg = (_sigmoid(gate_b) * jnp.concatenate(h_slabs, axis=0)).astype(BF16)
    if nb == 1:
        rg = _dot(perm_ref[1], rg).astype(BF16)
    rg_ref[rs, :] = rg
    yield

    k_heads(kp)
    ga_ref[rs, :] = _sigmoid(gate_a).astype(BF16)


def _slab_permutation(rows):
    steps = rows // SUBLANES
    slab_row = jnp.arange(rows)
    time = (slab_row % SUBLANES) * steps + slab_row // SUBLANES
    gather = (time[:, None] == jnp.arange(rows)[None, :]).astype(BF16)
    return jnp.stack([gather, gather.T])


def _mixer_in(x, hist, h0, rope, wts, *, rows, nb):
    groups, tg, d_model = x.shape
    hp = hist.shape[1]
    q_lora = wts["wq"].shape[0]
    kv_lora = wts["wk"].shape[0]
    conv_w = wts["cw"].shape[0]
    n_t = tg // rows
    qk_w = N_HEADS * HEAD_PAD
    v_w = N_HEADS * V_DIM

    def row_spec(width):
        return pl.BlockSpec((None, rows, width), lambda g, t: (g, t, 0))

    def group_spec(r, width):
        return pl.BlockSpec((None, r, width), lambda g, t: (g, 0, 0))

    tab_spec = pl.BlockSpec((rows, HEAD_PAD), lambda g, t: (t, 0))
    w_names = ["g_mix", "w_in", "g_qa", "wq", "wqx", "gq", "gqx", "g_kva", "g_kr", "wk", "gk", "wvt",
               "cw", "cb", "wlru", "brg", "big", "lam"]
    sub = min(rows, SLAB_SUB_ROWS) if nb == 1 else rows
    vslab = min(rows, VALUE_SLAB_ROWS)
    assert rows % sub == 0 and rows % vslab == 0 and vslab % sub == 0
    w_list = [wts[n] for n in w_names] + ([_slab_permutation(sub)] if nb == 1 else [])
    in_specs = ([row_spec(d_model), group_spec(hp, d_model), group_spec(nb, d_model),
                 tab_spec, tab_spec, tab_spec] + [_const_spec(w.shape) for w in w_list])
    out_shape = (
        jax.ShapeDtypeStruct((groups, tg, qk_w), BF16),
        jax.ShapeDtypeStruct((groups, tg, qk_w), BF16),
        jax.ShapeDtypeStruct((groups, tg // vslab, v_w, vslab), BF16),
        jax.ShapeDtypeStruct((groups, tg, kv_lora), F32),
        jax.ShapeDtypeStruct((groups, tg, ROPE_DIM), F32),
        jax.ShapeDtypeStruct((groups, tg, d_model), BF16),
        jax.ShapeDtypeStruct((groups, tg, d_model), BF16),
        jax.ShapeDtypeStruct((groups, nb, d_model), F32),
        jax.ShapeDtypeStruct((groups, hp, d_model), F32),
    )
    vt_spec = pl.BlockSpec((None, rows // vslab, v_w, vslab), lambda g, t: (g, t, 0, 0))
    out_specs = (row_spec(qk_w), row_spec(qk_w), vt_spec, row_spec(kv_lora), row_spec(ROPE_DIM),
                 row_spec(d_model), row_spec(d_model), group_spec(nb, d_model), group_spec(hp, d_model))
    w_bytes = sum(w.size * w.dtype.itemsize for w in w_list)
    io_bytes = 2 * rows * (4 * d_model + 2 * (2 * qk_w + v_w + 2 * d_model) + 4 * (kv_lora + LANES) + 12 * LANES)
    tmp_bytes = 4 * rows * (wts["w_in"].shape[1] + 2 * qk_w + 6 * d_model) + 4 * (hp + rows) * d_model
    kern = functools.partial(_mixer_in_kernel, rows=rows, sub=sub, nb=nb, hp=hp, d_model=d_model,
                             q_lora=q_lora, kv_lora=kv_lora, conv_w=conv_w)
    return pl.pallas_call(
        kern,
        grid=(groups, n_t),
        in_specs=in_specs,
        out_specs=out_specs,
        out_shape=out_shape,
        scratch_shapes=[pltpu.VMEM((hp, d_model), F32), pltpu.VMEM((nb, d_model), F32)],
        compiler_params=pltpu.CompilerParams(
            dimension_semantics=("parallel", "arbitrary"),
            vmem_limit_bytes=_vmem_limit(w_bytes + io_bytes + tmp_bytes)),
        name="mixer_in",
    )(x, hist, h0, *rope, *w_list)


def _attn_kernel(q_ref, k_ref, vt_ref, ga_ref, rg_ref, o_ref, m_ref, acc_ref, carry_ref, *, tq, tk, cq, vblk):
    qi = pl.program_id(2)
    m_ref[...] = jnp.full(m_ref.shape, -jnp.inf, F32)
    acc_ref[...] = jnp.zeros(acc_ref.shape, F32)

    n_sub = tq // tk

    def wide(kj, b):
        return [(kj, hh, b * tk, tk, tk, None) for hh in range(2)]

    def narrow(kj, b):
        return [(kj, hh, b * tk + q0, cq, q0 + cq, q0) for hh in range(2) for q0 in range(0, tk, cq)]

    def scores(kj, hh, q0, qw, n_keys):
        lanes = slice(hh * HEAD_PAD, (hh + 1) * HEAD_PAD)
        k0 = pl.multiple_of(kj * tk, tk)
        return _dot_nt(k_ref[pl.ds(k0, n_keys), lanes], q_ref[q0:q0 + qw, lanes])

    def diag_mask(s, q0):
        k_chunk = lax.broadcasted_iota(jnp.int32, s.shape, 0) >> CHUNK_SHIFT
        q_chunk = (q0 + lax.broadcasted_iota(jnp.int32, s.shape, 1)) >> CHUNK_SHIFT
        return jnp.where(k_chunk <= q_chunk, s, -jnp.inf)

    def softmax(hh, q0, qw, s):
        qs = slice(q0, q0 + qw)
        m_prev = m_ref[hh, :, qs]
        m_new = jnp.maximum(m_prev, jnp.max(s, axis=0, keepdims=True))
        m_ref[hh, :, qs] = m_new
        return jnp.exp2(s - m_new).astype(BF16), jnp.exp2(m_prev - m_new)

    def values(kj, hh, q0, qw, pb, alpha):
        qs = slice(q0, q0 + qw)
        pv = None
        n_keys = pb.shape[0]
        piece = min(vblk, n_keys)
        for j0 in range(0, n_keys, piece):
            vt = vt_ref[kj * (tk // vblk) + j0 // vblk, hh * V_DIM:(hh + 1) * V_DIM, pl.ds(j0 % vblk, piece)]
            ones_rows = jnp.ones((acc_ref.shape[1] - V_DIM, piece), BF16)
            part = _dot(jnp.concatenate([vt, ones_rows], axis=0), pb[j0:j0 + piece])
            pv = part if pv is None else pv + part
        acc_ref[hh, :, qs] = alpha * acc_ref[hh, :, qs] + pv

    def run_chains(items, next_tile):
        _, _, _, qw0, n_keys0, _ = items[0]
        s_next = carry_ref[0:n_keys0, 0:qw0]
        prev = None
        for i, (kj, hh, q0, qw, n_keys, mask_q0) in enumerate(items):
            s_cur = s_next
            if i + 1 < len(items):
                s_next = scores(*items[i + 1][:5])
            elif next_tile is not None:
                carry_ref[...] = scores(*wide(next_tile, 0)[0][:5])
            if mask_q0 is not None:
                s_cur = diag_mask(s_cur, mask_q0)
            cur = softmax(hh, q0, qw, s_cur)
            if prev is not None:
                values(*items[i - 1][:4], *prev)
            prev = cur
        values(*items[-1][:4], *prev)

    carry_ref[...] = scores(*wide(0, 0)[0][:5])

    @pl.loop(0, qi)
    def _(j):
        run_chains([c for d in range(n_sub) for b in range(n_sub) for c in wide(n_sub * j + d, b)],
                   n_sub * (j + 1))

    diag_items = []
    for d in range(n_sub):
        kj = n_sub * qi + d
        diag_items += narrow(kj, d) + [c for b in range(d + 1, n_sub) for c in wide(kj, b)]
    run_chains(diag_items, None)

    attn_t = jnp.concatenate([acc_ref[hh, 0:V_DIM, :] / acc_ref[hh, V_DIM:V_DIM + 1, :] for hh in range(2)],
                             axis=0)
    o_ref[...] = (ga_ref[...].astype(F32) * attn_t.T + rg_ref[...].astype(F32)).astype(BF16)


def _attention(q, k, vt, ga, rg, *, tq, tk):
    bsz, t, _ = q.shape
    n_vb, _, vblk = vt.shape[1:]
    assert LANES == 2 * V_DIM and N_HEADS % 2 == 0
    assert k.shape[1] == t == n_vb * vblk and tk % vblk == 0 and t % tq == 0 and tq % tk == 0
    assert tk % CHUNK == 0 and tk % MXU_DIM == 0
    n_hp = N_HEADS // 2
    qk_blk = 2 * HEAD_PAD

    def q_index(b, p, qi):
        return (b, qi, p)

    kern = functools.partial(_attn_kernel, tq=tq, tk=tk, cq=MXU_DIM, vblk=vblk)
    blk_bytes = 2 * 2 * (tq * qk_blk + t * qk_blk + t * LANES + 3 * tq * LANES)
    tmp_bytes = 4 * 8 * tk * tk + 4 * 4 * tq * LANES
    return pl.pallas_call(
        kern,
        grid=(bsz, n_hp, t // tq),
        in_specs=[pl.BlockSpec((None, tq, qk_blk), q_index),
                  pl.BlockSpec((None, t, qk_blk), lambda b, p, qi: (b, 0, p)),
                  pl.BlockSpec((None, n_vb, LANES, vblk), lambda b, p, qi: (b, 0, p, 0)),
                  pl.BlockSpec((None, tq, LANES), q_index),
                  pl.BlockSpec((None, tq, LANES), q_index)],
        out_specs=pl.BlockSpec((None, tq, LANES), q_index),
        out_shape=jax.ShapeDtypeStruct((bsz, t, N_HEADS * V_DIM), BF16),
        scratch_shapes=[pltpu.VMEM((2, 1, tq), F32),
                        pltpu.VMEM((2, V_DIM + BF16_SUBLANES, tq), F32),
                        pltpu.VMEM((tk, tk), F32)],
        compiler_params=pltpu.CompilerParams(
            dimension_semantics=("parallel", "parallel", "arbitrary"),
            vmem_limit_bytes=_vmem_limit(blk_bytes + tmp_bytes)),
        name="attention",
    )(q, k, vt, ga, rg)


def _cache_attn_kernel(q_ref, latc_ref, krc_ref, latn_ref, krn_ref, ga_ref, rg_ref,
                       wkg_ref, wkc_ref, seg_ref, expand_ref, wv_ref, o_ref,
                       qabs_ref, qr_ref, m_ref, acc_ref, *, t, tk, n_new):
    past = latc_ref.shape[0]
    hq = N_HEADS * t
    for h in range(N_HEADS):
        qh = q_ref[:, h * HEAD_PAD:(h + 1) * HEAD_PAD]
        qabs_ref[h * t:(h + 1) * t, :] = _dot(qh, wkg_ref[h]).astype(BF16)
        qr_ref[h * t:(h + 1) * t, :] = qh[:, NOPE_DIM:QK_DIM]
    m_ref[...] = jnp.full(m_ref.shape, -jnp.inf, F32)
    acc_ref[...] = jnp.zeros(acc_ref.shape, F32)
    kv_lora = latc_ref.shape[1]

    def scores(lat, kr, n_valid):
        n = lat.shape[0]
        latb = lat.astype(BF16)
        kvk = _dot(latb, wkc_ref[...])
        ssum = _dot((kvk * kvk).astype(BF16), seg_ref[...])
        r = lax.rsqrt(ssum * (1.0 / NOPE_DIM) + EPS)
        r_hi = r.astype(BF16)
        r_lo = (r - r_hi.astype(F32)).astype(BF16)
        r_cols = _dot(jnp.concatenate([r_hi, r_lo], axis=1), expand_ref[...])
        s = _dot_nt(latb, qabs_ref[...]) * r_cols + _dot_nt(kr.astype(BF16), qr_ref[...])
        if n_valid < n:
            s = jnp.where(lax.broadcasted_iota(jnp.int32, (n, hq), 0) < n_valid, s, -jnp.inf)
        lat_t = jnp.concatenate([lat.T.astype(BF16), jnp.ones((acc_ref.shape[0] - kv_lora, n), BF16)], axis=0)
        return s, lat_t

    def accumulate(s, lat_t):
        m_prev = m_ref[...]
        m_new = jnp.maximum(m_prev, jnp.max(s, axis=0, keepdims=True))
        m_ref[...] = m_new
        p = jnp.exp2(s - m_new).astype(BF16)
        acc_ref[...] = jnp.exp2(m_prev - m_new) * acc_ref[...] + _dot(lat_t, p)

    def tile_scores(j):
        if j < past // tk:
            return scores(latc_ref[j * tk:(j + 1) * tk, :], krc_ref[j * tk:(j + 1) * tk, :], tk)
        return scores(latn_ref[...], krn_ref[...], n_new)

    n_tiles = past // tk + 1
    cur = tile_scores(0)
    for j in range(n_tiles):
        nxt = tile_scores(j + 1) if j + 1 < n_tiles else None
        accumulate(*cur)
        cur = nxt

    ctx = (acc_ref[0:kv_lora, :] / acc_ref[kv_lora:kv_lora + 1, :]).T.astype(BF16)
    first_head = lax.broadcasted_iota(jnp.int32, (t, LANES), 1) < V_DIM
    for pair in range(N_HEADS // 2):
        cols = slice(pair * LANES, (pair + 1) * LANES)
        wv = wv_ref[:, cols]
        a0 = _dot(ctx[(2 * pair) * t:(2 * pair + 1) * t], wv)
        a1 = _dot(ctx[(2 * pair + 1) * t:(2 * pair + 2) * t], wv)
        attn = jnp.where(first_head, a0, a1)
        o_ref[:, cols] = (ga_ref[:, cols].astype(F32) * attn + rg_ref[:, cols].astype(F32)).astype(BF16)


def _cache_attention(q, lat_cache, kr_cache, lat_new, kr_new, ga, rg, wts, *, tk, n_new):
    bsz, t, qk_w = q.shape
    past, kv_lora = lat_cache.shape[1:]
    n_pad = lat_new.shape[1]
    d_model = ga.shape[2]
    hq = N_HEADS * t
    assert past % tk == 0 and LANES == 2 * V_DIM
    w_list = [wts["wkg"], wts["wkc"], wts["seg"], wts["expand"], wts["wv"]]
    w_bytes = sum(w.size * w.dtype.itemsize for w in w_list)
    blk_bytes = 2 * (4 * (past + n_pad) * (kv_lora + LANES) + 2 * t * (qk_w + 3 * d_model))
    tmp_bytes = 4 * tk * (2 * N_HEADS * NOPE_DIM + 4 * hq)

    def batch_spec(r, width):
        return pl.BlockSpec((None, r, width), lambda b: (b, 0, 0))

    kern = functools.partial(_cache_attn_kernel, t=t, tk=tk, n_new=n_new)
    return pl.pallas_call(
        kern,
        grid=(bsz,),
        in_specs=[batch_spec(t, qk_w), batch_spec(past, kv_lora), batch_spec(past, ROPE_DIM),
                  batch_spec(n_pad, kv_lora), batch_spec(n_pad, ROPE_DIM),
                  batch_spec(t, d_model), batch_spec(t, d_model)] + [_const_spec(w.shape) for w in w_list],
        out_specs=batch_spec(t, d_model),
        out_shape=jax.ShapeDtypeStruct((bsz, t, d_model), BF16),
        scratch_shapes=[pltpu.VMEM((hq, kv_lora), BF16), pltpu.VMEM((hq, ROPE_DIM), BF16),
                        pltpu.VMEM((1, hq), F32), pltpu.VMEM((kv_lora + BF16_SUBLANES, hq), F32)],
        compiler_params=pltpu.CompilerParams(
            dimension_semantics=("parallel",),
            vmem_limit_bytes=_vmem_limit(w_bytes + blk_bytes + tmp_bytes)),
        name="cache_attention",
    )(q, lat_cache, kr_cache, lat_new, kr_new, ga, rg, *w_list)


def _mixer_out_kernel(x_ref, mix_ref, hist_ref, w_out_ref, g_ffn_ref, w_up_ref, fw_ref, fb_ref, w_down_ref,
                      y_ref, fst_ref, upc_ref, *, rows, nb, hp, d_model, d_ff, conv_w, col_blk):
    t = pl.program_id(1)

    @pl.when(t == 0)
    def _():
        upc_ref[0:hp, :] = hist_ref[...]

    x1 = x_ref[...] + _dot(mix_ref[...], w_out_ref[...])
    xn = (x1 * _rms_rows(x1, d_model) * g_ffn_ref[...]).astype(BF16)

    def up_proj(c):
        for c0 in (c * col_blk, d_ff + c * col_blk):
            sl = slice(c0, c0 + col_blk)
            upc_ref[hp:hp + rows, sl] = _dot(xn, w_up_ref[:, sl])

    def conv_cols(c0):
        sl = slice(c0, c0 + col_blk)
        out = fb_ref[:, sl] + fw_ref[conv_w - 1:conv_w, sl] * upc_ref[hp:hp + rows, sl]
        for j in range(1, conv_w):
            out = out + fw_ref[conv_w - 1 - j:conv_w - j, sl] * upc_ref[hp - j * nb:hp - j * nb + rows, sl]
        return out

    def gated(c):
        gate = conv_cols(c * col_blk)
        val = conv_cols(d_ff + c * col_blk)
        return (gate * _sigmoid(gate) * val).astype(BF16)

    def down_proj(c, hmid):
        return _dot(hmid, w_down_ref[c * col_blk:(c + 1) * col_blk, :])

    n_chunks = d_ff // col_blk
    y = x1
    up_proj(0)
    prev = None
    for c in range(n_chunks):
        if c + 1 < n_chunks:
            up_proj(c + 1)
        hmid = gated(c)
        if prev is not None:
            y = y + down_proj(c - 1, prev)
        prev = hmid
    y_ref[...] = y + down_proj(n_chunks - 1, prev)

    tail = upc_ref[rows:rows + hp, :]
    fst_ref[...] = tail
    upc_ref[0:hp, :] = tail


def _mixer_out(x, mixed, hist, wts, *, rows, nb):
    groups, tg, d_model = x.shape
    hp = hist.shape[1]
    d_ff = wts["w_down"].shape[0]
    conv_w = wts["fw"].shape[0]
    n_t = tg // rows
    col_blk = MXU_DIM

    def row_spec(width):
        return pl.BlockSpec((None, rows, width), lambda g, t: (g, t, 0))

    def group_spec(r, width):
        return pl.BlockSpec((None, r, width), lambda g, t: (g, 0, 0))

    w_names = ["w_out", "g_ffn", "w_up", "fw", "fb", "w_down"]
    w_list = [wts[n] for n in w_names]
    w_bytes = sum(w.size * w.dtype.itemsize for w in w_list)
    blk_bytes = 2 * rows * d_model * (4 + 2 + 4) + 4 * 4 * hp * 2 * d_ff
    tmp_bytes = 4 * (hp + rows) * 2 * d_ff + 4 * rows * (3 * d_model + 8 * col_blk)
    kern = functools.partial(_mixer_out_kernel, rows=rows, nb=nb, hp=hp, d_model=d_model, d_ff=d_ff,
                             conv_w=conv_w, col_blk=col_blk)
    return pl.pallas_call(
        kern,
        grid=(groups, n_t),
        in_specs=[row_spec(d_model), row_spec(d_model), group_spec(hp, 2 * d_ff)]
                 + [_const_spec(w.shape) for w in w_list],
        out_specs=(row_spec(d_model), group_spec(hp, 2 * d_ff)),
        out_shape=(jax.ShapeDtypeStruct((groups, tg, d_model), F32),
                   jax.ShapeDtypeStruct((groups, hp, 2 * d_ff), F32)),
        scratch_shapes=[pltpu.VMEM((hp + rows, 2 * d_ff), F32)],
        compiler_params=pltpu.CompilerParams(
            dimension_semantics=("parallel", "arbitrary"),
            vmem_limit_bytes=_vmem_limit(w_bytes + blk_bytes + tmp_bytes)),
        name="mixer_out",
    )(x, mixed, hist, *w_list)


def _head_pad(w, widths):
    lead = w.shape[:-1]
    per_head = sum(widths)
    w = w.reshape(lead + (N_HEADS, per_head))
    w = jnp.pad(w, [(0, 0)] * len(lead) + [(0, 0), (0, HEAD_PAD - per_head)])
    return w.reshape(lead + (N_HEADS * HEAD_PAD,))


def _prep_weights(l, g_mix_norm, w_in, g_q_a, w_q_b, g_kv_a, w_kv_b, g_qn, g_qr, g_kn, g_kr,
                  lru_conv_w, lru_conv_b, w_rg, b_rg, w_ig, b_ig, lru_lambda, w_out, g_ffn_norm,
                  w_up, ffn_conv_w, ffn_conv_b, w_down):
    d_model = w_in.shape[1]
    q_lora = w_q_b.shape[1]
    kv_lora = w_kv_b.shape[1]
    row = lambda a: a.reshape(1, -1)
    o_kr = q_lora + kv_lora
    wi = w_in[l].astype(BF16)
    w_kr = jnp.pad(wi[:, o_kr:o_kr + ROPE_DIM], ((0, 0), (NOPE_DIM, HEAD_PAD - QK_DIM)))
    w_in_p = jnp.concatenate([wi[:, :o_kr], w_kr, wi[:, o_kr + ROPE_DIM:]], axis=1)
    scale = QK_DIM ** -0.5 * LOG2_E
    gq = jnp.pad(jnp.concatenate([g_qn[l], g_qr[l]]) * scale, (0, HEAD_PAD - QK_DIM))
    g_rope = g_qr[l] * scale
    gqx = jnp.pad(jnp.concatenate([g_rope[HALF_ROPE:], g_rope[:HALF_ROPE]]), (NOPE_DIM, HEAD_PAD - QK_DIM))
    wq3 = w_q_b[l].reshape(q_lora, N_HEADS, QK_DIM)
    wqx = jnp.concatenate([jnp.zeros((q_lora, N_HEADS, NOPE_DIM), F32), wq3[:, :, NOPE_DIM + HALF_ROPE:],
                           wq3[:, :, NOPE_DIM:NOPE_DIM + HALF_ROPE]], axis=2).reshape(q_lora, N_HEADS * QK_DIM)
    gk = jnp.tile(jnp.pad(g_kn[l], (0, HEAD_PAD - NOPE_DIM)), N_HEADS)
    kv = w_kv_b[l].reshape(kv_lora, N_HEADS, NOPE_DIM + V_DIM)
    wk = _head_pad(kv[:, :, :NOPE_DIM].reshape(kv_lora, N_HEADS * NOPE_DIM), (NOPE_DIM,))
    wv = kv[:, :, NOPE_DIM:].reshape(kv_lora, N_HEADS * V_DIM)
    wkg = jnp.pad(jnp.transpose(kv[:, :, :NOPE_DIM], (1, 2, 0)) * g_kn[l][None, :, None],
                  ((0, 0), (0, HEAD_PAD - NOPE_DIM), (0, 0)))
    seg = jnp.pad(jnp.repeat(jnp.eye(N_HEADS, dtype=F32), NOPE_DIM, axis=0), ((0, 0), (0, LANES - N_HEADS)))
    return {
        "wkg": wkg.astype(BF16), "wkc": kv[:, :, :NOPE_DIM].reshape(kv_lora, -1).astype(BF16),
        "seg": seg.astype(BF16), "wv": wv.astype(BF16),
        "g_mix": row(g_mix_norm[l]), "w_in": w_in_p, "g_qa": row(g_q_a[l]),
        "wq": _head_pad(w_q_b[l], (NOPE_DIM, ROPE_DIM)).astype(BF16), "gq": row(gq), "gqx": row(gqx),
        "wqx": _head_pad(wqx, (NOPE_DIM, ROPE_DIM)).astype(BF16),
        "g_kva": row(g_kv_a[l]), "g_kr": row(jnp.pad(g_kr[l], (NOPE_DIM, HEAD_PAD - QK_DIM))),
        "wk": wk.astype(BF16), "gk": row(gk), "wvt": wv.T.astype(BF16),
        "cw": lru_conv_w[l], "cb": row(lru_conv_b[l]),
        "wlru": jnp.concatenate([w_rg[l], w_ig[l]], axis=-1).astype(BF16),
        "brg": row(b_rg[l]), "big": row(b_ig[l]), "lam": row(lru_lambda[l]),
        "w_out": w_out[l].astype(BF16), "g_ffn": row(g_ffn_norm[l]), "w_up": w_up[l].astype(BF16),
        "fw": ffn_conv_w[l], "fb": row(ffn_conv_b[l]), "w_down": w_down[l].astype(BF16),
    }


def _rope_tables(pos):
    inv = np.float32(ROPE_THETA) ** (-np.arange(0, ROPE_DIM, 2, dtype=np.float32) / np.float32(ROPE_DIM))
    ang = pos.astype(np.float32)[:, None] * inv[None, :]
    cos = np.cos(ang.astype(np.float64)).astype(np.float32)
    sin = np.sin(ang.astype(np.float64)).astype(np.float32)
    n = pos.shape[0]
    ones_lo = np.ones((n, NOPE_DIM), np.float32)
    zeros_lo = np.zeros((n, NOPE_DIM), np.float32)
    zeros_half = np.zeros((n, HALF_ROPE), np.float32)
    tail = np.zeros((n, HEAD_PAD - QK_DIM), np.float32)
    c = np.concatenate([ones_lo, cos, cos, tail], axis=1)
    s_lo = np.concatenate([zeros_lo, -sin, zeros_half, tail], axis=1)
    s_hi = np.concatenate([zeros_lo, zeros_half, sin, tail], axis=1)
    return jnp.asarray(c), jnp.asarray(s_lo), jnp.asarray(s_hi)


def _expand_matrix(t):
    one_part = jnp.pad(jnp.repeat(jnp.eye(N_HEADS, dtype=F32), t, axis=1), ((0, LANES - N_HEADS), (0, 0)))
    return jnp.concatenate([one_part, one_part], axis=0).astype(BF16)


def _front_pad_rows(a, hp):
    return jnp.pad(a, ((0, 0), (hp - a.shape[1], 0), (0, 0)))


def _layer_prompt(x, wts, *, rows, out_rows, attn_tq, attn_tk):
    bsz, t, d_model = x.shape
    lru_w = wts["cw"].shape[0]
    ffn_w = wts["fw"].shape[0]
    d_ff2 = wts["w_up"].shape[1]
    hp1 = _round_up(lru_w - 1, SUBLANES)
    hp2 = _round_up(ffn_w - 1, SUBLANES)
    rope = _rope_tables(np.arange(t, dtype=np.int32))
    q, k, vt, lat, kr, ga, rg, h_last, cst = _mixer_in(
        x, jnp.zeros((bsz, hp1, d_model), F32), jnp.zeros((bsz, 1, d_model), F32), rope, wts, rows=rows, nb=1)
    mixed = _attention(q, k, vt, ga, rg, tq=attn_tq, tk=attn_tk)
    y, fst = _mixer_out(x, mixed, jnp.zeros((bsz, hp2, d_ff2), F32), wts, rows=out_rows, nb=1)
    return y, (lat, kr, h_last[:, 0], cst[:, hp1 - (lru_w - 1):], fst[:, hp2 - (ffn_w - 1):])


def _layer_sample(x, past_lat, past_kr, h0, lru_buf, ffn_buf, wts, *, tk):
    bsz, t, d_model = x.shape
    past = past_lat.shape[1]
    lru_w = wts["cw"].shape[0]
    ffn_w = wts["fw"].shape[0]
    hp1 = _round_up((lru_w - 1) * bsz, SUBLANES)
    hp2 = _round_up((ffn_w - 1) * bsz, SUBLANES)
    rows = t * bsz

    def to_tm(a):
        return jnp.swapaxes(a, 0, 1).reshape(1, a.shape[1] * bsz, a.shape[2])

    def from_tm(a):
        return jnp.swapaxes(a.reshape(a.shape[1] // bsz, bsz, a.shape[2]), 0, 1)

    rope = _rope_tables(np.repeat(past + np.arange(t, dtype=np.int32), bsz))
    q, k, vt, lat, kr, ga, rg, h_last, cst = _mixer_in(
        to_tm(x), _front_pad_rows(to_tm(lru_buf), hp1), h0[None], rope, wts, rows=rows, nb=bsz)

    assert (past % CHUNK) + t <= CHUNK, "cache attention assumes all keys visible to all queries"
    n_pad = _round_up(t, LANES)
    pad_rows = lambda a: jnp.pad(from_tm(a), ((0, 0), (0, n_pad - t), (0, 0)))
    cache_wts = dict(wts, expand=_expand_matrix(t))
    mixed = _cache_attention(from_tm(q), past_lat, past_kr, pad_rows(lat), pad_rows(kr),
                             from_tm(ga), from_tm(rg), cache_wts, tk=tk, n_new=t)

    y, fst = _mixer_out(to_tm(x), to_tm(mixed), _front_pad_rows(to_tm(ffn_buf), hp2), wts, rows=rows, nb=bsz)
    states = (from_tm(lat), from_tm(kr), h_last[0],
              from_tm(cst[:, hp1 - (lru_w - 1) * bsz:]), from_tm(fst[:, hp2 - (ffn_w - 1) * bsz:]))
    return from_tm(y), states


def kernel(x_prompt, x_sample, cache_kv_latent, cache_k_rope, state_lru_h, state_lru_conv, state_ffn_conv,
           g_mix_norm, w_in, g_q_a, w_q_b, g_kv_a, w_kv_b, g_qn, g_qr, g_kn, g_kr, lru_conv_w, lru_conv_b,
           w_rg, b_rg, w_ig, b_ig, lru_lambda, w_out, g_ffn_norm, w_up, ffn_conv_w, ffn_conv_b, w_down):
    depth = w_in.shape[0]
    yp, ys = x_prompt, x_sample
    p_states, s_states = [], []
    for l in range(depth):
        wts = _prep_weights(l, g_mix_norm, w_in, g_q_a, w_q_b, g_kv_a, w_kv_b, g_qn, g_qr, g_kn, g_kr,
                            lru_conv_w, lru_conv_b, w_rg, b_rg, w_ig, b_ig, lru_lambda, w_out, g_ffn_norm,
                            w_up, ffn_conv_w, ffn_conv_b, w_down)
        yp, st_p = _layer_prompt(yp, wts, rows=512, out_rows=512, attn_tq=2048, attn_tk=512)
        ys, st_s = _layer_sample(ys, cache_kv_latent[l], cache_k_rope[l], state_lru_h[l],
                                 state_lru_conv[l], state_ffn_conv[l], wts, tk=1024)
        p_states.append(st_p)
        s_states.append(st_s)
    p_out = [jnp.stack([st[j] for st in p_states], axis=0) for j in range(5)]
    s_out = [jnp.stack([st[j] for st in s_states], axis=0) for j in range(5)]
    return (yp, ys, *p_out, *s_out)
```

```python
import functools

import jax
import jax.numpy as jnp
import numpy as np
from jax import lax
from jax.experimental import pallas as pl
from jax.experimental.pallas import tpu as pltpu

CHUNK = 64
CHUNK_SHIFT = CHUNK.bit_length() - 1
assert CHUNK == 1 << CHUNK_SHIFT
N_HEADS = 16
NOPE_DIM = 64
ROPE_DIM = 32
V_DIM = 64
QK_DIM = NOPE_DIM + ROPE_DIM
ROPE_THETA = 10000.0
RG_C = 8.0
EPS = 1e-6
LOG2_E = 1.4426950408889634

LANES = 128
SUBLANES = 8
BF16_SUBLANES = 16
MXU_DIM = 256
VMEM_BYTES_V7X = 64 * 1024 * 1024

HEAD_PAD = LANES
SLAB_SUB_ROWS = MXU_DIM
VALUE_SLAB_ROWS = 2 * MXU_DIM
HALF_ROPE = ROPE_DIM // 2
F32 = jnp.float32
BF16 = jnp.bfloat16


def _round_up(n, m):
    return (n + m - 1) // m * m


def _vmem_limit(nbytes):
    return int(min(2 * nbytes, VMEM_BYTES_V7X - 8 * 1024 * 1024))


def _const_spec(shape):
    nd = len(shape)
    return pl.BlockSpec(shape, lambda *_: (0,) * nd, pipeline_mode=pl.Buffered(1))


def _dot(a, b):
    return jnp.dot(a, b, preferred_element_type=F32)


def _dot_nt(a, b):
    return lax.dot_general(a, b, (((1,), (1,)), ((), ())), preferred_element_type=F32)


def _sigmoid(x):
    return 0.5 * jnp.tanh(0.5 * x) + 0.5


def _rms_rows(x, n):
    return lax.rsqrt(jnp.sum(x * x, axis=-1, keepdims=True) * (1.0 / n) + EPS)


def _rope_head(x, c, s_lo, s_hi):
    return (x * c + pltpu.roll(x, HALF_ROPE, 1) * s_hi
            + pltpu.roll(x, HEAD_PAD - HALF_ROPE, 1) * s_lo)


def _shift_rows(x, n, fill):
    rows, cols = x.shape
    if n % SUBLANES == 0:
        return jnp.concatenate([jnp.full((n, cols), fill, x.dtype), x[:rows - n]], axis=0)
    rolled = pltpu.roll(x, n, 0)
    row = lax.broadcasted_iota(jnp.int32, x.shape, 0)
    return jnp.where(row >= n, rolled, fill)


def _linear_scan(a, b, nb):
    rows = a.shape[0]
    s = nb
    while s < rows:
        b = a * _shift_rows(b, s, 0.0) + b
        if 2 * s < rows:
            a = a * _shift_rows(a, s, 1.0)
        s *= 2
    return b


def _mixer_in_kernel(*refs, rows, sub, **static):
    hist_ref, h0_ref = refs[1], refs[2]
    xh_ref, hprev_ref = refs[-2], refs[-1]

    @pl.when(pl.program_id(1) == 0)
    def _():
        xh_ref[...] = hist_ref[...]
        hprev_ref[...] = h0_ref[...]

    waiting = [_mixer_in_stages(*refs, r0=r0, sub=sub, **static) for r0 in range(0, rows, sub)]
    running = []
    while waiting or running:
        if waiting:
            running.append(waiting.pop(0))
        for stages in list(running):
            if next(stages, "done") == "done":
                running.remove(stages)


def _mixer_in_stages(x_ref, hist_ref, h0_ref, cos_ref, slo_ref, shi_ref,
                     g_mix_ref, w_in_ref, g_qa_ref, wq_ref, wqx_ref, gq_ref, gqx_ref,
                     g_kva_ref, g_kr_ref,
                     wk_ref, gk_ref, wvt_ref, cw_ref, cb_ref, wlru_ref, brg_ref, big_ref, lam_ref, *rest,
                     r0, sub, nb, hp, d_model, q_lora, kv_lora, conv_w):
    if nb == 1:
        perm_ref, *rest = rest
    q_ref, k_ref, vt_ref, lat_ref, kr_ref, ga_ref, rg_ref, hl_ref, cst_ref, xh_ref, hprev_ref = rest
    rs = slice(r0, r0 + sub)
    rows = sub

    o_kv = q_lora
    o_kr = o_kv + kv_lora
    o_u = o_kr + HEAD_PAD
    o_ga = o_u + d_model
    o_gb = o_ga + d_model
    cos = cos_ref[rs, :]
    s_lo = slo_ref[rs, :]
    s_hi = shi_ref[rs, :]

    x = x_ref[rs, :]
    xn = (x * _rms_rows(x, d_model) * g_mix_ref[...]).astype(BF16)
    yield

    xs = _dot(perm_ref[0], xn).astype(BF16) if nb == 1 else xn

    def in_proj(lhs, c0, c1):
        return _dot(lhs, w_in_ref[:, c0:c1])

    z_lat = in_proj(xn, 0, o_u)
    u = in_proj(xs, o_u, o_ga)
    yield

    cq = z_lat[:, 0:o_kv]
    ckv = z_lat[:, o_kv:o_kr]
    krb = z_lat[:, o_kr:o_u]
    lat = ckv * _rms_rows(ckv, kv_lora) * g_kva_ref[...]
    lat_ref[rs, :] = lat
    latb = lat.astype(BF16)
    kr = _rope_head(krb * _rms_rows(krb, ROPE_DIM) * g_kr_ref[...], cos, s_lo, s_hi)
    kr_ref[rs, :] = kr[:, NOPE_DIM:NOPE_DIM + ROPE_DIM]
    cqn = (cq * _rms_rows(cq, q_lora) * g_qa_ref[...]).astype(BF16)

    def q_heads():
        is_nope = lax.broadcasted_iota(jnp.int32, (rows, HEAD_PAD), 1) < NOPE_DIM
        root_n = jnp.where(is_nope, NOPE_DIM ** 0.5, ROPE_DIM ** 0.5)
        cg = cos * gq_ref[...] * root_n
        sg = (s_lo + s_hi) * gqx_ref[...] * root_n
        for h in range(N_HEADS):
            sl = slice(h * HEAD_PAD, (h + 1) * HEAD_PAD)
            qh = qp[:, sl]
            sq = qh * qh
            r_n = lax.rsqrt(jnp.sum(jnp.where(is_nope, sq, 0.0), axis=-1, keepdims=True) + NOPE_DIM * EPS)
            r_r = lax.rsqrt(jnp.sum(jnp.where(is_nope, 0.0, sq), axis=-1, keepdims=True) + ROPE_DIM * EPS)
            q_ref[rs, sl] = (jnp.where(is_nope, r_n, r_r) * (qh * cg + qp2[:, sl] * sg)).astype(BF16)

    def k_heads(kp):
        gk = gk_ref[:, 0:HEAD_PAD] * NOPE_DIM ** 0.5
        for h in range(N_HEADS):
            sl = slice(h * HEAD_PAD, (h + 1) * HEAD_PAD)
            kh = kp[:, sl]
            r_k = lax.rsqrt(jnp.sum(kh * kh, axis=-1, keepdims=True) + NOPE_DIM * EPS)
            k_ref[rs, sl] = (kh * r_k * gk + kr).astype(BF16)

    hgt = SUBLANES if nb == 1 else nb
    steps = rows // hgt
    x_slabs = {g: u[g * hgt:(g + 1) * hgt] for g in range(steps)}
    first_run = lax.broadcasted_iota(jnp.int32, (hgt, d_model), 0) == 0
    for j in range(1, conv_w):
        if nb == 1:
            x_slabs[-j] = jnp.where(first_run, xh_ref[hp - j:hp - j + 1, :], pltpu.roll(x_slabs[steps - j], 1, 0))
        else:
            x_slabs[-j] = xh_ref[hp - j * nb:hp - (j - 1) * nb, :]
    uc_slabs = []
    for g in range(steps):
        acc = cb_ref[...] + cw_ref[conv_w - 1:conv_w, :] * x_slabs[g]
        for j in range(1, conv_w):
            acc = acc + cw_ref[conv_w - 1 - j:conv_w - j, :] * x_slabs[g - j]
        uc_slabs.append(acc)
    u_c = jnp.concatenate(uc_slabs, axis=0)
    if nb == 1:
        row = lax.broadcasted_iota(jnp.int32, (hp, d_model), 0)
        tail = jnp.zeros((hp, d_model), F32)
        for j in range(1, conv_w):
            last = jnp.broadcast_to(x_slabs[steps - j][hgt - 1:hgt], (hp, d_model))
            tail = jnp.where(row == hp - j, last, tail)
    else:
        tail = u[rows - hp:rows]
    cst_ref[...] = tail
    xh_ref[...] = tail
    yield

    qp = _dot(cqn, wq_ref[...])
    qp2 = _dot(cqn, wqx_ref[...])
    n_blocks, blk_w, _ = wlru_ref.shape
    gate_proj = [_dot(u_c[:, n * blk_w:(n + 1) * blk_w].astype(BF16), wlru_ref[n])
                 for n in range(n_blocks)]
    yield

    q_heads()
    yield

    kp = _dot(latb, wk_ref[...])
    vslab = vt_ref.shape[-1]
    vt_ref[r0 // vslab, :, r0 % vslab:r0 % vslab + sub] = _dot_nt(wvt_ref[...], latb).astype(BF16)
    gate_a = in_proj(xn, o_ga, o_gb)
    gate_b = in_proj(xs, o_gb, o_gb + d_model)
    yield

    lam = lam_ref[...]
    softplus_neg = jnp.maximum(-lam, 0.0) + jnp.log1p(jnp.exp(-jnp.abs(lam)))
    a_parts, b_parts = [], []
    for n in range(n_blocks):
        sl = slice(n * blk_w, (n + 1) * blk_w)
        ucn = u_c[:, sl]
        g = gate_proj[n]
        r = _sigmoid(g[:, 0:blk_w] + brg_ref[:, sl])
        i = _sigmoid(g[:, blk_w:2 * blk_w] + big_ref[:, sl])
        log_a = (-RG_C * r) * softplus_neg[:, sl]
        a_n = jnp.exp(log_a)
        a_parts.append(a_n)
        b_parts.append(jnp.sqrt(-jnp.tanh(log_a) * (1.0 + a_n * a_n)) * (i * ucn))
    a = jnp.concatenate(a_parts, axis=1)
    b = jnp.concatenate(b_parts, axis=1)

    a_g = a[0:hgt]
    h_in = a_g * hprev_ref[...]
    if nb == 1:
        h_in = jnp.where(lax.broadcasted_iota(jnp.int32, h_in.shape, 0) == 0, h_in, 0.0)
    h = b[0:hgt] + h_in
    a_run = a_g
    h_slabs, a_slabs = [h], [a_run]
    for g in range(1, steps):
        a_g = a[g * hgt:(g + 1) * hgt]
        h = a_g * h + b[g * hgt:(g + 1) * hgt]
        h_slabs.append(h)
        if nb == 1:
            a_run = a_g * a_run
            a_slabs.append(a_run)
    if nb == 1:
        run_end = _linear_scan(a_run, h, 1)
        run_in = _shift_rows(run_end, 1, 0.0)
        h_slabs = [h_g + a_r * run_in for h_g, a_r in zip(h_slabs, a_slabs)]
        h_last = run_end[hgt - 1:hgt]
    else:
        h_last = h_slabs[-1]
    hprev_ref[...] = h_last
    hl_ref[...] = h_last
    rg = (_sigmoid(gate_b) * jnp.concatenate(h_slabs, axis=0)).astype(BF16)
    if nb == 1:
        rg = _dot(perm_ref[1], rg).astype(BF16)
    rg_ref[rs, :] = rg
    yield

    k_heads(kp)
    ga_ref[rs, :] = _sigmoid(gate_a).astype(BF16)


def _slab_permutation(rows):
    steps = rows // SUBLANES
    slab_row = jnp.arange(rows)
    time = (slab_row % SUBLANES) * steps + slab_row // SUBLANES
    gather = (time[:, None] == jnp.arange(rows)[None, :]).astype(BF16)
    return jnp.stack([gather, gather.T])


def _mixer_in(x, hist, h0, rope, wts, *, rows, nb):
    groups, tg, d_model = x.shape
    hp = hist.shape[1]
    q_lora = wts["wq"].shape[0]
    kv_lora = wts["wk"].shape[0]
    conv_w = wts["cw"].shape[0]
    n_t = tg // rows
    qk_w = N_HEADS * HEAD_PAD
    v_w = N_HEADS * V_DIM

    def row_spec(width):
        return pl.BlockSpec((None, rows, width), lambda g, t: (g, t, 0))

    def group_spec(r, width):
        return pl.BlockSpec((None, r, width), lambda g, t: (g, 0, 0))

    tab_spec = pl.BlockSpec((rows, HEAD_PAD), lambda g, t: (t, 0))
    w_names = ["g_mix", "w_in", "g_qa", "wq", "wqx", "gq", "gqx", "g_kva", "g_kr", "wk", "gk", "wvt",
               "cw", "cb", "wlru", "brg", "big", "lam"]
    sub = min(rows, SLAB_SUB_ROWS) if nb == 1 else rows
    vslab = min(rows, VALUE_SLAB_ROWS)
    assert rows % sub == 0 and rows % vslab == 0 and vslab % sub == 0
    w_list = [wts[n] for n in w_names] + ([_slab_permutation(sub)] if nb == 1 else [])
    in_specs = ([row_spec(d_model), group_spec(hp, d_model), group_spec(nb, d_model),
                 tab_spec, tab_spec, tab_spec] + [_const_spec(w.shape) for w in w_list])
    out_shape = (
        jax.ShapeDtypeStruct((groups, tg, qk_w), BF16),
        jax.ShapeDtypeStruct((groups, tg, qk_w), BF16),
        jax.ShapeDtypeStruct((groups, tg // vslab, v_w, vslab), BF16),
        jax.ShapeDtypeStruct((groups, tg, kv_lora), F32),
        jax.ShapeDtypeStruct((groups, tg, ROPE_DIM), F32),
        jax.ShapeDtypeStruct((groups, tg, d_model), BF16),
        jax.ShapeDtypeStruct((groups, tg, d_model), BF16),
        jax.ShapeDtypeStruct((groups, nb, d_model), F32),
        jax.ShapeDtypeStruct((groups, hp, d_model), F32),
    )
    vt_spec = pl.BlockSpec((None, rows // vslab, v_w, vslab), lambda g, t: (g, t, 0, 0))
    out_specs = (row_spec(qk_w), row_spec(qk_w), vt_spec, row_spec(kv_lora), row_spec(ROPE_DIM),
                 row_spec(d_model), row_spec(d_model), group_spec(nb, d_model), group_spec(hp, d_model))
    w_bytes = sum(w.size * w.dtype.itemsize for w in w_list)
    io_bytes = 2 * rows * (4 * d_model + 2 * (2 * qk_w + v_w + 2 * d_model) + 4 * (kv_lora + LANES) + 12 * LANES)
    tmp_bytes = 4 * rows * (wts["w_in"].shape[1] + 2 * qk_w + 6 * d_model) + 4 * (hp + rows) * d_model
    kern = functools.partial(_mixer_in_kernel, rows=rows, sub=sub, nb=nb, hp=hp, d_model=d_model,
                             q_lora=q_lora, kv_lora=kv_lora, conv_w=conv_w)
    return pl.pallas_call(
        kern,
        grid=(groups, n_t),
        in_specs=in_specs,
        out_specs=out_specs,
        out_shape=out_shape,
        scratch_shapes=[pltpu.VMEM((hp, d_model), F32), pltpu.VMEM((nb, d_model), F32)],
        compiler_params=pltpu.CompilerParams(
            dimension_semantics=("parallel", "arbitrary"),
            vmem_limit_bytes=_vmem_limit(w_bytes + io_bytes + tmp_bytes)),
        name="mixer_in",
    )(x, hist, h0, *rope, *w_list)


def _attn_kernel(q_ref, k_ref, vt_ref, ga_ref, rg_ref, o_ref, m_ref, acc_ref, carry_ref, *, tq, tk, cq, vblk):
    qi = pl.program_id(2)
    m_ref[...] = jnp.full(m_ref.shape, -jnp.inf, F32)
    acc_ref[...] = jnp.zeros(acc_ref.shape, F32)

    n_sub = tq // tk

    def wide(kj, b):
        return [(kj, hh, b * tk, tk, tk, None) for hh in range(2)]

    def narrow(kj, b):
        return [(kj, hh, b * tk + q0, cq, q0 + cq, q0) for hh in range(2) for q0 in range(0, tk, cq)]

    def scores(kj, hh, q0, qw, n_keys):
        lanes = slice(hh * HEAD_PAD, (hh + 1) * HEAD_PAD)
        k0 = pl.multiple_of(kj * tk, tk)
        return _dot_nt(k_ref[pl.ds(k0, n_keys), lanes], q_ref[q0:q0 + qw, lanes])

    def diag_mask(s, q0):
        k_chunk = lax.broadcasted_iota(jnp.int32, s.shape, 0) >> CHUNK_SHIFT
        q_chunk = (q0 + lax.broadcasted_iota(jnp.int32, s.shape, 1)) >> CHUNK_SHIFT
        return jnp.where(k_chunk <= q_chunk, s, -jnp.inf)

    def softmax(hh, q0, qw, s):
        qs = slice(q0, q0 + qw)
        m_prev = m_ref[hh, :, qs]
        m_new = jnp.maximum(m_prev, jnp.max(s, axis=0, keepdims=True))
        m_ref[hh, :, qs] = m_new
        return jnp.exp2(s - m_new).astype(BF16), jnp.exp2(m_prev - m_new)

    def values(kj, hh, q0, qw, pb, alpha):
        qs = slice(q0, q0 + qw)
        pv = None
        n_keys = pb.shape[0]
        piece = min(vblk, n_keys)
        for j0 in range(0, n_keys, piece):
            vt = vt_ref[kj * (tk // vblk) + j0 // vblk, hh * V_DIM:(hh + 1) * V_DIM, pl.ds(j0 % vblk, piece)]
            ones_rows = jnp.ones((acc_ref.shape[1] - V_DIM, piece), BF16)
            part = _dot(jnp.concatenate([vt, ones_rows], axis=0), pb[j0:j0 + piece])
            pv = part if pv is None else pv + part
        acc_ref[hh, :, qs] = alpha * acc_ref[hh, :, qs] + pv

    def run_chains(items, next_tile):
        _, _, _, qw0, n_keys0, _ = items[0]
        s_next = carry_ref[0:n_keys0, 0:qw0]
        prev = None
        for i, (kj, hh, q0, qw, n_keys, mask_q0) in enumerate(items):
            s_cur = s_next
            if i + 1 < len(items):
                s_next = scores(*items[i + 1][:5])
            elif next_tile is not None:
                carry_ref[...] = scores(*wide(next_tile, 0)[0][:5])
            if mask_q0 is not None:
                s_cur = diag_mask(s_cur, mask_q0)
            cur = softmax(hh, q0, qw, s_cur)
            if prev is not None:
                values(*items[i - 1][:4], *prev)
            prev = cur
        values(*items[-1][:4], *prev)

    carry_ref[...] = scores(*wide(0, 0)[0][:5])

    @pl.loop(0, qi)
    def _(j):
        run_chains([c for d in range(n_sub) for b in range(n_sub) for c in wide(n_sub * j + d, b)],
                   n_sub * (j + 1))

    diag_items = []
    for d in range(n_sub):
        kj = n_sub * qi + d
        diag_items += narrow(kj, d) + [c for b in range(d + 1, n_sub) for c in wide(kj, b)]
    run_chains(diag_items, None)

    attn_t = jnp.concatenate([acc_ref[hh, 0:V_DIM, :] / acc_ref[hh, V_DIM:V_DIM + 1, :] for hh in range(2)],
                             axis=0)
    o_ref[...] = (ga_ref[...].astype(F32) * attn_t.T + rg_ref[...].astype(F32)).astype(BF16)


def _attention(q, k, vt, ga, rg, *, tq, tk):
    bsz, t, _ = q.shape
    n_vb, _, vblk = vt.shape[1:]
    assert LANES == 2 * V_DIM and N_HEADS % 2 == 0
    assert k.shape[1] == t == n_vb * vblk and tk % vblk == 0 and t % tq == 0 and tq % tk == 0
    assert tk % CHUNK == 0 and tk % MXU_DIM == 0
    n_hp = N_HEADS // 2
    qk_blk = 2 * HEAD_PAD

    def q_index(b, p, qi):
        return (b, qi, p)

    kern = functools.partial(_attn_kernel, tq=tq, tk=tk, cq=MXU_DIM, vblk=vblk)
    blk_bytes = 2 * 2 * (tq * qk_blk + t * qk_blk + t * LANES + 3 * tq * LANES)
    tmp_bytes = 4 * 8 * tk * tk + 4 * 4 * tq * LANES
    return pl.pallas_call(
        kern,
        grid=(bsz, n_hp, t // tq),
        in_specs=[pl.BlockSpec((None, tq, qk_blk), q_index),
                  pl.BlockSpec((None, t, qk_blk), lambda b, p, qi: (b, 0, p)),
                  pl.BlockSpec((None, n_vb, LANES, vblk), lambda b, p, qi: (b, 0, p, 0)),
                  pl.BlockSpec((None, tq, LANES), q_index),
                  pl.BlockSpec((None, tq, LANES), q_index)],
        out_specs=pl.BlockSpec((None, tq, LANES), q_index),
        out_shape=jax.ShapeDtypeStruct((bsz, t, N_HEADS * V_DIM), BF16),
        scratch_shapes=[pltpu.VMEM((2, 1, tq), F32),
                        pltpu.VMEM((2, V_DIM + BF16_SUBLANES, tq), F32),
                        pltpu.VMEM((tk, tk), F32)],
        compiler_params=pltpu.CompilerParams(
            dimension_semantics=("parallel", "parallel", "arbitrary"),
            vmem_limit_bytes=_vmem_limit(blk_bytes + tmp_bytes)),
        name="attention",
    )(q, k, vt, ga, rg)


def _cache_attn_kernel(q_ref, latc_ref, krc_ref, latn_ref, krn_ref, ga_ref, rg_ref,
                       wkg_ref, wkc_ref, seg_ref, expand_ref, wv_ref, o_ref,
                       qabs_ref, qr_ref, m_ref, acc_ref, *, t, tk, n_new):
    past = latc_ref.shape[0]
    hq = N_HEADS * t
    for h in range(N_HEADS):
        qh = q_ref[:, h * HEAD_PAD:(h + 1) * HEAD_PAD]
        qabs_ref[h * t:(h + 1) * t, :] = _dot(qh, wkg_ref[h]).astype(BF16)
        qr_ref[h * t:(h + 1) * t, :] = qh[:, NOPE_DIM:QK_DIM]
    m_ref[...] = jnp.full(m_ref.shape, -jnp.inf, F32)
    acc_ref[...] = jnp.zeros(acc_ref.shape, F32)
    kv_lora = latc_ref.shape[1]

    def scores(lat, kr, n_valid):
        n = lat.shape[0]
        latb = lat.astype(BF16)
        kvk = _dot(latb, wkc_ref[...])
        ssum = _dot((kvk * kvk).astype(BF16), seg_ref[...])
        r = lax.rsqrt(ssum * (1.0 / NOPE_DIM) + EPS)
        r_hi = r.astype(BF16)
        r_lo = (r - r_hi.astype(F32)).astype(BF16)
        r_cols = _dot(jnp.concatenate([r_hi, r_lo], axis=1), expand_ref[...])
        s = _dot_nt(latb, qabs_ref[...]) * r_cols + _dot_nt(kr.astype(BF16), qr_ref[...])
        if n_valid < n:
            s = jnp.where(lax.broadcasted_iota(jnp.int32, (n, hq), 0) < n_valid, s, -jnp.inf)
        lat_t = jnp.concatenate([lat.T.astype(BF16), jnp.ones((acc_ref.shape[0] - kv_lora, n), BF16)], axis=0)
        return s, lat_t

    def accumulate(s, lat_t):
        m_prev = m_ref[...]
        m_new = jnp.maximum(m_prev, jnp.max(s, axis=0, keepdims=True))
        m_ref[...] = m_new
        p = jnp.exp2(s - m_new).astype(BF16)
        acc_ref[...] = jnp.exp2(m_prev - m_new) * acc_ref[...] + _dot(lat_t, p)

    def tile_scores(j):
        if j < past // tk:
            return scores(latc_ref[j * tk:(j + 1) * tk, :], krc_ref[j * tk:(j + 1) * tk, :], tk)
        return scores(latn_ref[...], krn_ref[...], n_new)

    n_tiles = past // tk + 1
    cur = tile_scores(0)
    for j in range(n_tiles):
        nxt = tile_scores(j + 1) if j + 1 < n_tiles else None
        accumulate(*cur)
        cur = nxt

    ctx = (acc_ref[0:kv_lora, :] / acc_ref[kv_lora:kv_lora + 1, :]).T.astype(BF16)
    first_head = lax.broadcasted_iota(jnp.int32, (t, LANES), 1) < V_DIM
    for pair in range(N_HEADS // 2):
        cols = slice(pair * LANES, (pair + 1) * LANES)
        wv = wv_ref[:, cols]
        a0 = _dot(ctx[(2 * pair) * t:(2 * pair + 1) * t], wv)
        a1 = _dot(ctx[(2 * pair + 1) * t:(2 * pair + 2) * t], wv)
        attn = jnp.where(first_head, a0, a1)
        o_ref[:, cols] = (ga_ref[:, cols].astype(F32) * attn + rg_ref[:, cols].astype(F32)).astype(BF16)


def _cache_attention(q, lat_cache, kr_cache, lat_new, kr_new, ga, rg, wts, *, tk, n_new):
    bsz, t, qk_w = q.shape
    past, kv_lora = lat_cache.shape[1:]
    n_pad = lat_new.shape[1]
    d_model = ga.shape[2]
    hq = N_HEADS * t
    assert past % tk == 0 and LANES == 2 * V_DIM
    w_list = [wts["wkg"], wts["wkc"], wts["seg"], wts["expand"], wts["wv"]]
    w_bytes = sum(w.size * w.dtype.itemsize for w in w_list)
    blk_bytes = 2 * (4 * (past + n_pad) * (kv_lora + LANES) + 2 * t * (qk_w + 3 * d_model))
    tmp_bytes = 4 * tk * (2 * N_HEADS * NOPE_DIM + 4 * hq)

    def batch_spec(r, width):
        return pl.BlockSpec((None, r, width), lambda b: (b, 0, 0))

    kern = functools.partial(_cache_attn_kernel, t=t, tk=tk, n_new=n_new)
    return pl.pallas_call(
        kern,
        grid=(bsz,),
        in_specs=[batch_spec(t, qk_w), batch_spec(past, kv_lora), batch_spec(past, ROPE_DIM),
                  batch_spec(n_pad, kv_lora), batch_spec(n_pad, ROPE_DIM),
                  batch_spec(t, d_model), batch_spec(t, d_model)] + [_const_spec(w.shape) for w in w_list],
        out_specs=batch_spec(t, d_model),
        out_shape=jax.ShapeDtypeStruct((bsz, t, d_model), BF16),
        scratch_shapes=[pltpu.VMEM((hq, kv_lora), BF16), pltpu.VMEM((hq, ROPE_DIM), BF16),
                        pltpu.VMEM((1, hq), F32), pltpu.VMEM((kv_lora + BF16_SUBLANES, hq), F32)],
        compiler_params=pltpu.CompilerParams(
            dimension_semantics=("parallel",),
            vmem_limit_bytes=_vmem_limit(w_bytes + blk_bytes + tmp_bytes)),
        name="cache_attention",
    )(q, lat_cache, kr_cache, lat_new, kr_new, ga, rg, *w_list)


def _mixer_out_kernel(x_ref, mix_ref, hist_ref, w_out_ref, g_ffn_ref, w_up_ref, fw_ref, fb_ref, w_down_ref,
                      y_ref, fst_ref, upc_ref, *, rows, nb, hp, d_model, d_ff, conv_w, col_blk):
    t = pl.program_id(1)

    @pl.when(t == 0)
    def _():
        upc_ref[0:hp, :] = hist_ref[...]

    x1 = x_ref[...] + _dot(mix_ref[...], w_out_ref[...])
    xn = (x1 * _rms_rows(x1, d_model) * g_ffn_ref[...]).astype(BF16)

    def up_proj(c):
        for c0 in (c * col_blk, d_ff + c * col_blk):
            sl = slice(c0, c0 + col_blk)
            upc_ref[hp:hp + rows, sl] = _dot(xn, w_up_ref[:, sl])

    def conv_cols(c0):
        sl = slice(c0, c0 + col_blk)
        out = fb_ref[:, sl] + fw_ref[conv_w - 1:conv_w, sl] * upc_ref[hp:hp + rows, sl]
        for j in range(1, conv_w):
            out = out + fw_ref[conv_w - 1 - j:conv_w - j, sl] * upc_ref[hp - j * nb:hp - j * nb + rows, sl]
        return out

    def gated(c):
        gate = conv_cols(c * col_blk)
        val = conv_cols(d_ff + c * col_blk)
        return (gate * _sigmoid(gate) * val).astype(BF16)

    def down_proj(c, hmid):
        return _dot(hmid, w_down_ref[c * col_blk:(c + 1) * col_blk, :])

    n_chunks = d_ff // col_blk
    y = x1
    up_proj(0)
    prev = None
    for c in range(n_chunks):
        if c + 1 < n_chunks:
            up_proj(c + 1)
        hmid = gated(c)
        if prev is not None:
            y = y + down_proj(c - 1, prev)
        prev = hmid
    y_ref[...] = y + down_proj(n_chunks - 1, prev)

    tail = upc_ref[rows:rows + hp, :]
    fst_ref[...] = tail
    upc_ref[0:hp, :] = tail


def _mixer_out(x, mixed, hist, wts, *, rows, nb):
    groups, tg, d_model = x.shape
    hp = hist.shape[1]
    d_ff = wts["w_down"].shape[0]
    conv_w = wts["fw"].shape[0]
    n_t = tg // rows
    col_blk = MXU_DIM

    def row_spec(width):
        return pl.BlockSpec((None, rows, width), lambda g, t: (g, t, 0))

    def group_spec(r, width):
        return pl.BlockSpec((None, r, width), lambda g, t: (g, 0, 0))

    w_names = ["w_out", "g_ffn", "w_up", "fw", "fb", "w_down"]
    w_list = [wts[n] for n in w_names]
    w_bytes = sum(w.size * w.dtype.itemsize for w in w_list)
    blk_bytes = 2 * rows * d_model * (4 + 2 + 4) + 4 * 4 * hp * 2 * d_ff
    tmp_bytes = 4 * (hp + rows) * 2 * d_ff + 4 * rows * (3 * d_model + 8 * col_blk)
    kern = functools.partial(_mixer_out_kernel, rows=rows, nb=nb, hp=hp, d_model=d_model, d_ff=d_ff,
                             conv_w=conv_w, col_blk=col_blk)
    return pl.pallas_call(
        kern,
        grid=(groups, n_t),
        in_specs=[row_spec(d_model), row_spec(d_model), group_spec(hp, 2 * d_ff)]
                 + [_const_spec(w.shape) for w in w_list],
        out_specs=(row_spec(d_model), group_spec(hp, 2 * d_ff)),
        out_shape=(jax.ShapeDtypeStruct((groups, tg, d_model), F32),
                   jax.ShapeDtypeStruct((groups, hp, 2 * d_ff), F32)),
        scratch_shapes=[pltpu.VMEM((hp + rows, 2 * d_ff), F32)],
        compiler_params=pltpu.CompilerParams(
            dimension_semantics=("parallel", "arbitrary"),
            vmem_limit_bytes=_vmem_limit(w_bytes + blk_bytes + tmp_bytes)),
        name="mixer_out",
    )(x, mixed, hist, *w_list)


def _head_pad(w, widths):
    lead = w.shape[:-1]
    per_head = sum(widths)
    w = w.reshape(lead + (N_HEADS, per_head))
    w = jnp.pad(w, [(0, 0)] * len(lead) + [(0, 0), (0, HEAD_PAD - per_head)])
    return w.reshape(lead + (N_HEADS * HEAD_PAD,))


def _prep_weights(l, g_mix_norm, w_in, g_q_a, w_q_b, g_kv_a, w_kv_b, g_qn, g_qr, g_kn, g_kr,
                  lru_conv_w, lru_conv_b, w_rg, b_rg, w_ig, b_ig, lru_lambda, w_out, g_ffn_norm,
                  w_up, ffn_conv_w, ffn_conv_b, w_down):
    d_model = w_in.shape[1]
    q_lora = w_q_b.shape[1]
    kv_lora = w_kv_b.shape[1]
    row = lambda a: a.reshape(1, -1)
    o_kr = q_lora + kv_lora
    wi = w_in[l].astype(BF16)
    w_kr = jnp.pad(wi[:, o_kr:o_kr + ROPE_DIM], ((0, 0), (NOPE_DIM, HEAD_PAD - QK_DIM)))
    w_in_p = jnp.concatenate([wi[:, :o_kr], w_kr, wi[:, o_kr + ROPE_DIM:]], axis=1)
    scale = QK_DIM ** -0.5 * LOG2_E
    gq = jnp.pad(jnp.concatenate([g_qn[l], g_qr[l]]) * scale, (0, HEAD_PAD - QK_DIM))
    g_rope = g_qr[l] * scale
    gqx = jnp.pad(jnp.concatenate([g_rope[HALF_ROPE:], g_rope[:HALF_ROPE]]), (NOPE_DIM, HEAD_PAD - QK_DIM))
    wq3 = w_q_b[l].reshape(q_lora, N_HEADS, QK_DIM)
    wqx = jnp.concatenate([jnp.zeros((q_lora, N_HEADS, NOPE_DIM), F32), wq3[:, :, NOPE_DIM + HALF_ROPE:],
                           wq3[:, :, NOPE_DIM:NOPE_DIM + HALF_ROPE]], axis=2).reshape(q_lora, N_HEADS * QK_DIM)
    gk = jnp.tile(jnp.pad(g_kn[l], (0, HEAD_PAD - NOPE_DIM)), N_HEADS)
    kv = w_kv_b[l].reshape(kv_lora, N_HEADS, NOPE_DIM + V_DIM)
    wk = _head_pad(kv[:, :, :NOPE_DIM].reshape(kv_lora, N_HEADS * NOPE_DIM), (NOPE_DIM,))
    wv = kv[:, :, NOPE_DIM:].reshape(kv_lora, N_HEADS * V_DIM)
    wkg = jnp.pad(jnp.transpose(kv[:, :, :NOPE_DIM], (1, 2, 0)) * g_kn[l][None, :, None],
                  ((0, 0), (0, HEAD_PAD - NOPE_DIM), (0, 0)))
    seg = jnp.pad(jnp.repeat(jnp.eye(N_HEADS, dtype=F32), NOPE_DIM, axis=0), ((0, 0), (0, LANES - N_HEADS)))
    return {
        "wkg": wkg.astype(BF16), "wkc": kv[:, :, :NOPE_DIM].reshape(kv_lora, -1).astype(BF16),
        "seg": seg.astype(BF16), "wv": wv.astype(BF16),
        "g_mix": row(g_mix_norm[l]), "w_in": w_in_p, "g_qa": row(g_q_a[l]),
        "wq": _head_pad(w_q_b[l], (NOPE_DIM, ROPE_DIM)).astype(BF16), "gq": row(gq), "gqx": row(gqx),
        "wqx": _head_pad(wqx, (NOPE_DIM, ROPE_DIM)).astype(BF16),
        "g_kva": row(g_kv_a[l]), "g_kr": row(jnp.pad(g_kr[l], (NOPE_DIM, HEAD_PAD - QK_DIM))),
        "wk": wk.astype(BF16), "gk": row(gk), "wvt": wv.T.astype(BF16),
        "cw": lru_conv_w[l], "cb": row(lru_conv_b[l]),
        "wlru": jnp.concatenate([w_rg[l], w_ig[l]], axis=-1).astype(BF16),
        "brg": row(b_rg[l]), "big": row(b_ig[l]), "lam": row(lru_lambda[l]),
        "w_out": w_out[l].astype(BF16), "g_ffn": row(g_ffn_norm[l]), "w_up": w_up[l].astype(BF16),
        "fw": ffn_conv_w[l], "fb": row(ffn_conv_b[l]), "w_down": w_down[l].astype(BF16),
    }


def _rope_tables(pos):
    inv = np.float32(ROPE_THETA) ** (-np.arange(0, ROPE_DIM, 2, dtype=np.float32) / np.float32(ROPE_DIM))
    ang = pos.astype(np.float32)[:, None] * inv[None, :]
    cos = np.cos(ang.astype(np.float64)).astype(np.float32)
    sin = np.sin(ang.astype(np.float64)).astype(np.float32)
    n = pos.shape[0]
    ones_lo = np.ones((n, NOPE_DIM), np.float32)
    zeros_lo = np.zeros((n, NOPE_DIM), np.float32)
    zeros_half = np.zeros((n, HALF_ROPE), np.float32)
    tail = np.zeros((n, HEAD_PAD - QK_DIM), np.float32)
    c = np.concatenate([ones_lo, cos, cos, tail], axis=1)
    s_lo = np.concatenate([zeros_lo, -sin, zeros_half, tail], axis=1)
    s_hi = np.concatenate([zeros_lo, zeros_half, sin, tail], axis=1)
    return jnp.asarray(c), jnp.asarray(s_lo), jnp.asarray(s_hi)


def _expand_matrix(t):
    one_part = jnp.pad(jnp.repeat(jnp.eye(N_HEADS, dtype=F32), t, axis=1), ((0, LANES - N_HEADS), (0, 0)))
    return jnp.concatenate([one_part, one_part], axis=0).astype(BF16)


def _front_pad_rows(a, hp):
    return jnp.pad(a, ((0, 0), (hp - a.shape[1], 0), (0, 0)))


def _layer_prompt(x, wts, *, rows, out_rows, attn_tq, attn_tk):
    bsz, t, d_model = x.shape
    lru_w = wts["cw"].shape[0]
    ffn_w = wts["fw"].shape[0]
    d_ff2 = wts["w_up"].shape[1]
    hp1 = _round_up(lru_w - 1, SUBLANES)
    hp2 = _round_up(ffn_w - 1, SUBLANES)
    rope = _rope_tables(np.arange(t, dtype=np.int32))
    q, k, vt, lat, kr, ga, rg, h_last, cst = _mixer_in(
        x, jnp.zeros((bsz, hp1, d_model), F32), jnp.zeros((bsz, 1, d_model), F32), rope, wts, rows=rows, nb=1)
    mixed = _attention(q, k, vt, ga, rg, tq=attn_tq, tk=attn_tk)
    y, fst = _mixer_out(x, mixed, jnp.zeros((bsz, hp2, d_ff2), F32), wts, rows=out_rows, nb=1)
    return y, (lat, kr, h_last[:, 0], cst[:, hp1 - (lru_w - 1):], fst[:, hp2 - (ffn_w - 1):])


def _layer_sample(x, past_lat, past_kr, h0, lru_buf, ffn_buf, wts, *, tk):
    bsz, t, d_model = x.shape
    past = past_lat.shape[1]
    lru_w = wts["cw"].shape[0]
    ffn_w = wts["fw"].shape[0]
    hp1 = _round_up((lru_w - 1) * bsz, SUBLANES)
    hp2 = _round_up((ffn_w - 1) * bsz, SUBLANES)
    rows = t * bsz

    def to_tm(a):
        return jnp.swapaxes(a, 0, 1).reshape(1, a.shape[1] * bsz, a.shape[2])

    def from_tm(a):
        return jnp.swapaxes(a.reshape(a.shape[1] // bsz, bsz, a.shape[2]), 0, 1)

    rope = _rope_tables(np.repeat(past + np.arange(t, dtype=np.int32), bsz))
    q, k, vt, lat, kr, ga, rg, h_last, cst = _mixer_in(
        to_tm(x), _front_pad_rows(to_tm(lru_buf), hp1), h0[None], rope, wts, rows=rows, nb=bsz)

    assert (past % CHUNK) + t <= CHUNK, "cache attention assumes all keys visible to all queries"
    n_pad = _round_up(t, LANES)
    pad_rows = lambda a: jnp.pad(from_tm(a), ((0, 0), (0, n_pad - t), (0, 0)))
    cache_wts = dict(wts, expand=_expand_matrix(t))
    mixed = _cache_attention(from_tm(q), past_lat, past_kr, pad_rows(lat), pad_rows(kr),
                             from_tm(ga), from_tm(rg), cache_wts, tk=tk, n_new=t)

    y, fst = _mixer_out(to_tm(x), to_tm(mixed), _front_pad_rows(to_tm(ffn_buf), hp2), wts, rows=rows, nb=bsz)
    states = (from_tm(lat), from_tm(kr), h_last[0],
              from_tm(cst[:, hp1 - (lru_w - 1) * bsz:]), from_tm(fst[:, hp2 - (ffn_w - 1) * bsz:]))
    return from_tm(y), states


def kernel(x_prompt, x_sample, cache_kv_latent, cache_k_rope, state_lru_h, state_lru_conv, state_ffn_conv,
           g_mix_norm, w_in, g_q_a, w_q_b, g_kv_a, w_kv_b, g_qn, g_qr, g_kn, g_kr, lru_conv_w, lru_conv_b,
           w_rg, b_rg, w_ig, b_ig, lru_lambda, w_out, g_ffn_norm, w_up, ffn_conv_w, ffn_conv_b, w_down):
    depth = w_in.shape[0]
    yp, ys = x_prompt, x_sample
    p_states, s_states = [], []
    for l in range(depth):
        wts = _prep_weights(l, g_mix_norm, w_in, g_q_a, w_q_b, g_kv_a, w_kv_b, g_qn, g_qr, g_kn, g_kr,
                            lru_conv_w, lru_conv_b, w_rg, b_rg, w_ig, b_ig, lru_lambda, w_out, g_ffn_norm,
                            w_up, ffn_conv_w, ffn_conv_b, w_down)
        yp, st_p = _layer_prompt(yp, wts, rows=512, out_rows=512, attn_tq=4096, attn_tk=512)
        ys, st_s = _layer_sample(ys, cache_kv_latent[l], cache_k_rope[l], state_lru_h[l],
                                 state_lru_conv[l], state_ffn_conv[l], wts, tk=1024)
        p_states.append(st_p)
        s_states.append(st_s)
    p_out = [jnp.stack([st[j] for st in p_states], axis=0) for j in range(5)]
    s_out = [jnp.stack([st[j] for st in s_states], axis=0) for j in range(5)]
    return (yp, ys, *p_out, *s_out)
```

```python
import functools

import jax
import jax.numpy as jnp
import numpy as np
from jax import lax
from jax.experimental import pallas as pl
from jax.experimental.pallas import tpu as pltpu

CHUNK = 64
CHUNK_SHIFT = CHUNK.bit_length() - 1
assert CHUNK == 1 << CHUNK_SHIFT
N_HEADS = 16
NOPE_DIM = 64
ROPE_DIM = 32
V_DIM = 64
QK_DIM = NOPE_DIM + ROPE_DIM
ROPE_THETA = 10000.0
RG_C = 8.0
EPS = 1e-6
LOG2_E = 1.4426950408889634

LANES = 128
SUBLANES = 8
BF16_SUBLANES = 16
MXU_DIM = 256
VMEM_BYTES_V7X = 64 * 1024 * 1024

HEAD_PAD = LANES
SLAB_SUB_ROWS = MXU_DIM
VALUE_SLAB_ROWS = 2 * MXU_DIM
HALF_ROPE = ROPE_DIM // 2
F32 = jnp.float32
BF16 = jnp.bfloat16


def _round_up(n, m):
    return (n + m - 1) // m * m


def _vmem_limit(nbytes):
    return int(min(2 * nbytes, VMEM_BYTES_V7X - 8 * 1024 * 1024))


def _const_spec(shape):
    nd = len(shape)
    return pl.BlockSpec(shape, lambda *_: (0,) * nd, pipeline_mode=pl.Buffered(1))


def _dot(a, b):
    return jnp.dot(a, b, preferred_element_type=F32)


def _dot_nt(a, b):
    return lax.dot_general(a, b, (((1,), (1,)), ((), ())), preferred_element_type=F32)


def _sigmoid(x):
    return 0.5 * jnp.tanh(0.5 * x) + 0.5


def _rms_rows(x, n):
    return lax.rsqrt(jnp.sum(x * x, axis=-1, keepdims=True) * (1.0 / n) + EPS)


def _rope_head(x, c, s_lo, s_hi):
    return (x * c + pltpu.roll(x, HALF_ROPE, 1) * s_hi
            + pltpu.roll(x, HEAD_PAD - HALF_ROPE, 1) * s_lo)


def _shift_rows(x, n, fill):
    rows, cols = x.shape
    if n % SUBLANES == 0:
        return jnp.concatenate([jnp.full((n, cols), fill, x.dtype), x[:rows - n]], axis=0)
    rolled = pltpu.roll(x, n, 0)
    row = lax.broadcasted_iota(jnp.int32, x.shape, 0)
    return jnp.where(row >= n, rolled, fill)


def _linear_scan(a, b, nb):
    rows = a.shape[0]
    s = nb
    while s < rows:
        b = a * _shift_rows(b, s, 0.0) + b
        if 2 * s < rows:
            a = a * _shift_rows(a, s, 1.0)
        s *= 2
    return b


def _mixer_in_kernel(*refs, rows, sub, **static):
    hist_ref, h0_ref = refs[1], refs[2]
    xh_ref, hprev_ref = refs[-2], refs[-1]

    @pl.when(pl.program_id(1) == 0)
    def _():
        xh_ref[...] = hist_ref[...]
        hprev_ref[...] = h0_ref[...]

    waiting = [_mixer_in_stages(*refs, r0=r0, sub=sub, **static) for r0 in range(0, rows, sub)]
    running = []
    while waiting or running:
        if waiting:
            running.append(waiting.pop(0))
        for stages in list(running):
            if next(stages, "done") == "done":
                running.remove(stages)


def _mixer_in_stages(x_ref, hist_ref, h0_ref, cos_ref, slo_ref, shi_ref,
                     g_mix_ref, w_in_ref, g_qa_ref, wq_ref, wqx_ref, gq_ref, gqx_ref,
                     g_kva_ref, g_kr_ref,
                     wk_ref, gk_ref, wvt_ref, cw_ref, cb_ref, wlru_ref, brg_ref, big_ref, lam_ref, *rest,
                     r0, sub, nb, hp, d_model, q_lora, kv_lora, conv_w):
    if nb == 1:
        perm_ref, *rest = rest
    q_ref, k_ref, vt_ref, lat_ref, kr_ref, ga_ref, rg_ref, hl_ref, cst_ref, xh_ref, hprev_ref = rest
    rs = slice(r0, r0 + sub)
    rows = sub

    o_kv = q_lora
    o_kr = o_kv + kv_lora
    o_u = o_kr + HEAD_PAD
    o_ga = o_u + d_model
    o_gb = o_ga + d_model
    cos = cos_ref[rs, :]
    s_lo = slo_ref[rs, :]
    s_hi = shi_ref[rs, :]

    x = x_ref[rs, :]
    xn = (x * _rms_rows(x, d_model) * g_mix_ref[...]).astype(BF16)
    yield

    xs = _dot(perm_ref[0], xn).astype(BF16) if nb == 1 else xn

    def in_proj(lhs, c0, c1):
        return _dot(lhs, w_in_ref[:, c0:c1])

    z_lat = in_proj(xn, 0, o_u)
    u = in_proj(xs, o_u, o_ga)
    yield

    cq = z_lat[:, 0:o_kv]
    ckv = z_lat[:, o_kv:o_kr]
    krb = z_lat[:, o_kr:o_u]
    lat = ckv * _rms_rows(ckv, kv_lora) * g_kva_ref[...]
    lat_ref[rs, :] = lat
    latb = lat.astype(BF16)
    kr = _rope_head(krb * _rms_rows(krb, ROPE_DIM) * g_kr_ref[...], cos, s_lo, s_hi)
    kr_ref[rs, :] = kr[:, NOPE_DIM:NOPE_DIM + ROPE_DIM]
    cqn = (cq * _rms_rows(cq, q_lora) * g_qa_ref[...]).astype(BF16)

    def q_heads():
        is_nope = lax.broadcasted_iota(jnp.int32, (rows, HEAD_PAD), 1) < NOPE_DIM
        root_n = jnp.where(is_nope, NOPE_DIM ** 0.5, ROPE_DIM ** 0.5)
        cg = cos * gq_ref[...] * root_n
        sg = (s_lo + s_hi) * gqx_ref[...] * root_n
        for h in range(N_HEADS):
            sl = slice(h * HEAD_PAD, (h + 1) * HEAD_PAD)
            qh = qp[:, sl]
            sq = qh * qh
            r_n = lax.rsqrt(jnp.sum(jnp.where(is_nope, sq, 0.0), axis=-1, keepdims=True) + NOPE_DIM * EPS)
            r_r = lax.rsqrt(jnp.sum(jnp.where(is_nope, 0.0, sq), axis=-1, keepdims=True) + ROPE_DIM * EPS)
            q_ref[rs, sl] = (jnp.where(is_nope, r_n, r_r) * (qh * cg + qp2[:, sl] * sg)).astype(BF16)

    def k_heads(kp):
        gk = gk_ref[:, 0:HEAD_PAD] * NOPE_DIM ** 0.5
        for h in range(N_HEADS):
            sl = slice(h * HEAD_PAD, (h + 1) * HEAD_PAD)
            kh = kp[:, sl]
            r_k = lax.rsqrt(jnp.sum(kh * kh, axis=-1, keepdims=True) + NOPE_DIM * EPS)
            k_ref[rs, sl] = (kh * r_k * gk + kr).astype(BF16)

    hgt = SUBLANES if nb == 1 else nb
    steps = rows // hgt
    x_slabs = {g: u[g * hgt:(g + 1) * hgt] for g in range(steps)}
    first_run = lax.broadcasted_iota(jnp.int32, (hgt, d_model), 0) == 0
    for j in range(1, conv_w):
        if nb == 1:
            x_slabs[-j] = jnp.where(first_run, xh_ref[hp - j:hp - j + 1, :], pltpu.roll(x_slabs[steps - j], 1, 0))
        else:
            x_slabs[-j] = xh_ref[hp - j * nb:hp - (j - 1) * nb, :]
    uc_slabs = []
    for g in range(steps):
        acc = cb_ref[...] + cw_ref[conv_w - 1:conv_w, :] * x_slabs[g]
        for j in range(1, conv_w):
            acc = acc + cw_ref[conv_w - 1 - j:conv_w - j, :] * x_slabs[g - j]
        uc_slabs.append(acc)
    u_c = jnp.concatenate(uc_slabs, axis=0)
    if nb == 1:
        row = lax.broadcasted_iota(jnp.int32, (hp, d_model), 0)
        tail = jnp.zeros((hp, d_model), F32)
        for j in range(1, conv_w):
            last = jnp.broadcast_to(x_slabs[steps - j][hgt - 1:hgt], (hp, d_model))
            tail = jnp.where(row == hp - j, last, tail)
    else:
        tail = u[rows - hp:rows]
    cst_ref[...] = tail
    xh_ref[...] = tail
    yield

    qp = _dot(cqn, wq_ref[...])
    qp2 = _dot(cqn, wqx_ref[...])
    n_blocks, blk_w, _ = wlru_ref.shape
    gate_proj = [_dot(u_c[:, n * blk_w:(n + 1) * blk_w].astype(BF16), wlru_ref[n])
                 for n in range(n_blocks)]
    yield

    q_heads()
    yield

    kp = _dot(latb, wk_ref[...])
    vslab = vt_ref.shape[-1]
    vt_ref[r0 // vslab, :, r0 % vslab:r0 % vslab + sub] = _dot_nt(wvt_ref[...], latb).astype(BF16)
    gate_a = in_proj(xn, o_ga, o_gb)
    gate_b = in_proj(xs, o_gb, o_gb + d_model)
    yield

    lam = lam_ref[...]
    softplus_neg = jnp.maximum(-lam, 0.0) + jnp.log1p(jnp.exp(-jnp.abs(lam)))
    a_parts, b_parts = [], []
    for n in range(n_blocks):
        sl = slice(n * blk_w, (n + 1) * blk_w)
        ucn = u_c[:, sl]
        g = gate_proj[n]
        r = _sigmoid(g[:, 0:blk_w] + brg_ref[:, sl])
        i = _sigmoid(g[:, blk_w:2 * blk_w] + big_ref[:, sl])
        log_a = (-RG_C * r) * softplus_neg[:, sl]
        a_n = jnp.exp(log_a)
        a_parts.append(a_n)
        b_parts.append(jnp.sqrt(-jnp.tanh(log_a) * (1.0 + a_n * a_n)) * (i * ucn))
    a = jnp.concatenate(a_parts, axis=1)
    b = jnp.concatenate(b_parts, axis=1)

    a_g = a[0:hgt]
    h_in = a_g * hprev_ref[...]
    if nb == 1:
        h_in = jnp.where(lax.broadcasted_iota(jnp.int32, h_in.shape, 0) == 0, h_in, 0.0)
    h = b[0:hgt] + h_in
    a_run = a_g
    h_slabs, a_slabs = [h], [a_run]
    for g in range(1, steps):
        a_g = a[g * hgt:(g + 1) * hgt]
        h = a_g * h + b[g * hgt:(g + 1) * hgt]
        h_slabs.append(h)
        if nb == 1:
            a_run = a_g * a_run
            a_slabs.append(a_run)
    if nb == 1:
        run_end = _linear_scan(a_run, h, 1)
        run_in = _shift_rows(run_end, 1, 0.0)
        h_slabs = [h_g + a_r * run_in for h_g, a_r in zip(h_slabs, a_slabs)]
        h_last = run_end[hgt - 1:hgt]
    else:
        h_last = h_slabs[-1]
    hprev_ref[...] = h_last
    hl_ref[...] = h_last
    rg = (_sigmoid(gate_b) * jnp.concatenate(h_slabs, axis=0)).astype(BF16)
    if nb == 1:
        rg = _dot(perm_ref[1], rg).astype(BF16)
    rg_ref[rs, :] = rg
    yield

    k_heads(kp)
    ga_ref[rs, :] = _sigmoid(gate_a).astype(BF16)


def _slab_permutation(rows):
    steps = rows // SUBLANES
    slab_row = jnp.arange(rows)
    time = (slab_row % SUBLANES) * steps + slab_row // SUBLANES
    gather = (time[:, None] == jnp.arange(rows)[None, :]).astype(BF16)
    return jnp.stack([gather, gather.T])


def _mixer_in(x, hist, h0, rope, wts, *, rows, nb):
    groups, tg, d_model = x.shape
    hp = hist.shape[1]
    q_lora = wts["wq"].shape[0]
    kv_lora = wts["wk"].shape[0]
    conv_w = wts["cw"].shape[0]
    n_t = tg // rows
    qk_w = N_HEADS * HEAD_PAD
    v_w = N_HEADS * V_DIM

    def row_spec(width):
        return pl.BlockSpec((None, rows, width), lambda g, t: (g, t, 0))

    def group_spec(r, width):
        return pl.BlockSpec((None, r, width), lambda g, t: (g, 0, 0))

    tab_spec = pl.BlockSpec((rows, HEAD_PAD), lambda g, t: (t, 0))
    w_names = ["g_mix", "w_in", "g_qa", "wq", "wqx", "gq", "gqx", "g_kva", "g_kr", "wk", "gk", "wvt",
               "cw", "cb", "wlru", "brg", "big", "lam"]
    sub = min(rows, SLAB_SUB_ROWS) if nb == 1 else rows
    vslab = min(rows, VALUE_SLAB_ROWS)
    assert rows % sub == 0 and rows % vslab == 0 and vslab % sub == 0
    w_list = [wts[n] for n in w_names] + ([_slab_permutation(sub)] if nb == 1 else [])
    in_specs = ([row_spec(d_model), group_spec(hp, d_model), group_spec(nb, d_model),
                 tab_spec, tab_spec, tab_spec] + [_const_spec(w.shape) for w in w_list])
    out_shape = (
        jax.ShapeDtypeStruct((groups, tg, qk_w), BF16),
        jax.ShapeDtypeStruct((groups, tg, qk_w), BF16),
        jax.ShapeDtypeStruct((groups, tg // vslab, v_w, vslab), BF16),
        jax.ShapeDtypeStruct((groups, tg, kv_lora), F32),
        jax.ShapeDtypeStruct((groups, tg, ROPE_DIM), F32),
        jax.ShapeDtypeStruct((groups, tg, d_model), BF16),
        jax.ShapeDtypeStruct((groups, tg, d_model), BF16),
        jax.ShapeDtypeStruct((groups, nb, d_model), F32),
        jax.ShapeDtypeStruct((groups, hp, d_model), F32),
    )
    vt_spec = pl.BlockSpec((None, rows // vslab, v_w, vslab), lambda g, t: (g, t, 0, 0))
    out_specs = (row_spec(qk_w), row_spec(qk_w), vt_spec, row_spec(kv_lora), row_spec(ROPE_DIM),
                 row_spec(d_model), row_spec(d_model), group_spec(nb, d_model), group_spec(hp, d_model))
    w_bytes = sum(w.size * w.dtype.itemsize for w in w_list)
    io_bytes = 2 * rows * (4 * d_model + 2 * (2 * qk_w + v_w + 2 * d_model) + 4 * (kv_lora + LANES) + 12 * LANES)
    tmp_bytes = 4 * rows * (wts["w_in"].shape[1] + 2 * qk_w + 6 * d_model) + 4 * (hp + rows) * d_model
    kern = functools.partial(_mixer_in_kernel, rows=rows, sub=sub, nb=nb, hp=hp, d_model=d_model,
                             q_lora=q_lora, kv_lora=kv_lora, conv_w=conv_w)
    return pl.pallas_call(
        kern,
        grid=(groups, n_t),
        in_specs=in_specs,
        out_specs=out_specs,
        out_shape=out_shape,
        scratch_shapes=[pltpu.VMEM((hp, d_model), F32), pltpu.VMEM((nb, d_model), F32)],
        compiler_params=pltpu.CompilerParams(
            dimension_semantics=("parallel", "arbitrary"),
            vmem_limit_bytes=_vmem_limit(w_bytes + io_bytes + tmp_bytes)),
        name="mixer_in",
    )(x, hist, h0, *rope, *w_list)


def _attn_kernel(q_ref, k_ref, vt_ref, ga_ref, rg_ref, o_ref, m_ref, acc_ref, carry_ref, *, tq, tk, cq, vblk):
    qi = pl.program_id(2)
    m_ref[...] = jnp.full(m_ref.shape, -jnp.inf, F32)
    acc_ref[...] = jnp.zeros(acc_ref.shape, F32)

    n_sub = tq // tk

    def wide(kj, b):
        return [(kj, hh, b * tk, tk, tk, None) for hh in range(2)]

    def narrow(kj, b):
        return [(kj, hh, b * tk + q0, cq, q0 + cq, q0) for hh in range(2) for q0 in range(0, tk, cq)]

    def scores(kj, hh, q0, qw, n_keys):
        lanes = slice(hh * HEAD_PAD, (hh + 1) * HEAD_PAD)
        k0 = pl.multiple_of(kj * tk, tk)
        return _dot_nt(k_ref[pl.ds(k0, n_keys), lanes], q_ref[q0:q0 + qw, lanes])

    def diag_mask(s, q0):
        k_chunk = lax.broadcasted_iota(jnp.int32, s.shape, 0) >> CHUNK_SHIFT
        q_chunk = (q0 + lax.broadcasted_iota(jnp.int32, s.shape, 1)) >> CHUNK_SHIFT
        return jnp.where(k_chunk <= q_chunk, s, -jnp.inf)

    def softmax(hh, q0, qw, s):
        qs = slice(q0, q0 + qw)
        m_prev = m_ref[hh, :, qs]
        m_new = jnp.maximum(m_prev, jnp.max(s, axis=0, keepdims=True))
        m_ref[hh, :, qs] = m_new
        return jnp.exp2(s - m_new).astype(BF16), jnp.exp2(m_prev - m_new)

    def values(kj, hh, q0, qw, pb, alpha):
        qs = slice(q0, q0 + qw)
        pv = None
        n_keys = pb.shape[0]
        piece = min(vblk, n_keys)
        for j0 in range(0, n_keys, piece):
            vt = vt_ref[kj * (tk // vblk) + j0 // vblk, hh * V_DIM:(hh + 1) * V_DIM, pl.ds(j0 % vblk, piece)]
            ones_rows = jnp.ones((acc_ref.shape[1] - V_DIM, piece), BF16)
            part = _dot(jnp.concatenate([vt, ones_rows], axis=0), pb[j0:j0 + piece])
            pv = part if pv is None else pv + part
        acc_ref[hh, :, qs] = alpha * acc_ref[hh, :, qs] + pv

    def run_chains(items, next_tile):
        _, _, _, qw0, n_keys0, _ = items[0]
        s_next = carry_ref[0:n_keys0, 0:qw0]
        prev = None
        for i, (kj, hh, q0, qw, n_keys, mask_q0) in enumerate(items):
            s_cur = s_next
            if i + 1 < len(items):
                s_next = scores(*items[i + 1][:5])
            elif next_tile is not None:
                carry_ref[...] = scores(*wide(next_tile, 0)[0][:5])
            if mask_q0 is not None:
                s_cur = diag_mask(s_cur, mask_q0)
            cur = softmax(hh, q0, qw, s_cur)
            if prev is not None:
                values(*items[i - 1][:4], *prev)
            prev = cur
        values(*items[-1][:4], *prev)

    carry_ref[...] = scores(*wide(0, 0)[0][:5])

    @pl.loop(0, qi)
    def _(j):
        run_chains([c for d in range(n_sub) for b in range(n_sub) for c in wide(n_sub * j + d, b)],
                   n_sub * (j + 1))

    diag_items = []
    for d in range(n_sub):
        kj = n_sub * qi + d
        diag_items += narrow(kj, d) + [c for b in range(d + 1, n_sub) for c in wide(kj, b)]
    run_chains(diag_items, None)

    attn_t = jnp.concatenate([acc_ref[hh, 0:V_DIM, :] / acc_ref[hh, V_DIM:V_DIM + 1, :] for hh in range(2)],
                             axis=0)
    o_ref[...] = (ga_ref[...].astype(F32) * attn_t.T + rg_ref[...].astype(F32)).astype(BF16)


def _attention(q, k, vt, ga, rg, *, tq, tk):
    bsz, t, _ = q.shape
    n_vb, _, vblk = vt.shape[1:]
    assert LANES == 2 * V_DIM and N_HEADS % 2 == 0
    assert k.shape[1] == t == n_vb * vblk and tk % vblk == 0 and t % tq == 0 and tq % tk == 0
    assert tk % CHUNK == 0 and tk % MXU_DIM == 0
    n_hp = N_HEADS // 2
    qk_blk = 2 * HEAD_PAD

    def q_index(b, p, qi):
        return (b, qi, p)

    kern = functools.partial(_attn_kernel, tq=tq, tk=tk, cq=2 * MXU_DIM, vblk=vblk)
    blk_bytes = 2 * 2 * (tq * qk_blk + t * qk_blk + t * LANES + 3 * tq * LANES)
    tmp_bytes = 4 * 8 * tk * tk + 4 * 4 * tq * LANES
    return pl.pallas_call(
        kern,
        grid=(bsz, n_hp, t // tq),
        in_specs=[pl.BlockSpec((None, tq, qk_blk), q_index),
                  pl.BlockSpec((None, t, qk_blk), lambda b, p, qi: (b, 0, p)),
                  pl.BlockSpec((None, n_vb, LANES, vblk), lambda b, p, qi: (b, 0, p, 0)),
                  pl.BlockSpec((None, tq, LANES), q_index),
                  pl.BlockSpec((None, tq, LANES), q_index)],
        out_specs=pl.BlockSpec((None, tq, LANES), q_index),
        out_shape=jax.ShapeDtypeStruct((bsz, t, N_HEADS * V_DIM), BF16),
        scratch_shapes=[pltpu.VMEM((2, 1, tq), F32),
                        pltpu.VMEM((2, V_DIM + BF16_SUBLANES, tq), F32),
                        pltpu.VMEM((tk, tk), F32)],
        compiler_params=pltpu.CompilerParams(
            dimension_semantics=("parallel", "parallel", "arbitrary"),
            vmem_limit_bytes=_vmem_limit(blk_bytes + tmp_bytes)),
        name="attention",
    )(q, k, vt, ga, rg)


def _cache_attn_kernel(q_ref, latc_ref, krc_ref, latn_ref, krn_ref, ga_ref, rg_ref,
                       wkg_ref, wkc_ref, seg_ref, expand_ref, wv_ref, o_ref,
                       qabs_ref, qr_ref, m_ref, acc_ref, *, t, tk, n_new):
    past = latc_ref.shape[0]
    hq = N_HEADS * t
    for h in range(N_HEADS):
        qh = q_ref[:, h * HEAD_PAD:(h + 1) * HEAD_PAD]
        qabs_ref[h * t:(h + 1) * t, :] = _dot(qh, wkg_ref[h]).astype(BF16)
        qr_ref[h * t:(h + 1) * t, :] = qh[:, NOPE_DIM:QK_DIM]
    m_ref[...] = jnp.full(m_ref.shape, -jnp.inf, F32)
    acc_ref[...] = jnp.zeros(acc_ref.shape, F32)
    kv_lora = latc_ref.shape[1]

    def scores(lat, kr, n_valid):
        n = lat.shape[0]
        latb = lat.astype(BF16)
        kvk = _dot(latb, wkc_ref[...])
        ssum = _dot((kvk * kvk).astype(BF16), seg_ref[...])
        r = lax.rsqrt(ssum * (1.0 / NOPE_DIM) + EPS)
        r_hi = r.astype(BF16)
        r_lo = (r - r_hi.astype(F32)).astype(BF16)
        r_cols = _dot(jnp.concatenate([r_hi, r_lo], axis=1), expand_ref[...])
        s = _dot_nt(latb, qabs_ref[...]) * r_cols + _dot_nt(kr.astype(BF16), qr_ref[...])
        if n_valid < n:
            s = jnp.where(lax.broadcasted_iota(jnp.int32, (n, hq), 0) < n_valid, s, -jnp.inf)
        lat_t = jnp.concatenate([lat.T.astype(BF16), jnp.ones((acc_ref.shape[0] - kv_lora, n), BF16)], axis=0)
        return s, lat_t

    def accumulate(s, lat_t):
        m_prev = m_ref[...]
        m_new = jnp.maximum(m_prev, jnp.max(s, axis=0, keepdims=True))
        m_ref[...] = m_new
        p = jnp.exp2(s - m_new).astype(BF16)
        acc_ref[...] = jnp.exp2(m_prev - m_new) * acc_ref[...] + _dot(lat_t, p)

    def tile_scores(j):
        if j < past // tk:
            return scores(latc_ref[j * tk:(j + 1) * tk, :], krc_ref[j * tk:(j + 1) * tk, :], tk)
        return scores(latn_ref[...], krn_ref[...], n_new)

    n_tiles = past // tk + 1
    cur = tile_scores(0)
    for j in range(n_tiles):
        nxt = tile_scores(j + 1) if j + 1 < n_tiles else None
        accumulate(*cur)
        cur = nxt

    ctx = (acc_ref[0:kv_lora, :] / acc_ref[kv_lora:kv_lora + 1, :]).T.astype(BF16)
    first_head = lax.broadcasted_iota(jnp.int32, (t, LANES), 1) < V_DIM
    for pair in range(N_HEADS // 2):
        cols = slice(pair * LANES, (pair + 1) * LANES)
        wv = wv_ref[:, cols]
        a0 = _dot(ctx[(2 * pair) * t:(2 * pair + 1) * t], wv)
        a1 = _dot(ctx[(2 * pair + 1) * t:(2 * pair + 2) * t], wv)
        attn = jnp.where(first_head, a0, a1)
        o_ref[:, cols] = (ga_ref[:, cols].astype(F32) * attn + rg_ref[:, cols].astype(F32)).astype(BF16)


def _cache_attention(q, lat_cache, kr_cache, lat_new, kr_new, ga, rg, wts, *, tk, n_new):
    bsz, t, qk_w = q.shape
    past, kv_lora = lat_cache.shape[1:]
    n_pad = lat_new.shape[1]
    d_model = ga.shape[2]
    hq = N_HEADS * t
    assert past % tk == 0 and LANES == 2 * V_DIM
    w_list = [wts["wkg"], wts["wkc"], wts["seg"], wts["expand"], wts["wv"]]
    w_bytes = sum(w.size * w.dtype.itemsize for w in w_list)
    blk_bytes = 2 * (4 * (past + n_pad) * (kv_lora + LANES) + 2 * t * (qk_w + 3 * d_model))
    tmp_bytes = 4 * tk * (2 * N_HEADS * NOPE_DIM + 4 * hq)

    def batch_spec(r, width):
        return pl.BlockSpec((None, r, width), lambda b: (b, 0, 0))

    kern = functools.partial(_cache_attn_kernel, t=t, tk=tk, n_new=n_new)
    return pl.pallas_call(
        kern,
        grid=(bsz,),
        in_specs=[batch_spec(t, qk_w), batch_spec(past, kv_lora), batch_spec(past, ROPE_DIM),
                  batch_spec(n_pad, kv_lora), batch_spec(n_pad, ROPE_DIM),
                  batch_spec(t, d_model), batch_spec(t, d_model)] + [_const_spec(w.shape) for w in w_list],
        out_specs=batch_spec(t, d_model),
        out_shape=jax.ShapeDtypeStruct((bsz, t, d_model), BF16),
        scratch_shapes=[pltpu.VMEM((hq, kv_lora), BF16), pltpu.VMEM((hq, ROPE_DIM), BF16),
                        pltpu.VMEM((1, hq), F32), pltpu.VMEM((kv_lora + BF16_SUBLANES, hq), F32)],
        compiler_params=pltpu.CompilerParams(
            dimension_semantics=("parallel",),
            vmem_limit_bytes=_vmem_limit(w_bytes + blk_bytes + tmp_bytes)),
        name="cache_attention",
    )(q, lat_cache, kr_cache, lat_new, kr_new, ga, rg, *w_list)


def _mixer_out_kernel(x_ref, mix_ref, hist_ref, w_out_ref, g_ffn_ref, w_up_ref, fw_ref, fb_ref, w_down_ref,
                      y_ref, fst_ref, upc_ref, *, rows, nb, hp, d_model, d_ff, conv_w, col_blk):
    t = pl.program_id(1)

    @pl.when(t == 0)
    def _():
        upc_ref[0:hp, :] = hist_ref[...]

    x1 = x_ref[...] + _dot(mix_ref[...], w_out_ref[...])
    xn = (x1 * _rms_rows(x1, d_model) * g_ffn_ref[...]).astype(BF16)

    def up_proj(c):
        for c0 in (c * col_blk, d_ff + c * col_blk):
            sl = slice(c0, c0 + col_blk)
            upc_ref[hp:hp + rows, sl] = _dot(xn, w_up_ref[:, sl])

    def conv_cols(c0):
        sl = slice(c0, c0 + col_blk)
        out = fb_ref[:, sl] + fw_ref[conv_w - 1:conv_w, sl] * upc_ref[hp:hp + rows, sl]
        for j in range(1, conv_w):
            out = out + fw_ref[conv_w - 1 - j:conv_w - j, sl] * upc_ref[hp - j * nb:hp - j * nb + rows, sl]
        return out

    def gated(c):
        gate = conv_cols(c * col_blk)
        val = conv_cols(d_ff + c * col_blk)
        return (gate * _sigmoid(gate) * val).astype(BF16)

    def down_proj(c, hmid):
        return _dot(hmid, w_down_ref[c * col_blk:(c + 1) * col_blk, :])

    n_chunks = d_ff // col_blk
    y = x1
    up_proj(0)
    prev = None
    for c in range(n_chunks):
        if c + 1 < n_chunks:
            up_proj(c + 1)
        hmid = gated(c)
        if prev is not None:
            y = y + down_proj(c - 1, prev)
        prev = hmid
    y_ref[...] = y + down_proj(n_chunks - 1, prev)

    tail = upc_ref[rows:rows + hp, :]
    fst_ref[...] = tail
    upc_ref[0:hp, :] = tail


def _mixer_out(x, mixed, hist, wts, *, rows, nb):
    groups, tg, d_model = x.shape
    hp = hist.shape[1]
    d_ff = wts["w_down"].shape[0]
    conv_w = wts["fw"].shape[0]
    n_t = tg // rows
    col_blk = MXU_DIM

    def row_spec(width):
        return pl.BlockSpec((None, rows, width), lambda g, t: (g, t, 0))

    def group_spec(r, width):
        return pl.BlockSpec((None, r, width), lambda g, t: (g, 0, 0))

    w_names = ["w_out", "g_ffn", "w_up", "fw", "fb", "w_down"]
    w_list = [wts[n] for n in w_names]
    w_bytes = sum(w.size * w.dtype.itemsize for w in w_list)
    blk_bytes = 2 * rows * d_model * (4 + 2 + 4) + 4 * 4 * hp * 2 * d_ff
    tmp_bytes = 4 * (hp + rows) * 2 * d_ff + 4 * rows * (3 * d_model + 8 * col_blk)
    kern = functools.partial(_mixer_out_kernel, rows=rows, nb=nb, hp=hp, d_model=d_model, d_ff=d_ff,
                             conv_w=conv_w, col_blk=col_blk)
    return pl.pallas_call(
        kern,
        grid=(groups, n_t),
        in_specs=[row_spec(d_model), row_spec(d_model), group_spec(hp, 2 * d_ff)]
                 + [_const_spec(w.shape) for w in w_list],
        out_specs=(row_spec(d_model), group_spec(hp, 2 * d_ff)),
        out_shape=(jax.ShapeDtypeStruct((groups, tg, d_model), F32),
                   jax.ShapeDtypeStruct((groups, hp, 2 * d_ff), F32)),
        scratch_shapes=[pltpu.VMEM((hp + rows, 2 * d_ff), F32)],
        compiler_params=pltpu.CompilerParams(
            dimension_semantics=("parallel", "arbitrary"),
            vmem_limit_bytes=_vmem_limit(w_bytes + blk_bytes + tmp_bytes)),
        name="mixer_out",
    )(x, mixed, hist, *w_list)


def _head_pad(w, widths):
    lead = w.shape[:-1]
    per_head = sum(widths)
    w = w.reshape(lead + (N_HEADS, per_head))
    w = jnp.pad(w, [(0, 0)] * len(lead) + [(0, 0), (0, HEAD_PAD - per_head)])
    return w.reshape(lead + (N_HEADS * HEAD_PAD,))


def _prep_weights(l, g_mix_norm, w_in, g_q_a, w_q_b, g_kv_a, w_kv_b, g_qn, g_qr, g_kn, g_kr,
                  lru_conv_w, lru_conv_b, w_rg, b_rg, w_ig, b_ig, lru_lambda, w_out, g_ffn_norm,
                  w_up, ffn_conv_w, ffn_conv_b, w_down):
    d_model = w_in.shape[1]
    q_lora = w_q_b.shape[1]
    kv_lora = w_kv_b.shape[1]
    row = lambda a: a.reshape(1, -1)
    o_kr = q_lora + kv_lora
    wi = w_in[l].astype(BF16)
    w_kr = jnp.pad(wi[:, o_kr:o_kr + ROPE_DIM], ((0, 0), (NOPE_DIM, HEAD_PAD - QK_DIM)))
    w_in_p = jnp.concatenate([wi[:, :o_kr], w_kr, wi[:, o_kr + ROPE_DIM:]], axis=1)
    scale = QK_DIM ** -0.5 * LOG2_E
    gq = jnp.pad(jnp.concatenate([g_qn[l], g_qr[l]]) * scale, (0, HEAD_PAD - QK_DIM))
    g_rope = g_qr[l] * scale
    gqx = jnp.pad(jnp.concatenate([g_rope[HALF_ROPE:], g_rope[:HALF_ROPE]]), (NOPE_DIM, HEAD_PAD - QK_DIM))
    wq3 = w_q_b[l].reshape(q_lora, N_HEADS, QK_DIM)
    wqx = jnp.concatenate([jnp.zeros((q_lora, N_HEADS, NOPE_DIM), F32), wq3[:, :, NOPE_DIM + HALF_ROPE:],
                           wq3[:, :, NOPE_DIM:NOPE_DIM + HALF_ROPE]], axis=2).reshape(q_lora, N_HEADS * QK_DIM)
    gk = jnp.tile(jnp.pad(g_kn[l], (0, HEAD_PAD - NOPE_DIM)), N_HEADS)
    kv = w_kv_b[l].reshape(kv_lora, N_HEADS, NOPE_DIM + V_DIM)
    wk = _head_pad(kv[:, :, :NOPE_DIM].reshape(kv_lora, N_HEADS * NOPE_DIM), (NOPE_DIM,))
    wv = kv[:, :, NOPE_DIM:].reshape(kv_lora, N_HEADS * V_DIM)
    wkg = jnp.pad(jnp.transpose(kv[:, :, :NOPE_DIM], (1, 2, 0)) * g_kn[l][None, :, None],
                  ((0, 0), (0, HEAD_PAD - NOPE_DIM), (0, 0)))
    seg = jnp.pad(jnp.repeat(jnp.eye(N_HEADS, dtype=F32), NOPE_DIM, axis=0), ((0, 0), (0, LANES - N_HEADS)))
    return {
        "wkg": wkg.astype(BF16), "wkc": kv[:, :, :NOPE_DIM].reshape(kv_lora, -1).astype(BF16),
        "seg": seg.astype(BF16), "wv": wv.astype(BF16),
        "g_mix": row(g_mix_norm[l]), "w_in": w_in_p, "g_qa": row(g_q_a[l]),
        "wq": _head_pad(w_q_b[l], (NOPE_DIM, ROPE_DIM)).astype(BF16), "gq": row(gq), "gqx": row(gqx),
        "wqx": _head_pad(wqx, (NOPE_DIM, ROPE_DIM)).astype(BF16),
        "g_kva": row(g_kv_a[l]), "g_kr": row(jnp.pad(g_kr[l], (NOPE_DIM, HEAD_PAD - QK_DIM))),
        "wk": wk.astype(BF16), "gk": row(gk), "wvt": wv.T.astype(BF16),
        "cw": lru_conv_w[l], "cb": row(lru_conv_b[l]),
        "wlru": jnp.concatenate([w_rg[l], w_ig[l]], axis=-1).astype(BF16),
        "brg": row(b_rg[l]), "big": row(b_ig[l]), "lam": row(lru_lambda[l]),
        "w_out": w_out[l].astype(BF16), "g_ffn": row(g_ffn_norm[l]), "w_up": w_up[l].astype(BF16),
        "fw": ffn_conv_w[l], "fb": row(ffn_conv_b[l]), "w_down": w_down[l].astype(BF16),
    }


def _rope_tables(pos):
    inv = np.float32(ROPE_THETA) ** (-np.arange(0, ROPE_DIM, 2, dtype=np.float32) / np.float32(ROPE_DIM))
    ang = pos.astype(np.float32)[:, None] * inv[None, :]
    cos = np.cos(ang.astype(np.float64)).astype(np.float32)
    sin = np.sin(ang.astype(np.float64)).astype(np.float32)
    n = pos.shape[0]
    ones_lo = np.ones((n, NOPE_DIM), np.float32)
    zeros_lo = np.zeros((n, NOPE_DIM), np.float32)
    zeros_half = np.zeros((n, HALF_ROPE), np.float32)
    tail = np.zeros((n, HEAD_PAD - QK_DIM), np.float32)
    c = np.concatenate([ones_lo, cos, cos, tail], axis=1)
    s_lo = np.concatenate([zeros_lo, -sin, zeros_half, tail], axis=1)
    s_hi = np.concatenate([zeros_lo, zeros_half, sin, tail], axis=1)
    return jnp.asarray(c), jnp.asarray(s_lo), jnp.asarray(s_hi)


def _expand_matrix(t):
    one_part = jnp.pad(jnp.repeat(jnp.eye(N_HEADS, dtype=F32), t, axis=1), ((0, LANES - N_HEADS), (0, 0)))
    return jnp.concatenate([one_part, one_part], axis=0).astype(BF16)


def _front_pad_rows(a, hp):
    return jnp.pad(a, ((0, 0), (hp - a.shape[1], 0), (0, 0)))


def _layer_prompt(x, wts, *, rows, out_rows, attn_tq, attn_tk):
    bsz, t, d_model = x.shape
    lru_w = wts["cw"].shape[0]
    ffn_w = wts["fw"].shape[0]
    d_ff2 = wts["w_up"].shape[1]
    hp1 = _round_up(lru_w - 1, SUBLANES)
    hp2 = _round_up(ffn_w - 1, SUBLANES)
    rope = _rope_tables(np.arange(t, dtype=np.int32))
    q, k, vt, lat, kr, ga, rg, h_last, cst = _mixer_in(
        x, jnp.zeros((bsz, hp1, d_model), F32), jnp.zeros((bsz, 1, d_model), F32), rope, wts, rows=rows, nb=1)
    mixed = _attention(q, k, vt, ga, rg, tq=attn_tq, tk=attn_tk)
    y, fst = _mixer_out(x, mixed, jnp.zeros((bsz, hp2, d_ff2), F32), wts, rows=out_rows, nb=1)
    return y, (lat, kr, h_last[:, 0], cst[:, hp1 - (lru_w - 1):], fst[:, hp2 - (ffn_w - 1):])


def _layer_sample(x, past_lat, past_kr, h0, lru_buf, ffn_buf, wts, *, tk):
    bsz, t, d_model = x.shape
    past = past_lat.shape[1]
    lru_w = wts["cw"].shape[0]
    ffn_w = wts["fw"].shape[0]
    hp1 = _round_up((lru_w - 1) * bsz, SUBLANES)
    hp2 = _round_up((ffn_w - 1) * bsz, SUBLANES)
    rows = t * bsz

    def to_tm(a):
        return jnp.swapaxes(a, 0, 1).reshape(1, a.shape[1] * bsz, a.shape[2])

    def from_tm(a):
        return jnp.swapaxes(a.reshape(a.shape[1] // bsz, bsz, a.shape[2]), 0, 1)

    rope = _rope_tables(np.repeat(past + np.arange(t, dtype=np.int32), bsz))
    q, k, vt, lat, kr, ga, rg, h_last, cst = _mixer_in(
        to_tm(x), _front_pad_rows(to_tm(lru_buf), hp1), h0[None], rope, wts, rows=rows, nb=bsz)

    assert (past % CHUNK) + t <= CHUNK, "cache attention assumes all keys visible to all queries"
    n_pad = _round_up(t, LANES)
    pad_rows = lambda a: jnp.pad(from_tm(a), ((0, 0), (0, n_pad - t), (0, 0)))
    cache_wts = dict(wts, expand=_expand_matrix(t))
    mixed = _cache_attention(from_tm(q), past_lat, past_kr, pad_rows(lat), pad_rows(kr),
                             from_tm(ga), from_tm(rg), cache_wts, tk=tk, n_new=t)

    y, fst = _mixer_out(to_tm(x), to_tm(mixed), _front_pad_rows(to_tm(ffn_buf), hp2), wts, rows=rows, nb=bsz)
    states = (from_tm(lat), from_tm(kr), h_last[0],
              from_tm(cst[:, hp1 - (lru_w - 1) * bsz:]), from_tm(fst[:, hp2 - (ffn_w - 1) * bsz:]))
    return from_tm(y), states


def kernel(x_prompt, x_sample, cache_kv_latent, cache_k_rope, state_lru_h, state_lru_conv, state_ffn_conv,
           g_mix_norm, w_in, g_q_a, w_q_b, g_kv_a, w_kv_b, g_qn, g_qr, g_kn, g_kr, lru_conv_w, lru_conv_b,
           w_rg, b_rg, w_ig, b_ig, lru_lambda, w_out, g_ffn_norm, w_up, ffn_conv_w, ffn_conv_b, w_down):
    depth = w_in.shape[0]
    yp, ys = x_prompt, x_sample
    p_states, s_states = [], []
    for l in range(depth):
        wts = _prep_weights(l, g_mix_norm, w_in, g_q_a, w_q_b, g_kv_a, w_kv_b, g_qn, g_qr, g_kn, g_kr,
                            lru_conv_w, lru_conv_b, w_rg, b_rg, w_ig, b_ig, lru_lambda, w_out, g_ffn_norm,
                            w_up, ffn_conv_w, ffn_conv_b, w_down)
        yp, st_p = _layer_prompt(yp, wts, rows=512, out_rows=512, attn_tq=4096, attn_tk=512)
        ys, st_s = _layer_sample(ys, cache_kv_latent[l], cache_k_rope[l], state_lru_h[l],
                                 state_lru_conv[l], state_ffn_conv[l], wts, tk=1024)
        p_states.append(st_p)
        s_states.append(st_s)
    p_out = [jnp.stack([st[j] for st in p_states], axis=0) for j in range(5)]
    s_out = [jnp.stack([st[j] for st in s_states], axis=0) for j in range(5)]
    return (yp, ys, *p_out, *s_out)
```
